```python
import math
import jax
import jax.numpy as jnp
from jax import lax
import numpy as np

D_MODEL = 1024
BATCH = 2
SEQ = 8192
DEPTH = 2

N_MIXERS = 2
CONV_WIDTH = 3
N_HEADS = 8
HEAD_DIM = 64
V_DIM = 2 * HEAD_DIM
ROPE_THETA = 500000.0
ROT_DIM = HEAD_DIM // 4
Q_BLOCK = 128
D_FF = 2816
N_EXPERTS = 8
TOP_K = 2
D_FF_EXPERT = 3584
RMS_EPS = 1e-6
N_CONV_LAYERS = (DEPTH + 1) // 2
N_ATTN_LAYERS = DEPTH // 2

kernel_name = 'hybrid_shortconv_diffattn_moe_encoder'


def rmsnorm(x, g):
    xf = x.astype(jnp.float32)
    y = xf * lax.rsqrt(jnp.mean(xf * xf, axis=-1, keepdims=True) + RMS_EPS)
    return (y * g.astype(jnp.float32)).astype(x.dtype)


def swiglu(h, w1, w3, w2):
    return (jax.nn.silu(h @ w1) * (h @ w3)) @ w2


def short_conv_mixer(h, w_in, conv_w, w_out):
    S = h.shape[1]
    b_gate, c_gate, v = jnp.split(h @ w_in, 3, axis=-1)
    u = c_gate * v
    pad = CONV_WIDTH // 2
    up = jnp.pad(u, ((0, 0), (pad, pad), (0, 0)))
    conv = sum(up[:, j:j + S] * conv_w[j] for j in range(CONV_WIDTH))
    return (b_gate * conv) @ w_out


def rope_partial(t, cos, sin):
    half = ROT_DIM // 2
    t1 = t[..., :half]
    t2 = t[..., half:ROT_DIM]
    return jnp.concatenate([t1 * cos - t2 * sin, t2 * cos + t1 * sin, t[..., ROT_DIM:]], axis=-1)


def diff_attention(h, positions, w_in, q_norm, k_norm, lam_q1, lam_k1, lam_q2, lam_k2, subln, w_out, lambda_init):
    Bsz, S, _ = h.shape
    q, k, v = jnp.split(h @ w_in, 3, axis=-1)
    q = rmsnorm(q.reshape(Bsz, S, N_HEADS, 2, HEAD_DIM), q_norm)
    k = rmsnorm(k.reshape(Bsz, S, N_HEADS, 2, HEAD_DIM), k_norm)
    v = v.reshape(Bsz, S, N_HEADS, V_DIM)
    inv_freq = ROPE_THETA ** (-jnp.arange(0, ROT_DIM, 2, dtype=jnp.float32) / ROT_DIM)
    ang = positions.astype(jnp.float32)[..., None] * inv_freq
    cos = jnp.cos(ang)[:, :, None, None, :].astype(h.dtype)
    sin = jnp.sin(ang)[:, :, None, None, :].astype(h.dtype)
    q = rope_partial(q, cos, sin) * (HEAD_DIM ** -0.5)
    k = rope_partial(k, cos, sin)
    lam = (jnp.exp(jnp.sum(lam_q1.astype(jnp.float32) * lam_k1.astype(jnp.float32)))
           - jnp.exp(jnp.sum(lam_q2.astype(jnp.float32) * lam_k2.astype(jnp.float32)))
           + lambda_init)
    n_blk = S // Q_BLOCK
    qb = q.reshape(Bsz, n_blk, Q_BLOCK, N_HEADS, 2, HEAD_DIM).transpose(1, 0, 2, 3, 4, 5)

    def block(q_blk):
        s = jnp.einsum('bqhcd,bkhcd->bhcqk', q_blk, k).astype(jnp.float32)
        p = jax.nn.softmax(s, axis=-1)
        a = p[:, :, 0] - lam * p[:, :, 1]
        return jnp.einsum('bhqk,bkhe->bqhe', a.astype(v.dtype), v)

    o = lax.map(block, qb)
    o = o.transpose(1, 0, 2, 3, 4).reshape(Bsz, S, N_HEADS, V_DIM)
    o = rmsnorm(o, subln) * (1.0 - lambda_init)
    return o.reshape(Bsz, S, N_HEADS * V_DIM) @ w_out


def moe_swiglu(h, router, w1, w3, w2):
    Bsz, S, D = h.shape
    t = h.reshape(Bsz * S, D)
    logits = (t @ router).astype(jnp.float32)
    top_v, top_i = lax.top_k(logits, TOP_K)
    gates = jax.nn.softmax(top_v, axis=-1)
    combine = jnp.sum(jax.nn.one_hot(top_i, N_EXPERTS, dtype=jnp.float32) * gates[..., None], axis=1)
    out = jnp.zeros_like(t)
    for e in range(N_EXPERTS):
        out = out + combine[:, e:e + 1].astype(t.dtype) * swiglu(t, w1[e], w3[e], w2[e])
    return out.reshape(Bsz, S, D)


def setup_inputs(seed: int = 0) -> dict:
    key = jax.random.key(seed)
    ks = iter(jax.random.split(key, 32))
    D = D_MODEL
    QKV = 2 * N_HEADS * HEAD_DIM * 2 + N_HEADS * V_DIM

    def nrm(shape, scale):
        return jax.random.normal(next(ks), shape, jnp.float32) * scale

    def gain(shape):
        return 1.0 + 0.05 * jax.random.normal(next(ks), shape, jnp.float32)

    return {
        'x': jax.random.normal(next(ks), (BATCH, SEQ, D), jnp.float32),
        'positions': jnp.broadcast_to(jnp.arange(SEQ, dtype=jnp.int32), (BATCH, SEQ)),
        'norm_mix': gain((DEPTH, D)),
        'norm_ffn': gain((DEPTH, D)),
        'conv_in': nrm((N_CONV_LAYERS, D, 3 * D), D ** -0.5),
        'conv_w': nrm((N_CONV_LAYERS, CONV_WIDTH, D), CONV_WIDTH ** -0.5),
        'conv_out': nrm((N_CONV_LAYERS, D, D), D ** -0.5),
        'attn_in': nrm((N_ATTN_LAYERS, D, QKV), D ** -0.5),
        'q_norm': gain((N_ATTN_LAYERS, HEAD_DIM)),
        'k_norm': gain((N_ATTN_LAYERS, HEAD_DIM)),
        'lam_q1': nrm((N_ATTN_LAYERS, HEAD_DIM), 0.1),
        'lam_k1': nrm((N_ATTN_LAYERS, HEAD_DIM), 0.1),
        'lam_q2': nrm((N_ATTN_LAYERS, HEAD_DIM), 0.1),
        'lam_k2': nrm((N_ATTN_LAYERS, HEAD_DIM), 0.1),
        'subln': gain((N_ATTN_LAYERS, V_DIM)),
        'attn_out': nrm((N_ATTN_LAYERS, N_HEADS * V_DIM, D), (N_HEADS * V_DIM) ** -0.5),
        'ffn_w1': nrm((N_CONV_LAYERS, D, D_FF), D ** -0.5),
        'ffn_w3': nrm((N_CONV_LAYERS, D, D_FF), D ** -0.5),
        'ffn_w2': nrm((N_CONV_LAYERS, D_FF, D), D_FF ** -0.5),
        'router': nrm((N_ATTN_LAYERS, D, N_EXPERTS), D ** -0.5),
        'moe_w1': nrm((N_ATTN_LAYERS, N_EXPERTS, D, D_FF_EXPERT), D ** -0.5),
        'moe_w3': nrm((N_ATTN_LAYERS, N_EXPERTS, D, D_FF_EXPERT), D ** -0.5),
        'moe_w2': nrm((N_ATTN_LAYERS, N_EXPERTS, D_FF_EXPERT, D), D_FF_EXPERT ** -0.5),
    }


def reference(x, positions, norm_mix, norm_ffn, conv_in, conv_w, conv_out, attn_in, q_norm, k_norm,
              lam_q1, lam_k1, lam_q2, lam_k2, subln, attn_out, ffn_w1, ffn_w3, ffn_w2,
              router, moe_w1, moe_w3, moe_w2):
    for i in range(DEPTH):
        j = i // N_MIXERS
        h = rmsnorm(x, norm_mix[i])
        if i % N_MIXERS == 0:
            x = x + short_conv_mixer(h, conv_in[j], conv_w[j], conv_out[j])
        else:
            lambda_init = 0.8 - 0.6 * math.exp(-0.3 * i)
            x = x + diff_attention(h, positions, attn_in[j], q_norm[j], k_norm[j], lam_q1[j], lam_k1[j],
                                   lam_q2[j], lam_k2[j], subln[j], attn_out[j], lambda_init)
        h = rmsnorm(x, norm_ffn[i])
        f = i // 2
        if i % 2 == 0:
            x = x + swiglu(h, ffn_w1[f], ffn_w3[f], ffn_w2[f])
        else:
            x = x + moe_swiglu(h, router[f], moe_w1[f], moe_w3[f], moe_w2[f])
    return x
```

```python
import functools
import math

import jax
import jax.numpy as jnp
from jax import lax
from jax.experimental import pallas as pl
from jax.experimental.pallas import tpu as pltpu

F32 = jnp.float32
BF16 = jnp.bfloat16

D_MODEL = 1024
N_HEADS = 8
HEAD_DIM = 64
V_DIM = 2 * HEAD_DIM
ROT_DIM = HEAD_DIM // 4
ROPE_THETA = 500000.0
N_EXPERTS = 8
RMS_EPS = 1e-6
LOG2E = 1.4426950408889634

TOKEN_TILE = 512
ATTN_KV_TILE = 512
MOE_TILE = 512
FF_CHUNK = 512
GATHER_TILE = 256
VMEM_LIMIT = 56 * 1024 * 1024

_NEG_BIG = -1e30


def _cparams(*sem):
    return pltpu.CompilerParams(dimension_semantics=sem, vmem_limit_bytes=VMEM_LIMIT)


def _rms_rows(x, g):
    ms = jnp.mean(x * x, axis=-1, keepdims=True)
    return x * lax.rsqrt(ms + RMS_EPS) * g


def _dot(a, b):
    return jnp.dot(a, b, preferred_element_type=F32)


def _silu(a):
    return a * (1.0 / (1.0 + jnp.exp(-a)))


def _ff_chunks(width):
    out, f0 = [], 0
    while f0 < width:
        fl = min(FF_CHUNK, width - f0)
        out.append((f0, fl))
        f0 += fl
    return out


def _conv_in_kernel(x_ref, g_ref, w_ref, b_ref, u_ref):
    d = x_ref.shape[1]
    h = _rms_rows(x_ref[...], g_ref[...]).astype(BF16)
    b_ref[...] = _dot(h, w_ref[:, 0:d]).astype(BF16)
    c = _dot(h, w_ref[:, d:2 * d])
    v = _dot(h, w_ref[:, 2 * d:3 * d])
    u_ref[...] = (c * v).astype(BF16)


def _conv_out_kernel(tiles_per_seq, u_ref, up_ref, un_ref, b_ref, x_ref, cw_ref, w_ref, o_ref):
    i = pl.program_id(0)
    tm = u_ref.shape[0]
    halo = up_ref.shape[0]
    u = u_ref[...].astype(F32)
    first = (i % tiles_per_seq) == 0
    last = (i % tiles_per_seq) == tiles_per_seq - 1
    prev_row = jnp.where(first, 0.0, up_ref[halo - 1:halo, :].astype(F32))
    next_row = jnp.where(last, 0.0, un_ref[0:1, :].astype(F32))
    rows = lax.broadcasted_iota(jnp.int32, (tm, 1), 0)
    u_m1 = jnp.where(rows == 0, prev_row, pltpu.roll(u, 1, 0))
    u_p1 = jnp.where(rows == tm - 1, next_row, pltpu.roll(u, tm - 1, 0))
    cw = cw_ref[...]
    conv = u_m1 * cw[0:1, :] + u * cw[1:2, :] + u_p1 * cw[2:3, :]
    y = (b_ref[...].astype(F32) * conv).astype(BF16)
    o_ref[...] = x_ref[...] + _dot(y, w_ref[...])


def _conv_mixer(x, g, w_in, conv_w, w_out, seq):
    t, d = x.shape
    tm = TOKEN_TILE
    nt = t // tm
    halo = 16
    b, u = pl.pallas_call(
        _conv_in_kernel,
        grid=(nt,),
        in_specs=[
            pl.BlockSpec((tm, d), lambda i: (i, 0)),
            pl.BlockSpec((1, d), lambda i: (0, 0)),
            pl.BlockSpec((d, 3 * d), lambda i: (0, 0)),
        ],
        out_specs=[pl.BlockSpec((tm, d), lambda i: (i, 0))] * 2,
        out_shape=[jax.ShapeDtypeStruct((t, d), BF16)] * 2,
        compiler_params=_cparams("parallel"),
        name="conv_in",
    )(x, g, w_in)
    hb = tm // halo
    nhb = t // halo
    return pl.pallas_call(
        functools.partial(_conv_out_kernel, seq // tm),
        grid=(nt,),
        in_specs=[
            pl.BlockSpec((tm, d), lambda i: (i, 0)),
            pl.BlockSpec((halo, d), lambda i: (jnp.maximum(i * hb - 1, 0), 0)),
            pl.BlockSpec((halo, d), lambda i: (jnp.minimum((i + 1) * hb, nhb - 1), 0)),
            pl.BlockSpec((tm, d), lambda i: (i, 0)),
            pl.BlockSpec((tm, d), lambda i: (i, 0)),
            pl.BlockSpec((3, d), lambda i: (0, 0)),
            pl.BlockSpec((d, d), lambda i: (0, 0)),
        ],
        out_specs=pl.BlockSpec((tm, d), lambda i: (i, 0)),
        out_shape=jax.ShapeDtypeStruct((t, d), F32),
        compiler_params=_cparams("parallel"),
        name="conv_out",
    )(u, u, u, b, x, conv_w, w_out)


def _ffn_kernel(x_ref, g_ref, w1_ref, w3_ref, w2_ref, o_ref):
    x = x_ref[...]
    h = _rms_rows(x, g_ref[...]).astype(BF16)
    acc = x
    for f0, fl in _ff_chunks(w1_ref.shape[1]):
        a = _dot(h, w1_ref[:, f0:f0 + fl])
        b = _dot(h, w3_ref[:, f0:f0 + fl])
        act = (_silu(a) * b).astype(BF16)
        acc = acc + _dot(act, w2_ref[f0:f0 + fl, :])
    o_ref[...] = acc


def _dense_ffn(x, g, w1, w3, w2):
    t, d = x.shape
    f = w1.shape[1]
    tm = TOKEN_TILE
    return pl.pallas_call(
        _ffn_kernel,
        grid=(t // tm,),
        in_specs=[
            pl.BlockSpec((tm, d), lambda i: (i, 0)),
            pl.BlockSpec((1, d), lambda i: (0, 0)),
            pl.BlockSpec((d, f), lambda i: (0, 0)),
            pl.BlockSpec((d, f), lambda i: (0, 0)),
            pl.BlockSpec((f, d), lambda i: (0, 0)),
        ],
        out_specs=pl.BlockSpec((tm, d), lambda i: (i, 0)),
        out_shape=jax.ShapeDtypeStruct((t, d), F32),
        compiler_params=_cparams("parallel"),
        name="dense_ffn",
    )(x, g, w1, w3, w2)


def _qkv_kernel(x_ref, g_ref, wt_ref, pos_ref, freq_ref, qn_ref, kn_ref, q_ref, k_ref, v_ref):
    tm, d = x_ref.shape
    h = _rms_rows(x_ref[...], g_ref[...]).astype(BF16)
    nt = (((1,), (1,)), ((), ()))
    ang = pos_ref[...].astype(F32) * freq_ref[...]
    cos = jnp.cos(ang)[None]
    sin = jnp.sin(ang)[None]
    half = ROT_DIM // 2
    groups = d // HEAD_DIM

    def norm_rope(w_rows, gain, scale):
        t = lax.dot_general(w_rows, h, nt, preferred_element_type=F32)
        t = t.reshape(groups, HEAD_DIM, tm)
        ms = jnp.mean(t * t, axis=1, keepdims=True)
        t = t * lax.rsqrt(ms + RMS_EPS) * gain[None]
        t1 = t[:, 0:half, :]
        t2 = t[:, half:ROT_DIM, :]
        t = jnp.concatenate([t1 * cos - t2 * sin, t2 * cos + t1 * sin, t[:, ROT_DIM:, :]], axis=1)
        return (t * scale).reshape(d, tm)

    q = norm_rope(wt_ref[0:d, :], qn_ref[...], (HEAD_DIM ** -0.5) * LOG2E)
    q_ref[0] = q.astype(BF16)
    k = norm_rope(wt_ref[d:2 * d, :], kn_ref[...], 1.0)
    k_ref[...] = k.T.astype(BF16)
    v = lax.dot_general(wt_ref[2 * d:3 * d, :], h, nt, preferred_element_type=F32)
    v_ref[0] = v.astype(BF16)


def _attn_kernel(lambda_init, q_ref, k_ref, v_ref, lam_ref, sub_ref, o_ref, acc1_ref, acc2_ref):
    tq = q_ref.shape[2]
    nkb = v_ref.shape[0]
    tk = v_ref.shape[2]
    q = q_ref[0]
    rows = lax.broadcasted_iota(jnp.int32, (V_DIM, 1), 0)
    zero = jnp.zeros_like(q)
    q1 = jnp.where(rows < HEAD_DIM, q, zero)
    q2 = jnp.where(rows >= HEAD_DIM, q, zero)
    acc1_ref[...] = jnp.zeros_like(acc1_ref)
    acc2_ref[...] = jnp.zeros_like(acc2_ref)

    def one_map(k, v, qc, m, l, acc_ref):
        s = _dot(k, qc)
        m_new = jnp.maximum(m, jnp.max(s, axis=0, keepdims=True))
        alpha = jnp.exp2(m - m_new)
        p = jnp.exp2(s - m_new)
        l_new = alpha * l + jnp.sum(p, axis=0, keepdims=True)
        acc_ref[...] = alpha * acc_ref[...] + _dot(v, p.astype(BF16))
        return m_new, l_new

    def body(kb, carry):
        m1, l1, m2, l2 = carry
        k = k_ref[pl.ds(pl.multiple_of(kb * tk, tk), tk), :]
        v = v_ref[kb]
        m1, l1 = one_map(k, v, q1, m1, l1, acc1_ref)
        m2, l2 = one_map(k, v, q2, m2, l2, acc2_ref)
        return m1, l1, m2, l2

    neg = jnp.full((1, tq), _NEG_BIG, F32)
    zer = jnp.zeros((1, tq), F32)
    _, l1, _, l2 = lax.fori_loop(0, nkb, body, (neg, zer, neg, zer))

    lp = lam_ref[...]
    lam = (jnp.exp(jnp.sum(lp[0:1, :] * lp[1:2, :], axis=-1, keepdims=True))
           - jnp.exp(jnp.sum(lp[2:3, :] * lp[3:4, :], axis=-1, keepdims=True))
           + lambda_init)
    o = acc1_ref[...] / l1 - lam * (acc2_ref[...] / l2)
    ms = jnp.mean(o * o, axis=0, keepdims=True)
    o = o * lax.rsqrt(ms + RMS_EPS) * sub_ref[...] * (1.0 - lambda_init)
    o_ref[0] = o.astype(BF16)


def _attn_out_router_kernel(o_ref, x_ref, w_ref, g_ref, rt_ref, x2_ref, h2_ref, idx_ref, gate_ref):
    tn = (((0,), (0,)), ((), ()))
    y = lax.dot_general(o_ref[0], w_ref[...], tn, preferred_element_type=F32)
    x2 = x_ref[...] + y
    x2_ref[...] = x2
    h2 = _rms_rows(x2, g_ref[...])
    h2_ref[...] = h2
    nt = (((1,), (1,)), ((), ()))
    logits = lax.dot_general(rt_ref[...], h2, nt, precision=lax.Precision.HIGHEST,
                             preferred_element_type=F32)
    ne = logits.shape[0]
    eid = lax.broadcasted_iota(jnp.int32, logits.shape, 0)
    m1 = jnp.max(logits, axis=0, keepdims=True)
    i1 = jnp.min(jnp.where(logits == m1, eid, ne), axis=0, keepdims=True)
    rest = jnp.where(eid == i1, -jnp.inf, logits)
    m2 = jnp.max(rest, axis=0, keepdims=True)
    i2 = jnp.min(jnp.where(rest == m2, eid, ne), axis=0, keepdims=True)
    e2 = jnp.exp(m2 - m1)
    den = 1.0 + e2
    idx_ref[0:1, :] = i1
    idx_ref[1:2, :] = i2
    gate_ref[0:1, :] = 1.0 / den
    gate_ref[1:2, :] = e2 / den


def _diff_attention_and_router(x, positions, g_mix, w_in, q_norm, k_norm, lam_params, subln, w_out,
                               g_ffn, router, lambda_init, batch, seq):
    t, d = x.shape
    tm = TOKEN_TILE
    nt = t // tm
    inv_freq = ROPE_THETA ** (-jnp.arange(0, ROT_DIM, 2, dtype=F32) / ROT_DIM)
    q, k, v = pl.pallas_call(
        _qkv_kernel,
        grid=(nt,),
        in_specs=[
            pl.BlockSpec((tm, d), lambda i: (i, 0)),
            pl.BlockSpec((1, d), lambda i: (0, 0)),
            pl.BlockSpec((3 * d, d), lambda i: (0, 0)),
            pl.BlockSpec((1, tm), lambda i: (0, i)),
            pl.BlockSpec((ROT_DIM // 2, 1), lambda i: (0, 0)),
            pl.BlockSpec((HEAD_DIM, 1), lambda i: (0, 0)),
            pl.BlockSpec((HEAD_DIM, 1), lambda i: (0, 0)),
        ],
        out_specs=[
            pl.BlockSpec((1, d, tm), lambda i: (i, 0, 0)),
            pl.BlockSpec((tm, d), lambda i: (i, 0)),
            pl.BlockSpec((1, d, tm), lambda i: (i, 0, 0)),
        ],
        out_shape=[
            jax.ShapeDtypeStruct((nt, d, tm), BF16),
            jax.ShapeDtypeStruct((t, d), BF16),
            jax.ShapeDtypeStruct((nt, d, tm), BF16),
        ],
        compiler_params=_cparams("parallel"),
        name="attn_qkv",
    )(x, g_mix, w_in.T, positions.reshape(1, t), inv_freq.reshape(-1, 1),
      q_norm.reshape(-1, 1), k_norm.reshape(-1, 1))

    nq = seq // tm
    o = pl.pallas_call(
        functools.partial(_attn_kernel, lambda_init),
        grid=(batch, N_HEADS, nq),
        in_specs=[
            pl.BlockSpec((1, V_DIM, tm), lambda b, h, i: (b * nq + i, h, 0)),
            pl.BlockSpec((seq, V_DIM), lambda b, h, i: (b, h)),
            pl.BlockSpec((nq, V_DIM, tm), lambda b, h, i: (b, h, 0)),
            pl.BlockSpec((4, HEAD_DIM), lambda b, h, i: (0, 0)),
            pl.BlockSpec((V_DIM, 1), lambda b, h, i: (0, 0)),
        ],
        out_specs=pl.BlockSpec((1, V_DIM, tm), lambda b, h, i: (b * nq + i, h, 0)),
        out_shape=jax.ShapeDtypeStruct((nt, d, tm), BF16),
        scratch_shapes=[pltpu.VMEM((V_DIM, tm), F32), pltpu.VMEM((V_DIM, tm), F32)],
        compiler_params=_cparams("parallel", "parallel", "arbitrary"),
        name="diff_attn",
    )(q, k, v, lam_params, subln.reshape(-1, 1))

    return pl.pallas_call(
        _attn_out_router_kernel,
        grid=(nt,),
        in_specs=[
            pl.BlockSpec((1, d, tm), lambda i: (i, 0, 0)),
            pl.BlockSpec((tm, d), lambda i: (i, 0)),
            pl.BlockSpec((d, d), lambda i: (0, 0)),
            pl.BlockSpec((1, d), lambda i: (0, 0)),
            pl.BlockSpec((N_EXPERTS, d), lambda i: (0, 0)),
        ],
        out_specs=[
            pl.BlockSpec((tm, d), lambda i: (i, 0)),
            pl.BlockSpec((tm, d), lambda i: (i, 0)),
            pl.BlockSpec((2, tm), lambda i: (0, i)),
            pl.BlockSpec((2, tm), lambda i: (0, i)),
        ],
        out_shape=[
            jax.ShapeDtypeStruct((t, d), F32),
            jax.ShapeDtypeStruct((t, d), F32),
            jax.ShapeDtypeStruct((2, t), jnp.int32),
            jax.ShapeDtypeStruct((2, t), F32),
        ],
        compiler_params=_cparams("parallel"),
        name="attn_out_router",
    )(o, x, w_out, g_ffn, router.T)


def _routing_plan(idx, n_tiles):
    t = idx.shape[1]
    hot = (jax.nn.one_hot(idx[0], N_EXPERTS, dtype=jnp.int32)
           + jax.nn.one_hot(idx[1], N_EXPERTS, dtype=jnp.int32))
    incl = jnp.cumsum(hot, axis=0)
    rank = incl - hot
    tiles = (incl[-1] + MOE_TILE - 1) // MOE_TILE
    tile_end = jnp.cumsum(tiles)
    row_start = (tile_end - tiles) * MOE_TILE
    slot = row_start[None, :] + rank
    pos = jnp.stack([jnp.take_along_axis(slot, idx[c][:, None], axis=1)[:, 0] for c in (0, 1)])
    tile_id = jnp.arange(n_tiles, dtype=jnp.int32)
    tile_expert = jnp.minimum(jnp.sum(tile_id[:, None] >= tile_end[None, :], axis=1), N_EXPERTS - 1)
    tile_active = (tile_id < tile_end[-1]).astype(jnp.int32)
    last_tile = jnp.where(tiles > 0, tile_end - 1, -1)
    tail = tile_end[-1] + jnp.arange(N_EXPERTS, dtype=jnp.int32)
    pad_tiles = jnp.concatenate([last_tile, jnp.where(tail < n_tiles, tail, -1)])
    return (pos.astype(jnp.int32), tile_expert.astype(jnp.int32), tile_active,
            pad_tiles.astype(jnp.int32))


def _dispatch_kernel(pad_ref, pos_ref, h_hbm, xs_hbm, zero_ref, sem):
    i = pl.program_id(0)
    n = pos_ref.shape[2] // 2
    base = i * n
    mt = zero_ref.shape[0]

    @pl.when(i == 0)
    def _():
        zero_ref[...] = jnp.zeros_like(zero_ref)
        fills = [pltpu.make_async_copy(
            zero_ref, xs_hbm.at[pl.ds(pl.multiple_of(jnp.maximum(pad_ref[j], 0) * mt, mt), mt)], sem)
            for j in range(pad_ref.shape[0])]
        for j, fill in enumerate(fills):
            pl.when(pad_ref[j] >= 0)(fill.start)
        for j, fill in enumerate(fills):
            pl.when(pad_ref[j] >= 0)(fill.wait)

    def issue(r, carry):
        for c in (0, 1):
            pltpu.make_async_copy(h_hbm.at[pl.ds(base + r, 1)],
                                  xs_hbm.at[pl.ds(pos_ref[0, 0, c * n + r], 1)], sem).start()
        return carry

    lax.fori_loop(0, n, issue, 0)
    for _ in (0, 1):
        pltpu.make_async_copy(h_hbm.at[pl.ds(0, n)], xs_hbm.at[pl.ds(0, n)], sem).wait()


def _expert_kernel(te_ref, ta_ref, xs_ref, w1_ref, w3_ref, w2_ref, y_ref):
    i = pl.program_id(0)

    @pl.when(ta_ref[i] == 1)
    def _():
        x = xs_ref[...].astype(BF16)
        acc = None
        for f0, fl in _ff_chunks(w1_ref.shape[1]):
            a = _dot(x, w1_ref[:, f0:f0 + fl])
            b = _dot(x, w3_ref[:, f0:f0 + fl])
            act = (_silu(a) * b).astype(BF16)
            part = _dot(act, w2_ref[f0:f0 + fl, :])
            acc = part if acc is None else acc + part
        y_ref[...] = acc

    @pl.when(ta_ref[i] == 0)
    def _():
        y_ref[...] = jnp.zeros_like(y_ref)


def _combine_kernel(pos_ref, x_ref, g_ref, y_hbm, o_ref, buf, sem):
    n = x_ref.shape[0]

    def issue(r, carry):
        for c in (0, 1):
            pltpu.make_async_copy(y_hbm.at[pl.ds(pos_ref[0, 0, c * n + r], 1)],
                                  buf.at[c, pl.ds(r, 1)], sem).start()
        return carry

    lax.fori_loop(0, n, issue, 0)
    for c in (0, 1):
        pltpu.make_async_copy(y_hbm.at[pl.ds(0, n)], buf.at[c], sem).wait()
    g = g_ref[...]
    o_ref[...] = x_ref[...] + g[:, 0:1] * buf[0] + g[:, 1:2] * buf[1]


def _moe(x2, h2, idx, gates, w1, w3, w2):
    t, d = x2.shape
    fe = w1.shape[2]
    n_tiles = (2 * t) // MOE_TILE + N_EXPERTS
    rows = n_tiles * MOE_TILE
    pos, tile_expert, tile_active, pad_tiles = _routing_plan(idx, n_tiles)
    gt = GATHER_TILE
    ng = t // gt
    pos_blocks = pos.reshape(2, ng, gt).transpose(1, 0, 2).reshape(ng, 1, 2 * gt)

    xs = pl.pallas_call(
        _dispatch_kernel,
        grid_spec=pltpu.PrefetchScalarGridSpec(
            num_scalar_prefetch=1,
            grid=(ng,),
            in_specs=[
                pl.BlockSpec((1, 1, 2 * gt), lambda i, pad: (i, 0, 0), memory_space=pltpu.SMEM),
                pl.BlockSpec(memory_space=pl.ANY),
            ],
            out_specs=pl.BlockSpec(memory_space=pl.ANY),
            scratch_shapes=[pltpu.VMEM((MOE_TILE, d), F32), pltpu.SemaphoreType.DMA],
        ),
        out_shape=jax.ShapeDtypeStruct((rows, d), F32),
        compiler_params=_cparams("arbitrary"),
        name="moe_dispatch",
    )(pad_tiles, pos_blocks, h2)

    y = pl.pallas_call(
        _expert_kernel,
        grid_spec=pltpu.PrefetchScalarGridSpec(
            num_scalar_prefetch=2,
            grid=(n_tiles,),
            in_specs=[
                pl.BlockSpec((MOE_TILE, d), lambda i, te, ta: (i, 0)),
                pl.BlockSpec((None, d, fe), lambda i, te, ta: (te[i], 0, 0),
                             pipeline_mode=pl.Buffered(1)),
                pl.BlockSpec((None, d, fe), lambda i, te, ta: (te[i], 0, 0),
                             pipeline_mode=pl.Buffered(1)),
                pl.BlockSpec((None, fe, d), lambda i, te, ta: (te[i], 0, 0),
                             pipeline_mode=pl.Buffered(1)),
            ],
            out_specs=pl.BlockSpec((MOE_TILE, d), lambda i, te, ta: (i, 0)),
        ),
        out_shape=jax.ShapeDtypeStruct((rows, d), F32),
        compiler_params=_cparams("arbitrary"),
        name="moe_experts",
    )(tile_expert, tile_active, xs, w1, w3, w2)

    return pl.pallas_call(
        _combine_kernel,
        grid=(ng,),
        in_specs=[
            pl.BlockSpec((1, 1, 2 * gt), lambda i: (i, 0, 0), memory_space=pltpu.SMEM),
            pl.BlockSpec((gt, d), lambda i: (i, 0)),
            pl.BlockSpec((gt, 2), lambda i: (i, 0)),
            pl.BlockSpec(memory_space=pl.ANY),
        ],
        out_specs=pl.BlockSpec((gt, d), lambda i: (i, 0)),
        out_shape=jax.ShapeDtypeStruct((t, d), F32),
        scratch_shapes=[pltpu.VMEM((2, gt, d), F32), pltpu.SemaphoreType.DMA],
        compiler_params=_cparams("arbitrary"),
        name="moe_combine",
    )(pos_blocks, x2, gates.T, y)


def kernel(x, positions, norm_mix, norm_ffn, conv_in, conv_w, conv_out, attn_in, q_norm, k_norm,
           lam_q1, lam_k1, lam_q2, lam_k2, subln, attn_out, ffn_w1, ffn_w3, ffn_w2,
           router, moe_w1, moe_w3, moe_w2):
    batch, seq, d = x.shape
    xt = x.reshape(batch * seq, d)
    bf = lambda w: w.astype(BF16)

    xt = _conv_mixer(xt, norm_mix[0:1], bf(conv_in[0]), conv_w[0], bf(conv_out[0]), seq)
    xt = _dense_ffn(xt, norm_ffn[0:1], bf(ffn_w1[0]), bf(ffn_w3[0]), bf(ffn_w2[0]))

    lambda_init = 0.8 - 0.6 * math.exp(-0.3 * 1)
    lam_params = jnp.concatenate([lam_q1, lam_k1, lam_q2, lam_k2], axis=0)
    x2, h2, idx, gates = _diff_attention_and_router(
        xt, positions, norm_mix[1:2], bf(attn_in[0]), q_norm[0], k_norm[0], lam_params, subln[0],
        bf(attn_out[0]), norm_ffn[1:2], router[0], lambda_init, batch, seq)
    out = _moe(x2, h2, idx, gates, bf(moe_w1[0]), bf(moe_w3[0]), bf(moe_w2[0]))
    return out.reshape(batch, seq, d)
```

```python
import functools
import math

import jax
import jax.numpy as jnp
from jax import lax
from jax.experimental import pallas as pl
from jax.experimental.pallas import tpu as pltpu

F32 = jnp.float32
BF16 = jnp.bfloat16

D_MODEL = 1024
N_HEADS = 8
HEAD_DIM = 64
V_DIM = 2 * HEAD_DIM
ROT_DIM = HEAD_DIM // 4
ROPE_THETA = 500000.0
N_EXPERTS = 8
RMS_EPS = 1e-6
LOG2E = 1.4426950408889634

TOKEN_TILE = 512
ATTN_KV_TILE = 512
MOE_TILE = 512
FF_CHUNK = 512
GATHER_TILE = 256
VMEM_LIMIT = 56 * 1024 * 1024

_NEG_BIG = -1e30
_EXP2_SAFE_LOG2 = 60.0
_BF16_ROUNDING_MARGIN = 1.02


def _cparams(*sem):
    return pltpu.CompilerParams(dimension_semantics=sem, vmem_limit_bytes=VMEM_LIMIT)


def _rms_rows(x, g):
    ms = jnp.mean(x * x, axis=-1, keepdims=True)
    return x * lax.rsqrt(ms + RMS_EPS) * g


def _dot(a, b):
    return jnp.dot(a, b, preferred_element_type=F32)


def _silu(a):
    return a * (1.0 / (1.0 + jnp.exp(-a)))


def _ff_chunks(width):
    out, f0 = [], 0
    while f0 < width:
        fl = min(FF_CHUNK, width - f0)
        out.append((f0, fl))
        f0 += fl
    return out


def _conv_in_kernel(x_ref, g_ref, w_ref, b_ref, u_ref):
    d = x_ref.shape[1]
    h = _rms_rows(x_ref[...], g_ref[...]).astype(BF16)
    b_ref[...] = _dot(h, w_ref[:, 0:d]).astype(BF16)
    c = _dot(h, w_ref[:, d:2 * d])
    v = _dot(h, w_ref[:, 2 * d:3 * d])
    u_ref[...] = (c * v).astype(BF16)


def _conv_out_kernel(tiles_per_seq, u_ref, up_ref, un_ref, b_ref, x_ref, cw_ref, w_ref, o_ref):
    i = pl.program_id(0)
    tm = u_ref.shape[0]
    halo = up_ref.shape[0]
    u = u_ref[...].astype(F32)
    first = (i % tiles_per_seq) == 0
    last = (i % tiles_per_seq) == tiles_per_seq - 1
    prev_row = jnp.where(first, 0.0, up_ref[halo - 1:halo, :].astype(F32))
    next_row = jnp.where(last, 0.0, un_ref[0:1, :].astype(F32))
    rows = lax.broadcasted_iota(jnp.int32, (tm, 1), 0)
    u_m1 = jnp.where(rows == 0, prev_row, pltpu.roll(u, 1, 0))
    u_p1 = jnp.where(rows == tm - 1, next_row, pltpu.roll(u, tm - 1, 0))
    cw = cw_ref[...]
    conv = u_m1 * cw[0:1, :] + u * cw[1:2, :] + u_p1 * cw[2:3, :]
    y = (b_ref[...].astype(F32) * conv).astype(BF16)
    o_ref[...] = x_ref[...] + _dot(y, w_ref[...])


def _conv_mixer(x, g, w_in, conv_w, w_out, seq):
    t, d = x.shape
    tm = TOKEN_TILE
    nt = t // tm
    halo = 16
    b, u = pl.pallas_call(
        _conv_in_kernel,
        grid=(nt,),
        in_specs=[
            pl.BlockSpec((tm, d), lambda i: (i, 0)),
            pl.BlockSpec((1, d), lambda i: (0, 0)),
            pl.BlockSpec((d, 3 * d), lambda i: (0, 0)),
        ],
        out_specs=[pl.BlockSpec((tm, d), lambda i: (i, 0))] * 2,
        out_shape=[jax.ShapeDtypeStruct((t, d), BF16)] * 2,
        compiler_params=_cparams("parallel"),
        name="conv_in",
    )(x, g, w_in)
    hb = tm // halo
    nhb = t // halo
    return pl.pallas_call(
        functools.partial(_conv_out_kernel, seq // tm),
        grid=(nt,),
        in_specs=[
            pl.BlockSpec((tm, d), lambda i: (i, 0)),
            pl.BlockSpec((halo, d), lambda i: (jnp.maximum(i * hb - 1, 0), 0)),
            pl.BlockSpec((halo, d), lambda i: (jnp.minimum((i + 1) * hb, nhb - 1), 0)),
            pl.BlockSpec((tm, d), lambda i: (i, 0)),
            pl.BlockSpec((tm, d), lambda i: (i, 0)),
            pl.BlockSpec((3, d), lambda i: (0, 0)),
            pl.BlockSpec((d, d), lambda i: (0, 0)),
        ],
        out_specs=pl.BlockSpec((tm, d), lambda i: (i, 0)),
        out_shape=jax.ShapeDtypeStruct((t, d), F32),
        compiler_params=_cparams("parallel"),
        name="conv_out",
    )(u, u, u, b, x, conv_w, w_out)


def _ffn_kernel(x_ref, g_ref, w1_ref, w3_ref, w2_ref, o_ref):
    x = x_ref[...]
    h = _rms_rows(x, g_ref[...]).astype(BF16)
    acc = x
    for f0, fl in _ff_chunks(w1_ref.shape[1]):
        a = _dot(h, w1_ref[:, f0:f0 + fl])
        b = _dot(h, w3_ref[:, f0:f0 + fl])
        act = (_silu(a) * b).astype(BF16)
        acc = acc + _dot(act, w2_ref[f0:f0 + fl, :])
    o_ref[...] = acc


def _dense_ffn(x, g, w1, w3, w2):
    t, d = x.shape
    f = w1.shape[1]
    tm = TOKEN_TILE
    return pl.pallas_call(
        _ffn_kernel,
        grid=(t // tm,),
        in_specs=[
            pl.BlockSpec((tm, d), lambda i: (i, 0)),
            pl.BlockSpec((1, d), lambda i: (0, 0)),
            pl.BlockSpec((d, f), lambda i: (0, 0)),
            pl.BlockSpec((d, f), lambda i: (0, 0)),
            pl.BlockSpec((f, d), lambda i: (0, 0)),
        ],
        out_specs=pl.BlockSpec((tm, d), lambda i: (i, 0)),
        out_shape=jax.ShapeDtypeStruct((t, d), F32),
        compiler_params=_cparams("parallel"),
        name="dense_ffn",
    )(x, g, w1, w3, w2)


def _qkv_kernel(x_ref, g_ref, wt_ref, pos_ref, freq_ref, qn_ref, kn_ref, q_ref, k_ref, v_ref):
    tm, d = x_ref.shape
    h = _rms_rows(x_ref[...], g_ref[...]).astype(BF16)
    nt = (((1,), (1,)), ((), ()))
    ang = pos_ref[...].astype(F32) * freq_ref[...]
    cos = jnp.cos(ang)[None]
    sin = jnp.sin(ang)[None]
    half = ROT_DIM // 2
    groups = d // HEAD_DIM

    def norm_rope(w_rows, gain, scale):
        t = lax.dot_general(w_rows, h, nt, preferred_element_type=F32)
        t = t.reshape(groups, HEAD_DIM, tm)
        ms = jnp.mean(t * t, axis=1, keepdims=True)
        t = t * lax.rsqrt(ms + RMS_EPS) * gain[None]
        t1 = t[:, 0:half, :]
        t2 = t[:, half:ROT_DIM, :]
        t = jnp.concatenate([t1 * cos - t2 * sin, t2 * cos + t1 * sin, t[:, ROT_DIM:, :]], axis=1)
        return (t * scale).reshape(d, tm)

    q = norm_rope(wt_ref[0:d, :], qn_ref[...], (HEAD_DIM ** -0.5) * LOG2E)
    q_ref[0] = q.astype(BF16)
    k = norm_rope(wt_ref[d:2 * d, :], kn_ref[...], 1.0)
    k_ref[...] = k.T.astype(BF16)
    v = lax.dot_general(wt_ref[2 * d:3 * d, :], h, nt, preferred_element_type=F32)
    v_ref[0] = v.astype(BF16)


def _attn_kernel(lambda_init, unshifted_ref, q_ref, k_ref, v_ref, lam_ref, sub_ref, o_ref,
                 acc1_ref, acc2_ref, p_ref):
    tq = q_ref.shape[2]
    nkb = v_ref.shape[0]
    tk = v_ref.shape[2]
    q = q_ref[0]
    rows = lax.broadcasted_iota(jnp.int32, (V_DIM, 1), 0)
    zero = jnp.zeros_like(q)
    q1 = jnp.where(rows < HEAD_DIM, q, zero)
    q2 = jnp.where(rows >= HEAD_DIM, q, zero)
    acc1_ref[...] = jnp.zeros_like(acc1_ref)
    acc2_ref[...] = jnp.zeros_like(acc2_ref)
    zer = jnp.zeros((1, tq), F32)

    def load_kv(kb):
        k = k_ref[pl.ds(pl.multiple_of(kb * tk, tk), tk), :]
        return k, v_ref[kb]

    def finish(l1, l2):
        lp = lam_ref[...]
        lam = (jnp.exp(jnp.sum(lp[0:1, :] * lp[1:2, :], axis=-1, keepdims=True))
               - jnp.exp(jnp.sum(lp[2:3, :] * lp[3:4, :], axis=-1, keepdims=True))
               + lambda_init)
        o = acc1_ref[...] / l1 - lam * (acc2_ref[...] / l2)
        ms = jnp.mean(o * o, axis=0, keepdims=True)
        o = o * lax.rsqrt(ms + RMS_EPS) * sub_ref[...] * (1.0 - lambda_init)
        o_ref[0] = o.astype(BF16)

    @pl.when(unshifted_ref[0] == 1)
    def _():
        def scores(kb, l1, l2, slot):
            k = load_kv(kb)[0]
            p1 = jnp.exp2(_dot(k, q1))
            p2 = jnp.exp2(_dot(k, q2))
            p_ref[slot, 0] = p1.astype(BF16)
            p_ref[slot, 1] = p2.astype(BF16)
            return l1 + jnp.sum(p1, axis=0, keepdims=True), l2 + jnp.sum(p2, axis=0, keepdims=True)

        def values(kb, slot):
            v = v_ref[kb]
            acc1_ref[...] += _dot(v, p_ref[slot, 0])
            acc2_ref[...] += _dot(v, p_ref[slot, 1])

        def body(j, carry):
            kb = 2 * j + 1
            l1, l2 = scores(kb, carry[0], carry[1], 1)
            values(kb - 1, 0)
            l1, l2 = scores(kb + 1, l1, l2, 0)
            values(kb, 1)
            return l1, l2

        l1, l2 = scores(0, zer, zer, 0)
        l1, l2 = lax.fori_loop(0, (nkb - 2) // 2, body, (l1, l2))
        l1, l2 = scores(nkb - 1, l1, l2, 1)
        values(nkb - 2, 0)
        values(nkb - 1, 1)
        finish(l1, l2)

    @pl.when(unshifted_ref[0] == 0)
    def _():
        def one_map(k, v, qc, m, l, acc_ref):
            s = _dot(k, qc)
            m_new = jnp.maximum(m, jnp.max(s, axis=0, keepdims=True))
            alpha = jnp.exp2(m - m_new)
            p = jnp.exp2(s - m_new)
            acc_ref[...] = alpha * acc_ref[...] + _dot(v, p.astype(BF16))
            return m_new, alpha * l + jnp.sum(p, axis=0, keepdims=True)

        def body(kb, carry):
            k, v = load_kv(kb)
            m1, l1 = one_map(k, v, q1, carry[0], carry[1], acc1_ref)
            m2, l2 = one_map(k, v, q2, carry[2], carry[3], acc2_ref)
            return m1, l1, m2, l2

        neg = jnp.full((1, tq), _NEG_BIG, F32)
        _, l1, _, l2 = lax.fori_loop(0, nkb, body, (neg, zer, neg, zer))
        finish(l1, l2)


def _attn_out_router_kernel(o_ref, x_ref, w_ref, g_ref, rt_ref, x2_ref, h2_ref, idx_ref, gate_ref):
    tn = (((0,), (0,)), ((), ()))
    y = lax.dot_general(o_ref[0], w_ref[...], tn, preferred_element_type=F32)
    x2 = x_ref[...] + y
    x2_ref[...] = x2
    h2 = _rms_rows(x2, g_ref[...])
    h2_ref[...] = h2
    nt = (((1,), (1,)), ((), ()))
    logits = lax.dot_general(rt_ref[...], h2, nt, precision=lax.Precision.HIGHEST,
                             preferred_element_type=F32)
    ne = logits.shape[0]
    eid = lax.broadcasted_iota(jnp.int32, logits.shape, 0)
    m1 = jnp.max(logits, axis=0, keepdims=True)
    i1 = jnp.min(jnp.where(logits == m1, eid, ne), axis=0, keepdims=True)
    rest = jnp.where(eid == i1, -jnp.inf, logits)
    m2 = jnp.max(rest, axis=0, keepdims=True)
    i2 = jnp.min(jnp.where(rest == m2, eid, ne), axis=0, keepdims=True)
    e2 = jnp.exp(m2 - m1)
    den = 1.0 + e2
    idx_ref[0:1, :] = i1
    idx_ref[1:2, :] = i2
    gate_ref[0:1, :] = 1.0 / den
    gate_ref[1:2, :] = e2 / den


def _diff_attention_and_router(x, positions, g_mix, w_in, q_norm, k_norm, lam_params, subln, w_out,
                               g_ffn, router, lambda_init, batch, seq):
    t, d = x.shape
    tm = TOKEN_TILE
    nt = t // tm
    inv_freq = ROPE_THETA ** (-jnp.arange(0, ROT_DIM, 2, dtype=F32) / ROT_DIM)
    q, k, v = pl.pallas_call(
        _qkv_kernel,
        grid=(nt,),
        in_specs=[
            pl.BlockSpec((tm, d), lambda i: (i, 0)),
            pl.BlockSpec((1, d), lambda i: (0, 0)),
            pl.BlockSpec((3 * d, d), lambda i: (0, 0)),
            pl.BlockSpec((1, tm), lambda i: (0, i)),
            pl.BlockSpec((ROT_DIM // 2, 1), lambda i: (0, 0)),
            pl.BlockSpec((HEAD_DIM, 1), lambda i: (0, 0)),
            pl.BlockSpec((HEAD_DIM, 1), lambda i: (0, 0)),
        ],
        out_specs=[
            pl.BlockSpec((1, d, tm), lambda i: (i, 0, 0)),
            pl.BlockSpec((tm, d), lambda i: (i, 0)),
            pl.BlockSpec((1, d, tm), lambda i: (i, 0, 0)),
        ],
        out_shape=[
            jax.ShapeDtypeStruct((nt, d, tm), BF16),
            jax.ShapeDtypeStruct((t, d), BF16),
            jax.ShapeDtypeStruct((nt, d, tm), BF16),
        ],
        compiler_params=_cparams("parallel"),
        name="attn_qkv",
    )(x, g_mix, w_in.T, positions.reshape(1, t), inv_freq.reshape(-1, 1),
      q_norm.reshape(-1, 1), k_norm.reshape(-1, 1))

    nq = seq // tm
    score_bound = (HEAD_DIM ** 0.5 * LOG2E * _BF16_ROUNDING_MARGIN
                   * jnp.max(jnp.abs(q_norm)) * jnp.max(jnp.abs(k_norm)))
    unshifted = (score_bound <= _EXP2_SAFE_LOG2).astype(jnp.int32).reshape(1)
    o = pl.pallas_call(
        functools.partial(_attn_kernel, lambda_init),
        grid_spec=pltpu.PrefetchScalarGridSpec(
            num_scalar_prefetch=1,
            grid=(batch, N_HEADS, nq),
            in_specs=[
                pl.BlockSpec((1, V_DIM, tm), lambda b, h, i, f: (b * nq + i, h, 0)),
                pl.BlockSpec((seq, V_DIM), lambda b, h, i, f: (b, h)),
                pl.BlockSpec((nq, V_DIM, tm), lambda b, h, i, f: (b, h, 0)),
                pl.BlockSpec((4, HEAD_DIM), lambda b, h, i, f: (0, 0)),
                pl.BlockSpec((V_DIM, 1), lambda b, h, i, f: (0, 0)),
            ],
            out_specs=pl.BlockSpec((1, V_DIM, tm), lambda b, h, i, f: (b * nq + i, h, 0)),
            scratch_shapes=[pltpu.VMEM((V_DIM, tm), F32), pltpu.VMEM((V_DIM, tm), F32),
                            pltpu.VMEM((2, 2, tm, tm), BF16)],
        ),
        out_shape=jax.ShapeDtypeStruct((nt, d, tm), BF16),
        compiler_params=_cparams("parallel", "parallel", "arbitrary"),
        name="diff_attn",
    )(unshifted, q, k, v, lam_params, subln.reshape(-1, 1))

    return pl.pallas_call(
        _attn_out_router_kernel,
        grid=(nt,),
        in_specs=[
            pl.BlockSpec((1, d, tm), lambda i: (i, 0, 0)),
            pl.BlockSpec((tm, d), lambda i: (i, 0)),
            pl.BlockSpec((d, d), lambda i: (0, 0)),
            pl.BlockSpec((1, d), lambda i: (0, 0)),
            pl.BlockSpec((N_EXPERTS, d), lambda i: (0, 0)),
        ],
        out_specs=[
            pl.BlockSpec((tm, d), lambda i: (i, 0)),
            pl.BlockSpec((tm, d), lambda i: (i, 0)),
            pl.BlockSpec((2, tm), lambda i: (0, i)),
            pl.BlockSpec((2, tm), lambda i: (0, i)),
        ],
        out_shape=[
            jax.ShapeDtypeStruct((t, d), F32),
            jax.ShapeDtypeStruct((t, d), F32),
            jax.ShapeDtypeStruct((2, t), jnp.int32),
            jax.ShapeDtypeStruct((2, t), F32),
        ],
        compiler_params=_cparams("parallel"),
        name="attn_out_router",
    )(o, x, w_out, g_ffn, router.T)


def _routing_plan(idx, n_tiles):
    t = idx.shape[1]
    hot = (jax.nn.one_hot(idx[0], N_EXPERTS, dtype=jnp.int32)
           + jax.nn.one_hot(idx[1], N_EXPERTS, dtype=jnp.int32))
    incl = jnp.cumsum(hot, axis=0)
    rank = incl - hot
    tiles = (incl[-1] + MOE_TILE - 1) // MOE_TILE
    tile_end = jnp.cumsum(tiles)
    row_start = (tile_end - tiles) * MOE_TILE
    slot = row_start[None, :] + rank
    pos = jnp.stack([jnp.take_along_axis(slot, idx[c][:, None], axis=1)[:, 0] for c in (0, 1)])
    tile_id = jnp.arange(n_tiles, dtype=jnp.int32)
    tile_expert = jnp.minimum(jnp.sum(tile_id[:, None] >= tile_end[None, :], axis=1), N_EXPERTS - 1)
    tile_active = (tile_id < tile_end[-1]).astype(jnp.int32)
    last_tile = jnp.where(tiles > 0, tile_end - 1, -1)
    tail = tile_end[-1] + jnp.arange(N_EXPERTS, dtype=jnp.int32)
    pad_tiles = jnp.concatenate([last_tile, jnp.where(tail < n_tiles, tail, -1)])
    return (pos.astype(jnp.int32), tile_expert.astype(jnp.int32), tile_active,
            pad_tiles.astype(jnp.int32))


def _dispatch_kernel(pad_ref, pos_ref, h_ref, xs_hbm, zero_ref, sem):
    i = pl.program_id(0)
    n = h_ref.shape[0]
    mt = zero_ref.shape[0]

    @pl.when(i == 0)
    def _():
        zero_ref[...] = jnp.zeros_like(zero_ref)
        fills = [pltpu.make_async_copy(
            zero_ref, xs_hbm.at[pl.ds(pl.multiple_of(jnp.maximum(pad_ref[j], 0) * mt, mt), mt)], sem)
            for j in range(pad_ref.shape[0])]
        for j, fill in enumerate(fills):
            pl.when(pad_ref[j] >= 0)(fill.start)
        for j, fill in enumerate(fills):
            pl.when(pad_ref[j] >= 0)(fill.wait)

    def issue(r, carry):
        for c in (0, 1):
            pltpu.make_async_copy(h_ref.at[pl.ds(r, 1)],
                                  xs_hbm.at[pl.ds(pos_ref[0, 0, c * n + r], 1)], sem).start()
        return carry

    lax.fori_loop(0, n, issue, 0, unroll=8)
    for _ in (0, 1):
        pltpu.make_async_copy(h_ref, xs_hbm.at[pl.ds(0, n)], sem).wait()


def _expert_kernel(te_ref, ta_ref, xs_ref, w1_ref, w3_ref, w2_ref, y_ref):
    i = pl.program_id(0)

    @pl.when(ta_ref[i] == 1)
    def _():
        x = xs_ref[...].astype(BF16)
        acc = None
        for f0, fl in _ff_chunks(w1_ref.shape[1]):
            a = _dot(x, w1_ref[:, f0:f0 + fl])
            b = _dot(x, w3_ref[:, f0:f0 + fl])
            act = (_silu(a) * b).astype(BF16)
            part = _dot(act, w2_ref[f0:f0 + fl, :])
            acc = part if acc is None else acc + part
        y_ref[...] = acc

    @pl.when(ta_ref[i] == 0)
    def _():
        y_ref[...] = jnp.zeros_like(y_ref)


def _combine_kernel(pos_ref, x_ref, g_ref, y_hbm, o_ref, buf, sem):
    n = x_ref.shape[0]

    def issue(r, carry):
        for c in (0, 1):
            pltpu.make_async_copy(y_hbm.at[pl.ds(pos_ref[0, 0, c * n + r], 1)],
                                  buf.at[c, pl.ds(r, 1)], sem).start()
        return carry

    lax.fori_loop(0, n, issue, 0)
    for c in (0, 1):
        pltpu.make_async_copy(y_hbm.at[pl.ds(0, n)], buf.at[c], sem).wait()
    g = g_ref[...]
    o_ref[...] = x_ref[...] + g[:, 0:1] * buf[0] + g[:, 1:2] * buf[1]


def _moe(x2, h2, idx, gates, w1, w3, w2):
    t, d = x2.shape
    fe = w1.shape[2]
    n_tiles = (2 * t) // MOE_TILE + N_EXPERTS
    rows = n_tiles * MOE_TILE
    pos, tile_expert, tile_active, pad_tiles = _routing_plan(idx, n_tiles)
    gt = GATHER_TILE
    ng = t // gt
    pos_blocks = pos.reshape(2, ng, gt).transpose(1, 0, 2).reshape(ng, 1, 2 * gt)

    xs = pl.pallas_call(
        _dispatch_kernel,
        grid_spec=pltpu.PrefetchScalarGridSpec(
            num_scalar_prefetch=1,
            grid=(ng,),
            in_specs=[
                pl.BlockSpec((1, 1, 2 * gt), lambda i, pad: (i, 0, 0), memory_space=pltpu.SMEM),
                pl.BlockSpec((gt, d), lambda i, pad: (i, 0)),
            ],
            out_specs=pl.BlockSpec(memory_space=pl.ANY),
            scratch_shapes=[pltpu.VMEM((MOE_TILE, d), F32), pltpu.SemaphoreType.DMA],
        ),
        out_shape=jax.ShapeDtypeStruct((rows, d), F32),
        compiler_params=_cparams("arbitrary"),
        name="moe_dispatch",
    )(pad_tiles, pos_blocks, h2)

    y = pl.pallas_call(
        _expert_kernel,
        grid_spec=pltpu.PrefetchScalarGridSpec(
            num_scalar_prefetch=2,
            grid=(n_tiles,),
            in_specs=[
                pl.BlockSpec((MOE_TILE, d), lambda i, te, ta: (i, 0)),
                pl.BlockSpec((None, d, fe), lambda i, te, ta: (te[i], 0, 0),
                             pipeline_mode=pl.Buffered(1)),
                pl.BlockSpec((None, d, fe), lambda i, te, ta: (te[i], 0, 0),
                             pipeline_mode=pl.Buffered(1)),
                pl.BlockSpec((None, fe, d), lambda i, te, ta: (te[i], 0, 0),
                             pipeline_mode=pl.Buffered(1)),
            ],
            out_specs=pl.BlockSpec((MOE_TILE, d), lambda i, te, ta: (i, 0)),
        ),
        out_shape=jax.ShapeDtypeStruct((rows, d), F32),
        compiler_params=_cparams("arbitrary"),
        name="moe_experts",
    )(tile_expert, tile_active, xs, w1, w3, w2)

    return pl.pallas_call(
        _combine_kernel,
        grid=(ng,),
        in_specs=[
            pl.BlockSpec((1, 1, 2 * gt), lambda i: (i, 0, 0), memory_space=pltpu.SMEM),
            pl.BlockSpec((gt, d), lambda i: (i, 0)),
            pl.BlockSpec((gt, 2), lambda i: (i, 0)),
            pl.BlockSpec(memory_space=pl.ANY),
        ],
        out_specs=pl.BlockSpec((gt, d), lambda i: (i, 0)),
        out_shape=jax.ShapeDtypeStruct((t, d), F32),
        scratch_shapes=[pltpu.VMEM((2, gt, d), F32), pltpu.SemaphoreType.DMA],
        compiler_params=_cparams("arbitrary"),
        name="moe_combine",
    )(pos_blocks, x2, gates.T, y)


def kernel(x, positions, norm_mix, norm_ffn, conv_in, conv_w, conv_out, attn_in, q_norm, k_norm,
           lam_q1, lam_k1, lam_q2, lam_k2, subln, attn_out, ffn_w1, ffn_w3, ffn_w2,
           router, moe_w1, moe_w3, moe_w2):
    batch, seq, d = x.shape
    xt = x.reshape(batch * seq, d)
    bf = lambda w: w.astype(BF16)

    xt = _conv_mixer(xt, norm_mix[0:1], bf(conv_in[0]), conv_w[0], bf(conv_out[0]), seq)
    xt = _dense_ffn(xt, norm_ffn[0:1], bf(ffn_w1[0]), bf(ffn_w3[0]), bf(ffn_w2[0]))

    lambda_init = 0.8 - 0.6 * math.exp(-0.3 * 1)
    lam_params = jnp.concatenate([lam_q1, lam_k1, lam_q2, lam_k2], axis=0)
    x2, h2, idx, gates = _diff_attention_and_router(
        xt, positions, norm_mix[1:2], bf(attn_in[0]), q_norm[0], k_norm[0], lam_params, subln[0],
        bf(attn_out[0]), norm_ffn[1:2], router[0], lambda_init, batch, seq)
    out = _moe(x2, h2, idx, gates, bf(moe_w1[0]), bf(moe_w3[0]), bf(moe_w2[0]))
    return out.reshape(batch, seq, d)
```

```python
import functools
import math

import jax
import jax.numpy as jnp
from jax import lax
from jax.experimental import pallas as pl
from jax.experimental.pallas import tpu as pltpu

F32 = jnp.float32
BF16 = jnp.bfloat16

D_MODEL = 1024
N_HEADS = 8
HEAD_DIM = 64
V_DIM = 2 * HEAD_DIM
ROT_DIM = HEAD_DIM // 4
ROPE_THETA = 500000.0
N_EXPERTS = 8
RMS_EPS = 1e-6
LOG2E = 1.4426950408889634

TOKEN_TILE = 512
ATTN_KV_TILE = 512
MOE_TILE = 512
FF_CHUNK = 512
GATHER_TILE = 256
VMEM_LIMIT = 56 * 1024 * 1024

_NEG_BIG = -1e30
_EXP2_SAFE_LOG2 = 60.0
_BF16_ROUNDING_MARGIN = 1.02


def _cparams(*sem):
    return pltpu.CompilerParams(dimension_semantics=sem, vmem_limit_bytes=VMEM_LIMIT)


def _rms_rows(x, g):
    ms = jnp.mean(x * x, axis=-1, keepdims=True)
    return x * lax.rsqrt(ms + RMS_EPS) * g


def _dot(a, b):
    return jnp.dot(a, b, preferred_element_type=F32)


def _silu(a):
    return a * (1.0 / (1.0 + jnp.exp(-a)))


def _ff_chunks(width):
    out, f0 = [], 0
    while f0 < width:
        fl = min(FF_CHUNK, width - f0)
        out.append((f0, fl))
        f0 += fl
    return out


def _conv_in_kernel(x_ref, g_ref, w_ref, b_ref, u_ref):
    d = x_ref.shape[1]
    h = _rms_rows(x_ref[...], g_ref[...]).astype(BF16)
    b_ref[...] = _dot(h, w_ref[:, 0:d]).astype(BF16)
    c = _dot(h, w_ref[:, d:2 * d])
    v = _dot(h, w_ref[:, 2 * d:3 * d])
    u_ref[...] = (c * v).astype(BF16)


def _conv_out_kernel(tiles_per_seq, u_ref, up_ref, un_ref, b_ref, x_ref, cw_ref, w_ref, o_ref):
    i = pl.program_id(0)
    tm = u_ref.shape[0]
    halo = up_ref.shape[0]
    u = u_ref[...].astype(F32)
    first = (i % tiles_per_seq) == 0
    last = (i % tiles_per_seq) == tiles_per_seq - 1
    prev_row = jnp.where(first, 0.0, up_ref[halo - 1:halo, :].astype(F32))
    next_row = jnp.where(last, 0.0, un_ref[0:1, :].astype(F32))
    rows = lax.broadcasted_iota(jnp.int32, (tm, 1), 0)
    u_m1 = jnp.where(rows == 0, prev_row, pltpu.roll(u, 1, 0))
    u_p1 = jnp.where(rows == tm - 1, next_row, pltpu.roll(u, tm - 1, 0))
    cw = cw_ref[...]
    conv = u_m1 * cw[0:1, :] + u * cw[1:2, :] + u_p1 * cw[2:3, :]
    y = (b_ref[...].astype(F32) * conv).astype(BF16)
    o_ref[...] = x_ref[...] + _dot(y, w_ref[...])


def _conv_mixer(x, g, w_in, conv_w, w_out, seq):
    t, d = x.shape
    tm = TOKEN_TILE
    nt = t // tm
    halo = 16
    b, u = pl.pallas_call(
        _conv_in_kernel,
        grid=(nt,),
        in_specs=[
            pl.BlockSpec((tm, d), lambda i: (i, 0)),
            pl.BlockSpec((1, d), lambda i: (0, 0)),
            pl.BlockSpec((d, 3 * d), lambda i: (0, 0)),
        ],
        out_specs=[pl.BlockSpec((tm, d), lambda i: (i, 0))] * 2,
        out_shape=[jax.ShapeDtypeStruct((t, d), BF16)] * 2,
        compiler_params=_cparams("parallel"),
        name="conv_in",
    )(x, g, w_in)
    hb = tm // halo
    nhb = t // halo
    return pl.pallas_call(
        functools.partial(_conv_out_kernel, seq // tm),
        grid=(nt,),
        in_specs=[
            pl.BlockSpec((tm, d), lambda i: (i, 0)),
            pl.BlockSpec((halo, d), lambda i: (jnp.maximum(i * hb - 1, 0), 0)),
            pl.BlockSpec((halo, d), lambda i: (jnp.minimum((i + 1) * hb, nhb - 1), 0)),
            pl.BlockSpec((tm, d), lambda i: (i, 0)),
            pl.BlockSpec((tm, d), lambda i: (i, 0)),
            pl.BlockSpec((3, d), lambda i: (0, 0)),
            pl.BlockSpec((d, d), lambda i: (0, 0)),
        ],
        out_specs=pl.BlockSpec((tm, d), lambda i: (i, 0)),
        out_shape=jax.ShapeDtypeStruct((t, d), F32),
        compiler_params=_cparams("parallel"),
        name="conv_out",
    )(u, u, u, b, x, conv_w, w_out)


def _ffn_kernel(x_ref, g_ref, w1_ref, w3_ref, w2_ref, o_ref):
    x = x_ref[...]
    h = _rms_rows(x, g_ref[...]).astype(BF16)
    acc = x
    for f0, fl in _ff_chunks(w1_ref.shape[1]):
        a = _dot(h, w1_ref[:, f0:f0 + fl])
        b = _dot(h, w3_ref[:, f0:f0 + fl])
        act = (_silu(a) * b).astype(BF16)
        acc = acc + _dot(act, w2_ref[f0:f0 + fl, :])
    o_ref[...] = acc


def _dense_ffn(x, g, w1, w3, w2):
    t, d = x.shape
    f = w1.shape[1]
    tm = TOKEN_TILE
    return pl.pallas_call(
        _ffn_kernel,
        grid=(t // tm,),
        in_specs=[
            pl.BlockSpec((tm, d), lambda i: (i, 0)),
            pl.BlockSpec((1, d), lambda i: (0, 0)),
            pl.BlockSpec((d, f), lambda i: (0, 0)),
            pl.BlockSpec((d, f), lambda i: (0, 0)),
            pl.BlockSpec((f, d), lambda i: (0, 0)),
        ],
        out_specs=pl.BlockSpec((tm, d), lambda i: (i, 0)),
        out_shape=jax.ShapeDtypeStruct((t, d), F32),
        compiler_params=_cparams("parallel"),
        name="dense_ffn",
    )(x, g, w1, w3, w2)


def _qkv_kernel(x_ref, g_ref, wt_ref, pos_ref, freq_ref, qn_ref, kn_ref, q_ref, k_ref, v_ref):
    tm, d = x_ref.shape
    h = _rms_rows(x_ref[...], g_ref[...]).astype(BF16)
    nt = (((1,), (1,)), ((), ()))
    ang = pos_ref[...].astype(F32) * freq_ref[...]
    cos = jnp.cos(ang)[None]
    sin = jnp.sin(ang)[None]
    half = ROT_DIM // 2
    groups = d // HEAD_DIM

    def norm_rope(w_rows, gain, scale):
        t = lax.dot_general(w_rows, h, nt, preferred_element_type=F32)
        t = t.reshape(groups, HEAD_DIM, tm)
        ms = jnp.mean(t * t, axis=1, keepdims=True)
        t = t * lax.rsqrt(ms + RMS_EPS) * gain[None]
        t1 = t[:, 0:half, :]
        t2 = t[:, half:ROT_DIM, :]
        t = jnp.concatenate([t1 * cos - t2 * sin, t2 * cos + t1 * sin, t[:, ROT_DIM:, :]], axis=1)
        return (t * scale).reshape(d, tm)

    q = norm_rope(wt_ref[0:d, :], qn_ref[...], (HEAD_DIM ** -0.5) * LOG2E)
    q_ref[0] = q.astype(BF16)
    k = norm_rope(wt_ref[d:2 * d, :], kn_ref[...], 1.0)
    k_ref[...] = k.T.astype(BF16)
    v = lax.dot_general(wt_ref[2 * d:3 * d, :], h, nt, preferred_element_type=F32)
    v_ref[0] = v.astype(BF16)


def _attn_kernel(lambda_init, unshifted_ref, q_ref, k_ref, v_ref, lam_ref, sub_ref, o_ref,
                 acc1_ref, acc2_ref, p_ref):
    tq = q_ref.shape[2]
    nkb = v_ref.shape[0]
    tk = v_ref.shape[2]
    q = q_ref[0]
    rows = lax.broadcasted_iota(jnp.int32, (V_DIM, 1), 0)
    zero = jnp.zeros_like(q)
    q1 = jnp.where(rows < HEAD_DIM, q, zero)
    q2 = jnp.where(rows >= HEAD_DIM, q, zero)
    acc1_ref[...] = jnp.zeros_like(acc1_ref)
    acc2_ref[...] = jnp.zeros_like(acc2_ref)
    zer = jnp.zeros((1, tq), F32)

    def load_kv(kb):
        k = k_ref[pl.ds(pl.multiple_of(kb * tk, tk), tk), :]
        return k, v_ref[kb]

    def finish(l1, l2):
        lp = lam_ref[...]
        lam = (jnp.exp(jnp.sum(lp[0:1, :] * lp[1:2, :], axis=-1, keepdims=True))
               - jnp.exp(jnp.sum(lp[2:3, :] * lp[3:4, :], axis=-1, keepdims=True))
               + lambda_init)
        o = acc1_ref[...] / l1 - lam * (acc2_ref[...] / l2)
        ms = jnp.mean(o * o, axis=0, keepdims=True)
        o = o * lax.rsqrt(ms + RMS_EPS) * sub_ref[...] * (1.0 - lambda_init)
        o_ref[0] = o.astype(BF16)

    @pl.when(unshifted_ref[0] == 1)
    def _():
        def scores(kb, l1, l2, slot):
            k = load_kv(kb)[0]
            p1 = jnp.exp2(_dot(k, q1))
            p2 = jnp.exp2(_dot(k, q2))
            p_ref[slot, 0] = p1.astype(BF16)
            p_ref[slot, 1] = p2.astype(BF16)
            return l1 + jnp.sum(p1, axis=0, keepdims=True), l2 + jnp.sum(p2, axis=0, keepdims=True)

        def values(kb, slot):
            v = v_ref[kb]
            acc1_ref[...] += _dot(v, p_ref[slot, 0])
            acc2_ref[...] += _dot(v, p_ref[slot, 1])

        l1, l2 = scores(0, zer, zer, 0)
        for kb in range(1, nkb):
            l1, l2 = scores(kb, l1, l2, kb % 2)
            values(kb - 1, (kb - 1) % 2)
        values(nkb - 1, (nkb - 1) % 2)
        finish(l1, l2)

    @pl.when(unshifted_ref[0] == 0)
    def _():
        def one_map(k, v, qc, m, l, acc_ref):
            s = _dot(k, qc)
            m_new = jnp.maximum(m, jnp.max(s, axis=0, keepdims=True))
            alpha = jnp.exp2(m - m_new)
            p = jnp.exp2(s - m_new)
            acc_ref[...] = alpha * acc_ref[...] + _dot(v, p.astype(BF16))
            return m_new, alpha * l + jnp.sum(p, axis=0, keepdims=True)

        def body(kb, carry):
            k, v = load_kv(kb)
            m1, l1 = one_map(k, v, q1, carry[0], carry[1], acc1_ref)
            m2, l2 = one_map(k, v, q2, carry[2], carry[3], acc2_ref)
            return m1, l1, m2, l2

        neg = jnp.full((1, tq), _NEG_BIG, F32)
        _, l1, _, l2 = lax.fori_loop(0, nkb, body, (neg, zer, neg, zer))
        finish(l1, l2)


def _attn_out_router_kernel(o_ref, x_ref, w_ref, g_ref, rt_ref, x2_ref, h2_ref, idx_ref, gate_ref):
    tn = (((0,), (0,)), ((), ()))
    y = lax.dot_general(o_ref[0], w_ref[...], tn, preferred_element_type=F32)
    x2 = x_ref[...] + y
    x2_ref[...] = x2
    h2 = _rms_rows(x2, g_ref[...])
    h2_ref[...] = h2
    nt = (((1,), (1,)), ((), ()))
    logits = lax.dot_general(rt_ref[...], h2, nt, precision=lax.Precision.HIGHEST,
                             preferred_element_type=F32)
    ne = logits.shape[0]
    eid = lax.broadcasted_iota(jnp.int32, logits.shape, 0)
    m1 = jnp.max(logits, axis=0, keepdims=True)
    i1 = jnp.min(jnp.where(logits == m1, eid, ne), axis=0, keepdims=True)
    rest = jnp.where(eid == i1, -jnp.inf, logits)
    m2 = jnp.max(rest, axis=0, keepdims=True)
    i2 = jnp.min(jnp.where(rest == m2, eid, ne), axis=0, keepdims=True)
    e2 = jnp.exp(m2 - m1)
    den = 1.0 + e2
    idx_ref[0:1, :] = i1
    idx_ref[1:2, :] = i2
    gate_ref[0:1, :] = 1.0 / den
    gate_ref[1:2, :] = e2 / den


def _diff_attention_and_router(x, positions, g_mix, w_in, q_norm, k_norm, lam_params, subln, w_out,
                               g_ffn, router, lambda_init, batch, seq):
    t, d = x.shape
    tm = TOKEN_TILE
    nt = t // tm
    inv_freq = ROPE_THETA ** (-jnp.arange(0, ROT_DIM, 2, dtype=F32) / ROT_DIM)
    q, k, v = pl.pallas_call(
        _qkv_kernel,
        grid=(nt,),
        in_specs=[
            pl.BlockSpec((tm, d), lambda i: (i, 0)),
            pl.BlockSpec((1, d), lambda i: (0, 0)),
            pl.BlockSpec((3 * d, d), lambda i: (0, 0)),
            pl.BlockSpec((1, tm), lambda i: (0, i)),
            pl.BlockSpec((ROT_DIM // 2, 1), lambda i: (0, 0)),
            pl.BlockSpec((HEAD_DIM, 1), lambda i: (0, 0)),
            pl.BlockSpec((HEAD_DIM, 1), lambda i: (0, 0)),
        ],
        out_specs=[
            pl.BlockSpec((1, d, tm), lambda i: (i, 0, 0)),
            pl.BlockSpec((tm, d), lambda i: (i, 0)),
            pl.BlockSpec((1, d, tm), lambda i: (i, 0, 0)),
        ],
        out_shape=[
            jax.ShapeDtypeStruct((nt, d, tm), BF16),
            jax.ShapeDtypeStruct((t, d), BF16),
            jax.ShapeDtypeStruct((nt, d, tm), BF16),
        ],
        compiler_params=_cparams("parallel"),
        name="attn_qkv",
    )(x, g_mix, w_in.T, positions.reshape(1, t), inv_freq.reshape(-1, 1),
      q_norm.reshape(-1, 1), k_norm.reshape(-1, 1))

    nq = seq // tm
    score_bound = (HEAD_DIM ** 0.5 * LOG2E * _BF16_ROUNDING_MARGIN
                   * jnp.max(jnp.abs(q_norm)) * jnp.max(jnp.abs(k_norm)))
    unshifted = (score_bound <= _EXP2_SAFE_LOG2).astype(jnp.int32).reshape(1)
    o = pl.pallas_call(
        functools.partial(_attn_kernel, lambda_init),
        grid_spec=pltpu.PrefetchScalarGridSpec(
            num_scalar_prefetch=1,
            grid=(batch, N_HEADS, nq),
            in_specs=[
                pl.BlockSpec((1, V_DIM, tm), lambda b, h, i, f: (b * nq + i, h, 0)),
                pl.BlockSpec((seq, V_DIM), lambda b, h, i, f: (b, h)),
                pl.BlockSpec((nq, V_DIM, tm), lambda b, h, i, f: (b, h, 0)),
                pl.BlockSpec((4, HEAD_DIM), lambda b, h, i, f: (0, 0)),
                pl.BlockSpec((V_DIM, 1), lambda b, h, i, f: (0, 0)),
            ],
            out_specs=pl.BlockSpec((1, V_DIM, tm), lambda b, h, i, f: (b * nq + i, h, 0)),
            scratch_shapes=[pltpu.VMEM((V_DIM, tm), F32), pltpu.VMEM((V_DIM, tm), F32),
                            pltpu.VMEM((2, 2, tm, tm), BF16)],
        ),
        out_shape=jax.ShapeDtypeStruct((nt, d, tm), BF16),
        compiler_params=_cparams("parallel", "parallel", "arbitrary"),
        name="diff_attn",
    )(unshifted, q, k, v, lam_params, subln.reshape(-1, 1))

    return pl.pallas_call(
        _attn_out_router_kernel,
        grid=(nt,),
        in_specs=[
            pl.BlockSpec((1, d, tm), lambda i: (i, 0, 0)),
            pl.BlockSpec((tm, d), lambda i: (i, 0)),
            pl.BlockSpec((d, d), lambda i: (0, 0)),
            pl.BlockSpec((1, d), lambda i: (0, 0)),
            pl.BlockSpec((N_EXPERTS, d), lambda i: (0, 0)),
        ],
        out_specs=[
            pl.BlockSpec((tm, d), lambda i: (i, 0)),
            pl.BlockSpec((tm, d), lambda i: (i, 0)),
            pl.BlockSpec((2, tm), lambda i: (0, i)),
            pl.BlockSpec((2, tm), lambda i: (0, i)),
        ],
        out_shape=[
            jax.ShapeDtypeStruct((t, d), F32),
            jax.ShapeDtypeStruct((t, d), F32),
            jax.ShapeDtypeStruct((2, t), jnp.int32),
            jax.ShapeDtypeStruct((2, t), F32),
        ],
        compiler_params=_cparams("parallel"),
        name="attn_out_router",
    )(o, x, w_out, g_ffn, router.T)


def _routing_plan(idx, n_tiles):
    t = idx.shape[1]
    hot = (jax.nn.one_hot(idx[0], N_EXPERTS, dtype=jnp.int32)
           + jax.nn.one_hot(idx[1], N_EXPERTS, dtype=jnp.int32))
    incl = jnp.cumsum(hot, axis=0)
    rank = incl - hot
    tiles = (incl[-1] + MOE_TILE - 1) // MOE_TILE
    tile_end = jnp.cumsum(tiles)
    row_start = (tile_end - tiles) * MOE_TILE
    slot = row_start[None, :] + rank
    pos = jnp.stack([jnp.take_along_axis(slot, idx[c][:, None], axis=1)[:, 0] for c in (0, 1)])
    tile_id = jnp.arange(n_tiles, dtype=jnp.int32)
    tile_expert = jnp.minimum(jnp.sum(tile_id[:, None] >= tile_end[None, :], axis=1), N_EXPERTS - 1)
    tile_active = (tile_id < tile_end[-1]).astype(jnp.int32)
    last_tile = jnp.where(tiles > 0, tile_end - 1, -1)
    tail = tile_end[-1] + jnp.arange(N_EXPERTS, dtype=jnp.int32)
    pad_tiles = jnp.concatenate([last_tile, jnp.where(tail < n_tiles, tail, -1)])
    return (pos.astype(jnp.int32), tile_expert.astype(jnp.int32), tile_active,
            pad_tiles.astype(jnp.int32))


def _dispatch_kernel(pad_ref, pos_ref, h_ref, xs_hbm, zero_ref, sem):
    i = pl.program_id(0)
    n = h_ref.shape[0]
    mt = zero_ref.shape[0]

    @pl.when(i == 0)
    def _():
        zero_ref[...] = jnp.zeros_like(zero_ref)
        fills = [pltpu.make_async_copy(
            zero_ref, xs_hbm.at[pl.ds(pl.multiple_of(jnp.maximum(pad_ref[j], 0) * mt, mt), mt)], sem)
            for j in range(pad_ref.shape[0])]
        for j, fill in enumerate(fills):
            pl.when(pad_ref[j] >= 0)(fill.start)
        for j, fill in enumerate(fills):
            pl.when(pad_ref[j] >= 0)(fill.wait)

    def issue(r, carry):
        for c in (0, 1):
            pltpu.make_async_copy(h_ref.at[pl.ds(r, 1)],
                                  xs_hbm.at[pl.ds(pos_ref[0, 0, c * n + r], 1)], sem).start(priority=c)
        return carry

    lax.fori_loop(0, n, issue, 0, unroll=8)
    for _ in (0, 1):
        pltpu.make_async_copy(h_ref, xs_hbm.at[pl.ds(0, n)], sem).wait()


def _expert_kernel(te_ref, ta_ref, xs_ref, w1_ref, w3_ref, w2_ref, y_ref):
    i = pl.program_id(0)

    @pl.when(ta_ref[i] == 1)
    def _():
        x = xs_ref[...].astype(BF16)
        acc = None
        for f0, fl in _ff_chunks(w1_ref.shape[1]):
            a = _dot(x, w1_ref[:, f0:f0 + fl])
            b = _dot(x, w3_ref[:, f0:f0 + fl])
            act = (_silu(a) * b).astype(BF16)
            part = _dot(act, w2_ref[f0:f0 + fl, :])
            acc = part if acc is None else acc + part
        y_ref[...] = acc

    @pl.when(ta_ref[i] == 0)
    def _():
        y_ref[...] = jnp.zeros_like(y_ref)


def _combine_kernel(pos_ref, pos_next_ref, x_ref, g_ref, y_hbm, o_ref, buf, sem):
    i = pl.program_id(0)
    n = x_ref.shape[0]

    def gather(p_ref, slot):
        def issue(r, carry):
            for c in (0, 1):
                pltpu.make_async_copy(y_hbm.at[pl.ds(p_ref[0, 0, c * n + r], 1)],
                                      buf.at[slot, c, pl.ds(r, 1)], sem.at[slot]).start(priority=c)
            return carry
        lax.fori_loop(0, n, issue, 0, unroll=8)

    @pl.when(i == 0)
    def _():
        gather(pos_ref, 0)

    @pl.when(i + 1 < pl.num_programs(0))
    def _():
        gather(pos_next_ref, (i + 1) % 2)

    slot = i % 2
    for c in (0, 1):
        pltpu.make_async_copy(y_hbm.at[pl.ds(0, n)], buf.at[slot, c], sem.at[slot]).wait()
    g = g_ref[...]
    o_ref[...] = x_ref[...] + g[:, 0:1] * buf[slot, 0] + g[:, 1:2] * buf[slot, 1]


def _moe(x2, h2, idx, gates, w1, w3, w2):
    t, d = x2.shape
    fe = w1.shape[2]
    n_tiles = (2 * t) // MOE_TILE + N_EXPERTS
    rows = n_tiles * MOE_TILE
    pos, tile_expert, tile_active, pad_tiles = _routing_plan(idx, n_tiles)
    gt = GATHER_TILE
    ng = t // gt
    pos_blocks = pos.reshape(2, ng, gt).transpose(1, 0, 2).reshape(ng, 1, 2 * gt)

    xs = pl.pallas_call(
        _dispatch_kernel,
        grid_spec=pltpu.PrefetchScalarGridSpec(
            num_scalar_prefetch=1,
            grid=(ng,),
            in_specs=[
                pl.BlockSpec((1, 1, 2 * gt), lambda i, pad: (i, 0, 0), memory_space=pltpu.SMEM),
                pl.BlockSpec((gt, d), lambda i, pad: (i, 0)),
            ],
            out_specs=pl.BlockSpec(memory_space=pl.ANY),
            scratch_shapes=[pltpu.VMEM((MOE_TILE, d), F32), pltpu.SemaphoreType.DMA],
        ),
        out_shape=jax.ShapeDtypeStruct((rows, d), F32),
        compiler_params=_cparams("arbitrary"),
        name="moe_dispatch",
    )(pad_tiles, pos_blocks, h2)

    y = pl.pallas_call(
        _expert_kernel,
        grid_spec=pltpu.PrefetchScalarGridSpec(
            num_scalar_prefetch=2,
            grid=(n_tiles,),
            in_specs=[
                pl.BlockSpec((MOE_TILE, d), lambda i, te, ta: (i, 0)),
                pl.BlockSpec((None, d, fe), lambda i, te, ta: (te[i], 0, 0),
                             pipeline_mode=pl.Buffered(1)),
                pl.BlockSpec((None, d, fe), lambda i, te, ta: (te[i], 0, 0),
                             pipeline_mode=pl.Buffered(1)),
                pl.BlockSpec((None, fe, d), lambda i, te, ta: (te[i], 0, 0),
                             pipeline_mode=pl.Buffered(1)),
            ],
            out_specs=pl.BlockSpec((MOE_TILE, d), lambda i, te, ta: (i, 0)),
        ),
        out_shape=jax.ShapeDtypeStruct((rows, d), F32),
        compiler_params=_cparams("arbitrary"),
        name="moe_experts",
    )(tile_expert, tile_active, xs, w1, w3, w2)

    return pl.pallas_call(
        _combine_kernel,
        grid=(ng,),
        in_specs=[
            pl.BlockSpec((1, 1, 2 * gt), lambda i: (i, 0, 0), memory_space=pltpu.SMEM),
            pl.BlockSpec((1, 1, 2 * gt), lambda i: (jnp.minimum(i + 1, ng - 1), 0, 0),
                         memory_space=pltpu.SMEM),
            pl.BlockSpec((gt, d), lambda i: (i, 0)),
            pl.BlockSpec((gt, 2), lambda i: (i, 0)),
            pl.BlockSpec(memory_space=pl.ANY),
        ],
        out_specs=pl.BlockSpec((gt, d), lambda i: (i, 0)),
        out_shape=jax.ShapeDtypeStruct((t, d), F32),
        scratch_shapes=[pltpu.VMEM((2, 2, gt, d), F32), pltpu.SemaphoreType.DMA((2,))],
        compiler_params=_cparams("arbitrary"),
        name="moe_combine",
    )(pos_blocks, pos_blocks, x2, gates.T, y)


def kernel(x, positions, norm_mix, norm_ffn, conv_in, conv_w, conv_out, attn_in, q_norm, k_norm,
           lam_q1, lam_k1, lam_q2, lam_k2, subln, attn_out, ffn_w1, ffn_w3, ffn_w2,
           router, moe_w1, moe_w3, moe_w2):
    batch, seq, d = x.shape
    xt = x.reshape(batch * seq, d)
    bf = lambda w: w.astype(BF16)

    xt = _conv_mixer(xt, norm_mix[0:1], bf(conv_in[0]), conv_w[0], bf(conv_out[0]), seq)
    xt = _dense_ffn(xt, norm_ffn[0:1], bf(ffn_w1[0]), bf(ffn_w3[0]), bf(ffn_w2[0]))

    lambda_init = 0.8 - 0.6 * math.exp(-0.3 * 1)
    lam_params = jnp.concatenate([lam_q1, lam_k1, lam_q2, lam_k2], axis=0)
    x2, h2, idx, gates = _diff_attention_and_router(
        xt, positions, norm_mix[1:2], bf(attn_in[0]), q_norm[0], k_norm[0], lam_params, subln[0],
        bf(attn_out[0]), norm_ffn[1:2], router[0], lambda_init, batch, seq)
    out = _moe(x2, h2, idx, gates, bf(moe_w1[0]), bf(moe_w3[0]), bf(moe_w2[0]))
    return out.reshape(batch, seq, d)
```

```python
import functools
import math

import jax
import jax.numpy as jnp
from jax import lax
from jax.experimental import pallas as pl
from jax.experimental.pallas import tpu as pltpu

F32 = jnp.float32
BF16 = jnp.bfloat16

D_MODEL = 1024
N_HEADS = 8
HEAD_DIM = 64
V_DIM = 2 * HEAD_DIM
ROT_DIM = HEAD_DIM // 4
ROPE_THETA = 500000.0
N_EXPERTS = 8
RMS_EPS = 1e-6
LOG2E = 1.4426950408889634
LANES = 128
ROW_SUBLANES = 8

TOKEN_TILE = 512
ATTN_KV_TILE = 512
MOE_TILE = 512
FF_CHUNK = 512
GATHER_TILE = 256
VMEM_LIMIT = 56 * 1024 * 1024

_NEG_BIG = -1e30
_EXP2_SAFE_LOG2 = 60.0
_BF16_ROUNDING_MARGIN = 1.02


def _cparams(*sem):
    return pltpu.CompilerParams(dimension_semantics=sem, vmem_limit_bytes=VMEM_LIMIT)


def _rms_rows(x, g):
    ms = jnp.mean(x * x, axis=-1, keepdims=True)
    return x * lax.rsqrt(ms + RMS_EPS) * g


def _dot(a, b):
    return jnp.dot(a, b, preferred_element_type=F32)


def _silu(a):
    return a * (1.0 / (1.0 + jnp.exp(-a)))


def _ff_chunks(width):
    out, f0 = [], 0
    while f0 < width:
        fl = min(FF_CHUNK, width - f0)
        out.append((f0, fl))
        f0 += fl
    return out


def _conv_in_kernel(x_ref, g_ref, w_ref, b_ref, u_ref):
    d = x_ref.shape[1]
    h = _rms_rows(x_ref[...], g_ref[...]).astype(BF16)
    b_ref[...] = _dot(h, w_ref[:, 0:d]).astype(BF16)
    c = _dot(h, w_ref[:, d:2 * d])
    v = _dot(h, w_ref[:, 2 * d:3 * d])
    u_ref[...] = (c * v).astype(BF16)


def _conv_out_ffn_kernel(tiles_per_seq, u_ref, up_ref, un_ref, b_ref, x_ref, cw_ref, w_ref,
                         g_ref, w1_ref, w3_ref, w2_ref, o_ref):
    i = pl.program_id(0)
    tm = u_ref.shape[0]
    halo = up_ref.shape[0]
    u = u_ref[...].astype(F32)
    first = (i % tiles_per_seq) == 0
    last = (i % tiles_per_seq) == tiles_per_seq - 1
    prev_row = jnp.where(first, 0.0, up_ref[halo - 1:halo, :].astype(F32))
    next_row = jnp.where(last, 0.0, un_ref[0:1, :].astype(F32))
    rows = lax.broadcasted_iota(jnp.int32, (tm, 1), 0)
    u_m1 = jnp.where(rows == 0, prev_row, pltpu.roll(u, 1, 0))
    u_p1 = jnp.where(rows == tm - 1, next_row, pltpu.roll(u, tm - 1, 0))
    cw = cw_ref[...]
    conv = u_m1 * cw[0:1, :] + u * cw[1:2, :] + u_p1 * cw[2:3, :]
    y = (b_ref[...].astype(F32) * conv).astype(BF16)
    x1 = x_ref[...] + _dot(y, w_ref[...])
    h = _rms_rows(x1, g_ref[...]).astype(BF16)
    acc = x1
    for f0, fl in _ff_chunks(w1_ref.shape[1]):
        a = _dot(h, w1_ref[:, f0:f0 + fl])
        b = _dot(h, w3_ref[:, f0:f0 + fl])
        act = (_silu(a) * b).astype(BF16)
        acc = acc + _dot(act, w2_ref[f0:f0 + fl, :])
    o_ref[...] = acc


def _resident(shape):
    return pl.BlockSpec(shape, lambda i: (0,) * len(shape), pipeline_mode=pl.Buffered(1))


def _layer0(x, g_mix, w_in, conv_w, w_out, g_ffn, w1, w3, w2, seq):
    t, d = x.shape
    f = w1.shape[1]
    tm = TOKEN_TILE
    nt = t // tm
    halo = 16
    b, u = pl.pallas_call(
        _conv_in_kernel,
        grid=(nt,),
        in_specs=[
            pl.BlockSpec((tm, d), lambda i: (i, 0)),
            _resident((1, d)),
            _resident((d, 3 * d)),
        ],
        out_specs=[pl.BlockSpec((tm, d), lambda i: (i, 0))] * 2,
        out_shape=[jax.ShapeDtypeStruct((t, d), BF16)] * 2,
        compiler_params=_cparams("parallel"),
        name="conv_in",
    )(x, g_mix, w_in)
    hb = tm // halo
    nhb = t // halo
    return pl.pallas_call(
        functools.partial(_conv_out_ffn_kernel, seq // tm),
        grid=(nt,),
        in_specs=[
            pl.BlockSpec((tm, d), lambda i: (i, 0)),
            pl.BlockSpec((halo, d), lambda i: (jnp.maximum(i * hb - 1, 0), 0)),
            pl.BlockSpec((halo, d), lambda i: (jnp.minimum((i + 1) * hb, nhb - 1), 0)),
            pl.BlockSpec((tm, d), lambda i: (i, 0)),
            pl.BlockSpec((tm, d), lambda i: (i, 0)),
            _resident((3, d)),
            _resident((d, d)),
            _resident((1, d)),
            _resident((d, f)),
            _resident((d, f)),
            _resident((f, d)),
        ],
        out_specs=pl.BlockSpec((tm, d), lambda i: (i, 0)),
        out_shape=jax.ShapeDtypeStruct((t, d), F32),
        compiler_params=_cparams("parallel"),
        name="conv_out_ffn",
    )(u, u, u, b, x, conv_w, w_out, g_ffn, w1, w3, w2)


def _qkv_kernel(x_ref, g_ref, wt_ref, pos_ref, freq_ref, qn_ref, kn_ref, q_ref, k_ref, v_ref):
    tm, d = x_ref.shape
    h = _rms_rows(x_ref[...], g_ref[...]).astype(BF16)
    nt = (((1,), (1,)), ((), ()))
    ang = pos_ref[...].astype(F32) * freq_ref[...]
    cos = jnp.cos(ang)[None]
    sin = jnp.sin(ang)[None]
    half = ROT_DIM // 2
    groups = d // HEAD_DIM

    def norm_rope(w_rows, gain, scale):
        t = lax.dot_general(w_rows, h, nt, preferred_element_type=F32)
        t = t.reshape(groups, HEAD_DIM, tm)
        ms = jnp.mean(t * t, axis=1, keepdims=True)
        t = t * lax.rsqrt(ms + RMS_EPS) * gain[None]
        t1 = t[:, 0:half, :]
        t2 = t[:, half:ROT_DIM, :]
        t = jnp.concatenate([t1 * cos - t2 * sin, t2 * cos + t1 * sin, t[:, ROT_DIM:, :]], axis=1)
        return (t * scale).reshape(d, tm)

    q = norm_rope(wt_ref[0:d, :], qn_ref[...], (HEAD_DIM ** -0.5) * LOG2E)
    q_ref[0] = q.astype(BF16)
    k = norm_rope(wt_ref[d:2 * d, :], kn_ref[...], 1.0)
    k_ref[...] = k.T.astype(BF16)
    v = lax.dot_general(wt_ref[2 * d:3 * d, :], h, nt, preferred_element_type=F32)
    v_ref[0] = v.astype(BF16)


def _attn_kernel(lambda_init, unshifted_ref, q_ref, k_ref, v_ref, lam_ref, sub_ref, o_ref,
                 acc1_ref, acc2_ref, p_ref):
    tq = q_ref.shape[2]
    nkb = v_ref.shape[0]
    tk = v_ref.shape[2]
    q = q_ref[0]
    rows = lax.broadcasted_iota(jnp.int32, (V_DIM, 1), 0)
    zero = jnp.zeros_like(q)
    q1 = jnp.where(rows < HEAD_DIM, q, zero)
    q2 = jnp.where(rows >= HEAD_DIM, q, zero)
    acc1_ref[...] = jnp.zeros_like(acc1_ref)
    acc2_ref[...] = jnp.zeros_like(acc2_ref)
    zer = jnp.zeros((1, tq), F32)

    def load_kv(kb):
        k = k_ref[pl.ds(pl.multiple_of(kb * tk, tk), tk), :]
        return k, v_ref[kb]

    def finish(l1, l2):
        lp = lam_ref[...]
        lam = (jnp.exp(jnp.sum(lp[0:1, :] * lp[1:2, :], axis=-1, keepdims=True))
               - jnp.exp(jnp.sum(lp[2:3, :] * lp[3:4, :], axis=-1, keepdims=True))
               + lambda_init)
        o = acc1_ref[...] / l1 - lam * (acc2_ref[...] / l2)
        ms = jnp.mean(o * o, axis=0, keepdims=True)
        o = o * lax.rsqrt(ms + RMS_EPS) * sub_ref[...] * (1.0 - lambda_init)
        o_ref[0] = o.astype(BF16)

    @pl.when(unshifted_ref[0] == 1)
    def _():
        def scores(kb, l1, l2, slot):
            k = load_kv(kb)[0]
            p1 = jnp.exp2(_dot(k, q1))
            p2 = jnp.exp2(_dot(k, q2))
            p_ref[slot, 0] = p1.astype(BF16)
            p_ref[slot, 1] = p2.astype(BF16)
            return l1 + jnp.sum(p1, axis=0, keepdims=True), l2 + jnp.sum(p2, axis=0, keepdims=True)

        def values(kb, slot):
            v = v_ref[kb]
            acc1_ref[...] += _dot(v, p_ref[slot, 0])
            acc2_ref[...] += _dot(v, p_ref[slot, 1])

        l1, l2 = scores(0, zer, zer, 0)
        for kb in range(1, nkb):
            l1, l2 = scores(kb, l1, l2, kb % 2)
            values(kb - 1, (kb - 1) % 2)
        values(nkb - 1, (nkb - 1) % 2)
        finish(l1, l2)

    @pl.when(unshifted_ref[0] == 0)
    def _():
        def one_map(k, v, qc, m, l, acc_ref):
            s = _dot(k, qc)
            m_new = jnp.maximum(m, jnp.max(s, axis=0, keepdims=True))
            alpha = jnp.exp2(m - m_new)
            p = jnp.exp2(s - m_new)
            acc_ref[...] = alpha * acc_ref[...] + _dot(v, p.astype(BF16))
            return m_new, alpha * l + jnp.sum(p, axis=0, keepdims=True)

        def body(kb, carry):
            k, v = load_kv(kb)
            m1, l1 = one_map(k, v, q1, carry[0], carry[1], acc1_ref)
            m2, l2 = one_map(k, v, q2, carry[2], carry[3], acc2_ref)
            return m1, l1, m2, l2

        neg = jnp.full((1, tq), _NEG_BIG, F32)
        _, l1, _, l2 = lax.fori_loop(0, nkb, body, (neg, zer, neg, zer))
        finish(l1, l2)


def _store_row_tiles(ref, x):
    rows = x.shape[0]
    for j in range(ROW_SUBLANES):
        ref[pl.ds(j, rows, stride=ROW_SUBLANES), :] = x[:, j * LANES:(j + 1) * LANES]


def _load_row_tiles(ref, base, rows):
    return jnp.concatenate(
        [ref[pl.ds(base + j, rows, stride=ROW_SUBLANES), :] for j in range(ROW_SUBLANES)], axis=1)


def _attn_out_router_kernel(o_ref, x_ref, w_ref, g_ref, rt_ref, x2_ref, h2_ref, idx_ref, gate_ref):
    tn = (((0,), (0,)), ((), ()))
    y = lax.dot_general(o_ref[0], w_ref[...], tn, preferred_element_type=F32)
    x2 = x_ref[...] + y
    x2_ref[...] = x2
    h2 = _rms_rows(x2, g_ref[...])
    _store_row_tiles(h2_ref, h2)
    ne = rt_ref.shape[0]
    nt = (((1,), (1,)), ((), ()))
    r = rt_ref[...]
    r_hi = r.astype(BF16)
    r_lo = (r - r_hi.astype(F32)).astype(BF16)
    h_hi = h2.astype(BF16)
    h_lo = (h2 - h_hi.astype(F32)).astype(BF16)
    both = lax.dot_general(jnp.concatenate([r_hi, r_lo], axis=0), h_hi, nt, preferred_element_type=F32)
    logits = (both[0:ne] + both[ne:2 * ne]
              + lax.dot_general(r_hi, h_lo, nt, preferred_element_type=F32))
    eid = lax.broadcasted_iota(jnp.int32, logits.shape, 0)
    m1 = jnp.max(logits, axis=0, keepdims=True)
    i1 = jnp.min(jnp.where(logits == m1, eid, ne), axis=0, keepdims=True)
    rest = jnp.where(eid == i1, -jnp.inf, logits)
    m2 = jnp.max(rest, axis=0, keepdims=True)
    i2 = jnp.min(jnp.where(rest == m2, eid, ne), axis=0, keepdims=True)
    e2 = jnp.exp(m2 - m1)
    den = 1.0 + e2
    idx_ref[0:1, :] = i1
    idx_ref[1:2, :] = i2
    gate_ref[0:1, :] = 1.0 / den
    gate_ref[1:2, :] = e2 / den


def _diff_attention_and_router(x, positions, g_mix, w_in, q_norm, k_norm, lam_params, subln, w_out,
                               g_ffn, router, lambda_init, batch, seq):
    t, d = x.shape
    tm = TOKEN_TILE
    nt = t // tm
    inv_freq = ROPE_THETA ** (-jnp.arange(0, ROT_DIM, 2, dtype=F32) / ROT_DIM)
    q, k, v = pl.pallas_call(
        _qkv_kernel,
        grid=(nt,),
        in_specs=[
            pl.BlockSpec((tm, d), lambda i: (i, 0)),
            _resident((1, d)),
            _resident((3 * d, d)),
            pl.BlockSpec((1, tm), lambda i: (0, i)),
            _resident((ROT_DIM // 2, 1)),
            _resident((HEAD_DIM, 1)),
            _resident((HEAD_DIM, 1)),
        ],
        out_specs=[
            pl.BlockSpec((1, d, tm), lambda i: (i, 0, 0)),
            pl.BlockSpec((tm, d), lambda i: (i, 0)),
            pl.BlockSpec((1, d, tm), lambda i: (i, 0, 0)),
        ],
        out_shape=[
            jax.ShapeDtypeStruct((nt, d, tm), BF16),
            jax.ShapeDtypeStruct((t, d), BF16),
            jax.ShapeDtypeStruct((nt, d, tm), BF16),
        ],
        compiler_params=_cparams("parallel"),
        name="attn_qkv",
    )(x, g_mix, w_in.T, positions.reshape(1, t), inv_freq.reshape(-1, 1),
      q_norm.reshape(-1, 1), k_norm.reshape(-1, 1))

    nq = seq // tm
    score_bound = (HEAD_DIM ** 0.5 * LOG2E * _BF16_ROUNDING_MARGIN
                   * jnp.max(jnp.abs(q_norm)) * jnp.max(jnp.abs(k_norm)))
    unshifted = (score_bound <= _EXP2_SAFE_LOG2).astype(jnp.int32).reshape(1)
    o = pl.pallas_call(
        functools.partial(_attn_kernel, lambda_init),
        grid_spec=pltpu.PrefetchScalarGridSpec(
            num_scalar_prefetch=1,
            grid=(batch, N_HEADS, nq),
            in_specs=[
                pl.BlockSpec((1, V_DIM, tm), lambda b, h, i, f: (b * nq + i, h, 0)),
                pl.BlockSpec((seq, V_DIM), lambda b, h, i, f: (b, h)),
                pl.BlockSpec((nq, V_DIM, tm), lambda b, h, i, f: (b, h, 0)),
                pl.BlockSpec((4, HEAD_DIM), lambda b, h, i, f: (0, 0)),
                pl.BlockSpec((V_DIM, 1), lambda b, h, i, f: (0, 0)),
            ],
            out_specs=pl.BlockSpec((1, V_DIM, tm), lambda b, h, i, f: (b * nq + i, h, 0)),
            scratch_shapes=[pltpu.VMEM((V_DIM, tm), F32), pltpu.VMEM((V_DIM, tm), F32),
                            pltpu.VMEM((2, 2, tm, tm), BF16)],
        ),
        out_shape=jax.ShapeDtypeStruct((nt, d, tm), BF16),
        compiler_params=_cparams("parallel", "parallel", "arbitrary"),
        name="diff_attn",
    )(unshifted, q, k, v, lam_params, subln.reshape(-1, 1))

    x2, h2, idx, gates = pl.pallas_call(
        _attn_out_router_kernel,
        grid=(nt,),
        in_specs=[
            pl.BlockSpec((1, d, tm), lambda i: (i, 0, 0)),
            pl.BlockSpec((tm, d), lambda i: (i, 0)),
            _resident((d, d)),
            _resident((1, d)),
            _resident((N_EXPERTS, d)),
        ],
        out_specs=[
            pl.BlockSpec((tm, d), lambda i: (i, 0)),
            pl.BlockSpec((tm * ROW_SUBLANES, LANES), lambda i: (i, 0)),
            pl.BlockSpec((2, tm), lambda i: (0, i)),
            pl.BlockSpec((2, tm), lambda i: (0, i)),
        ],
        out_shape=[
            jax.ShapeDtypeStruct((t, d), F32),
            jax.ShapeDtypeStruct((t * ROW_SUBLANES, LANES), F32),
            jax.ShapeDtypeStruct((2, t), jnp.int32),
            jax.ShapeDtypeStruct((2, t), F32),
        ],
        compiler_params=_cparams("parallel"),
        name="attn_out_router",
    )(o, x, w_out, g_ffn, router.T)
    return x2, h2, idx.T, gates.T


def _routing_plan(idx, n_tiles):
    hot = (jax.nn.one_hot(idx[:, 0], N_EXPERTS, dtype=jnp.int32)
           + jax.nn.one_hot(idx[:, 1], N_EXPERTS, dtype=jnp.int32))
    incl = jnp.cumsum(hot, axis=0)
    rank = incl - hot
    tiles = (incl[-1] + MOE_TILE - 1) // MOE_TILE
    tile_end = jnp.cumsum(tiles)
    row_start = (tile_end - tiles) * MOE_TILE
    slot = row_start[None, :] + rank
    pos = jnp.take_along_axis(slot, idx, axis=1).T
    tile_id = jnp.arange(n_tiles, dtype=jnp.int32)
    tile_expert = jnp.minimum(jnp.sum(tile_id[:, None] >= tile_end[None, :], axis=1), N_EXPERTS - 1)
    tile_active = (tile_id < tile_end[-1]).astype(jnp.int32)
    last_tile = jnp.where(tiles > 0, tile_end - 1, -1)
    tail = tile_end[-1] + jnp.arange(N_EXPERTS, dtype=jnp.int32)
    pad_tiles = jnp.concatenate([last_tile, jnp.where(tail < n_tiles, tail, -1)])
    return (pos.astype(jnp.int32), tile_expert.astype(jnp.int32), tile_active,
            pad_tiles.astype(jnp.int32))


def _row_tile(ref, row):
    return ref.at[pl.ds(pl.multiple_of(row * ROW_SUBLANES, ROW_SUBLANES), ROW_SUBLANES)]


def _dispatch_kernel(pad_ref, pos_ref, h_ref, xs_hbm, zero_ref, sem):
    i = pl.program_id(0)
    n = h_ref.shape[0] // ROW_SUBLANES
    mt = zero_ref.shape[0]

    @pl.when(i == 0)
    def _():
        zero_ref[...] = jnp.zeros_like(zero_ref)
        fills = [pltpu.make_async_copy(
            zero_ref, xs_hbm.at[pl.ds(pl.multiple_of(jnp.maximum(pad_ref[j], 0) * mt, mt), mt)], sem)
            for j in range(pad_ref.shape[0])]
        for j, fill in enumerate(fills):
            pl.when(pad_ref[j] >= 0)(fill.start)
        for j, fill in enumerate(fills):
            pl.when(pad_ref[j] >= 0)(fill.wait)

    def issue(r, carry):
        for c in (0, 1):
            pltpu.make_async_copy(_row_tile(h_ref, r), _row_tile(xs_hbm, pos_ref[0, 0, c * n + r]),
                                  sem).start(priority=c)
        return carry

    lax.fori_loop(0, n, issue, 0, unroll=8)
    for _ in (0, 1):
        pltpu.make_async_copy(h_ref, xs_hbm.at[pl.ds(0, n * ROW_SUBLANES)], sem).wait()


def _expert_kernel(te_ref, ta_ref, xs_ref, w1_ref, w3_ref, w2_ref, y_ref):
    i = pl.program_id(0)
    mt = xs_ref.shape[0] // ROW_SUBLANES

    @pl.when(ta_ref[i] == 1)
    def _():
        x = _load_row_tiles(xs_ref, 0, mt).astype(BF16)
        acc = None
        for f0, fl in _ff_chunks(w1_ref.shape[1]):
            a = _dot(x, w1_ref[:, f0:f0 + fl])
            b = _dot(x, w3_ref[:, f0:f0 + fl])
            act = (_silu(a) * b).astype(BF16)
            part = _dot(act, w2_ref[f0:f0 + fl, :])
            acc = part if acc is None else acc + part
        _store_row_tiles(y_ref, acc)

    @pl.when(ta_ref[i] == 0)
    def _():
        y_ref[...] = jnp.zeros_like(y_ref)


def _combine_kernel(pos_ref, pos_next_ref, x_ref, g_ref, y_hbm, o_ref, buf, sem):
    i = pl.program_id(0)
    n = x_ref.shape[0]

    def gather(p_ref, slot):
        def issue(r, carry):
            for c in (0, 1):
                pltpu.make_async_copy(_row_tile(y_hbm, p_ref[0, 0, c * n + r]),
                                      _row_tile(buf, (slot * 2 + c) * n + r),
                                      sem.at[slot]).start(priority=c)
            return carry
        lax.fori_loop(0, n, issue, 0, unroll=8)

    @pl.when(i == 0)
    def _():
        gather(pos_ref, 0)

    @pl.when(i + 1 < pl.num_programs(0))
    def _():
        gather(pos_next_ref, (i + 1) % 2)

    slot = i % 2
    chunk = n * ROW_SUBLANES
    base = pl.multiple_of(slot * 2 * chunk, chunk)
    pltpu.make_async_copy(y_hbm.at[pl.ds(0, 2 * chunk)], buf.at[pl.ds(base, 2 * chunk)],
                          sem.at[slot]).wait()
    g = g_ref[...]
    o_ref[...] = (x_ref[...] + g[:, 0:1] * _load_row_tiles(buf, base, n)
                  + g[:, 1:2] * _load_row_tiles(buf, base + chunk, n))


def _moe(x2, h2, idx, gates, w1, w3, w2):
    t, d = x2.shape
    fe = w1.shape[2]
    assert d == ROW_SUBLANES * LANES
    rs = ROW_SUBLANES
    n_tiles = (2 * t) // MOE_TILE + N_EXPERTS
    rows = n_tiles * MOE_TILE
    pos, tile_expert, tile_active, pad_tiles = _routing_plan(idx, n_tiles)
    gt = GATHER_TILE
    ng = t // gt
    pos_blocks = pos.reshape(2, ng, gt).transpose(1, 0, 2).reshape(ng, 1, 2 * gt)

    xs = pl.pallas_call(
        _dispatch_kernel,
        grid_spec=pltpu.PrefetchScalarGridSpec(
            num_scalar_prefetch=1,
            grid=(ng,),
            in_specs=[
                pl.BlockSpec((1, 1, 2 * gt), lambda i, pad: (i, 0, 0), memory_space=pltpu.SMEM),
                pl.BlockSpec((gt * rs, LANES), lambda i, pad: (i, 0)),
            ],
            out_specs=pl.BlockSpec(memory_space=pl.ANY),
            scratch_shapes=[pltpu.VMEM((MOE_TILE * rs, LANES), F32), pltpu.SemaphoreType.DMA],
        ),
        out_shape=jax.ShapeDtypeStruct((rows * rs, LANES), F32),
        compiler_params=_cparams("arbitrary"),
        name="moe_dispatch",
    )(pad_tiles, pos_blocks, h2)

    y = pl.pallas_call(
        _expert_kernel,
        grid_spec=pltpu.PrefetchScalarGridSpec(
            num_scalar_prefetch=2,
            grid=(n_tiles,),
            in_specs=[
                pl.BlockSpec((MOE_TILE * rs, LANES), lambda i, te, ta: (i, 0)),
                pl.BlockSpec((None, d, fe), lambda i, te, ta: (te[i], 0, 0),
                             pipeline_mode=pl.Buffered(1)),
                pl.BlockSpec((None, d, fe), lambda i, te, ta: (te[i], 0, 0),
                             pipeline_mode=pl.Buffered(1)),
                pl.BlockSpec((None, fe, d), lambda i, te, ta: (te[i], 0, 0),
                             pipeline_mode=pl.Buffered(1)),
            ],
            out_specs=pl.BlockSpec((MOE_TILE * rs, LANES), lambda i, te, ta: (i, 0)),
        ),
        out_shape=jax.ShapeDtypeStruct((rows * rs, LANES), F32),
        compiler_params=_cparams("arbitrary"),
        name="moe_experts",
    )(tile_expert, tile_active, xs, w1, w3, w2)

    return pl.pallas_call(
        _combine_kernel,
        grid=(ng,),
        in_specs=[
            pl.BlockSpec((1, 1, 2 * gt), lambda i: (i, 0, 0), memory_space=pltpu.SMEM),
            pl.BlockSpec((1, 1, 2 * gt), lambda i: (jnp.minimum(i + 1, ng - 1), 0, 0),
                         memory_space=pltpu.SMEM),
            pl.BlockSpec((gt, d), lambda i: (i, 0)),
            pl.BlockSpec((gt, 2), lambda i: (i, 0)),
            pl.BlockSpec(memory_space=pl.ANY),
        ],
        out_specs=pl.BlockSpec((gt, d), lambda i: (i, 0)),
        out_shape=jax.ShapeDtypeStruct((t, d), F32),
        scratch_shapes=[pltpu.VMEM((2 * 2 * gt * rs, LANES), F32), pltpu.SemaphoreType.DMA((2,))],
        compiler_params=_cparams("arbitrary"),
        name="moe_combine",
    )(pos_blocks, pos_blocks, x2, gates, y)


def kernel(x, positions, norm_mix, norm_ffn, conv_in, conv_w, conv_out, attn_in, q_norm, k_norm,
           lam_q1, lam_k1, lam_q2, lam_k2, subln, attn_out, ffn_w1, ffn_w3, ffn_w2,
           router, moe_w1, moe_w3, moe_w2):
    batch, seq, d = x.shape
    xt = x.reshape(batch * seq, d)
    bf = lambda w: w.astype(BF16)

    xt = _layer0(xt, norm_mix[0:1], bf(conv_in[0]), conv_w[0], bf(conv_out[0]),
                 norm_ffn[0:1], bf(ffn_w1[0]), bf(ffn_w3[0]), bf(ffn_w2[0]), seq)

    lambda_init = 0.8 - 0.6 * math.exp(-0.3 * 1)
    lam_params = jnp.concatenate([lam_q1, lam_k1, lam_q2, lam_k2], axis=0)
    x2, h2, idx, gates = _diff_attention_and_router(
        xt, positions, norm_mix[1:2], bf(attn_in[0]), q_norm[0], k_norm[0], lam_params, subln[0],
        bf(attn_out[0]), norm_ffn[1:2], router[0], lambda_init, batch, seq)
    out = _moe(x2, h2, idx, gates, bf(moe_w1[0]), bf(moe_w3[0]), bf(moe_w2[0]))
    return out.reshape(batch, seq, d)
```

```python
import functools
import math

import jax
import jax.numpy as jnp
from jax import lax
from jax.experimental import pallas as pl
from jax.experimental.pallas import tpu as pltpu

F32 = jnp.float32
BF16 = jnp.bfloat16

D_MODEL = 1024
N_HEADS = 8
HEAD_DIM = 64
V_DIM = 2 * HEAD_DIM
ROT_DIM = HEAD_DIM // 4
ROPE_THETA = 500000.0
N_EXPERTS = 8
RMS_EPS = 1e-6
LOG2E = 1.4426950408889634
LANES = 128
ROW_SUBLANES = 8

TOKEN_TILE = 512
ATTN_KV_TILE = 512
MOE_TILE = 256
FF_CHUNK = 512
GATHER_TILE = 256
DISPATCH_RING = 3
VMEM_LIMIT = 56 * 1024 * 1024

_NEG_BIG = -1e30
_EXP2_SAFE_LOG2 = 60.0
_BF16_ROUNDING_MARGIN = 1.02


def _cparams(*sem):
    return pltpu.CompilerParams(dimension_semantics=sem, vmem_limit_bytes=VMEM_LIMIT)


def _rms_rows(x, g):
    ms = jnp.mean(x * x, axis=-1, keepdims=True)
    return x * lax.rsqrt(ms + RMS_EPS) * g


def _dot(a, b):
    return jnp.dot(a, b, preferred_element_type=F32)


def _silu(a):
    return a * (1.0 / (1.0 + jnp.exp(-a)))


def _ff_chunks(width):
    out, f0 = [], 0
    while f0 < width:
        fl = min(FF_CHUNK, width - f0)
        out.append((f0, fl))
        f0 += fl
    return out


def _conv_in_kernel(x_ref, g_ref, w_ref, b_ref, u_ref):
    d = x_ref.shape[1]
    h = _rms_rows(x_ref[...], g_ref[...]).astype(BF16)
    b_ref[...] = _dot(h, w_ref[:, 0:d]).astype(BF16)
    c = _dot(h, w_ref[:, d:2 * d])
    v = _dot(h, w_ref[:, 2 * d:3 * d])
    u_ref[...] = (c * v).astype(BF16)


def _conv_out_ffn_kernel(tiles_per_seq, u_ref, up_ref, un_ref, b_ref, x_ref, cw_ref, w_ref,
                         g_ref, w1_ref, w3_ref, w2_ref, o_ref):
    i = pl.program_id(0)
    tm = u_ref.shape[0]
    halo = up_ref.shape[0]
    u = u_ref[...].astype(F32)
    first = (i % tiles_per_seq) == 0
    last = (i % tiles_per_seq) == tiles_per_seq - 1
    prev_row = jnp.where(first, 0.0, up_ref[halo - 1:halo, :].astype(F32))
    next_row = jnp.where(last, 0.0, un_ref[0:1, :].astype(F32))
    rows = lax.broadcasted_iota(jnp.int32, (tm, 1), 0)
    u_m1 = jnp.where(rows == 0, prev_row, pltpu.roll(u, 1, 0))
    u_p1 = jnp.where(rows == tm - 1, next_row, pltpu.roll(u, tm - 1, 0))
    cw = cw_ref[...]
    conv = u_m1 * cw[0:1, :] + u * cw[1:2, :] + u_p1 * cw[2:3, :]
    y = (b_ref[...].astype(F32) * conv).astype(BF16)
    x1 = x_ref[...] + _dot(y, w_ref[...])
    h = _rms_rows(x1, g_ref[...]).astype(BF16)
    acc = x1
    for f0, fl in _ff_chunks(w1_ref.shape[1]):
        a = _dot(h, w1_ref[:, f0:f0 + fl])
        b = _dot(h, w3_ref[:, f0:f0 + fl])
        act = (_silu(a) * b).astype(BF16)
        acc = acc + _dot(act, w2_ref[f0:f0 + fl, :])
    o_ref[...] = acc


def _resident(shape):
    return pl.BlockSpec(shape, lambda i: (0,) * len(shape), pipeline_mode=pl.Buffered(1))


def _layer0(x, g_mix, w_in, conv_w, w_out, g_ffn, w1, w3, w2, seq):
    t, d = x.shape
    f = w1.shape[1]
    tm = TOKEN_TILE
    nt = t // tm
    halo = 16
    b, u = pl.pallas_call(
        _conv_in_kernel,
        grid=(nt,),
        in_specs=[
            pl.BlockSpec((tm, d), lambda i: (i, 0)),
            _resident((1, d)),
            _resident((d, 3 * d)),
        ],
        out_specs=[pl.BlockSpec((tm, d), lambda i: (i, 0))] * 2,
        out_shape=[jax.ShapeDtypeStruct((t, d), BF16)] * 2,
        compiler_params=_cparams("parallel"),
        name="conv_in",
    )(x, g_mix, w_in)
    hb = tm // halo
    nhb = t // halo
    return pl.pallas_call(
        functools.partial(_conv_out_ffn_kernel, seq // tm),
        grid=(nt,),
        in_specs=[
            pl.BlockSpec((tm, d), lambda i: (i, 0)),
            pl.BlockSpec((halo, d), lambda i: (jnp.maximum(i * hb - 1, 0), 0)),
            pl.BlockSpec((halo, d), lambda i: (jnp.minimum((i + 1) * hb, nhb - 1), 0)),
            pl.BlockSpec((tm, d), lambda i: (i, 0)),
            pl.BlockSpec((tm, d), lambda i: (i, 0)),
            _resident((3, d)),
            _resident((d, d)),
            _resident((1, d)),
            _resident((d, f)),
            _resident((d, f)),
            _resident((f, d)),
        ],
        out_specs=pl.BlockSpec((tm, d), lambda i: (i, 0)),
        out_shape=jax.ShapeDtypeStruct((t, d), F32),
        compiler_params=_cparams("parallel"),
        name="conv_out_ffn",
    )(u, u, u, b, x, conv_w, w_out, g_ffn, w1, w3, w2)


def _qkv_kernel(x_ref, g_ref, wt_ref, pos_ref, freq_ref, qn_ref, kn_ref, q_ref, k_ref, v_ref):
    tm, d = x_ref.shape
    h = _rms_rows(x_ref[...], g_ref[...]).astype(BF16)
    nt = (((1,), (1,)), ((), ()))
    ang = pos_ref[...].astype(F32) * freq_ref[...]
    cos = jnp.cos(ang)[None]
    sin = jnp.sin(ang)[None]
    half = ROT_DIM // 2
    groups = d // HEAD_DIM

    def norm_rope(w_rows, gain, scale):
        t = lax.dot_general(w_rows, h, nt, preferred_element_type=F32)
        t = t.reshape(groups, HEAD_DIM, tm)
        ms = jnp.mean(t * t, axis=1, keepdims=True)
        t = t * lax.rsqrt(ms + RMS_EPS) * gain[None]
        t1 = t[:, 0:half, :]
        t2 = t[:, half:ROT_DIM, :]
        t = jnp.concatenate([t1 * cos - t2 * sin, t2 * cos + t1 * sin, t[:, ROT_DIM:, :]], axis=1)
        return (t * scale).reshape(d, tm)

    q = norm_rope(wt_ref[0:d, :], qn_ref[...], (HEAD_DIM ** -0.5) * LOG2E)
    q_ref[0] = q.astype(BF16)
    k = norm_rope(wt_ref[d:2 * d, :], kn_ref[...], 1.0)
    k_ref[...] = k.T.astype(BF16)
    v = lax.dot_general(wt_ref[2 * d:3 * d, :], h, nt, preferred_element_type=F32)
    v_ref[0] = v.astype(BF16)


def _attn_kernel(lambda_init, unshifted_ref, q_ref, k_ref, v_ref, lam_ref, sub_ref, o_ref,
                 acc1_ref, acc2_ref, p_ref):
    tq = q_ref.shape[2]
    nkb = v_ref.shape[0]
    tk = v_ref.shape[2]
    q = q_ref[0]
    rows = lax.broadcasted_iota(jnp.int32, (V_DIM, 1), 0)
    zero = jnp.zeros_like(q)
    q1 = jnp.where(rows < HEAD_DIM, q, zero)
    q2 = jnp.where(rows >= HEAD_DIM, q, zero)
    acc1_ref[...] = jnp.zeros_like(acc1_ref)
    acc2_ref[...] = jnp.zeros_like(acc2_ref)
    zer = jnp.zeros((1, tq), F32)

    def load_kv(kb):
        k = k_ref[pl.ds(pl.multiple_of(kb * tk, tk), tk), :]
        return k, v_ref[kb]

    def finish(l1, l2):
        lp = lam_ref[...]
        lam = (jnp.exp(jnp.sum(lp[0:1, :] * lp[1:2, :], axis=-1, keepdims=True))
               - jnp.exp(jnp.sum(lp[2:3, :] * lp[3:4, :], axis=-1, keepdims=True))
               + lambda_init)
        o = acc1_ref[...] / l1 - lam * (acc2_ref[...] / l2)
        ms = jnp.mean(o * o, axis=0, keepdims=True)
        o = o * lax.rsqrt(ms + RMS_EPS) * sub_ref[...] * (1.0 - lambda_init)
        o_ref[0] = o.astype(BF16)

    @pl.when(unshifted_ref[0] == 1)
    def _():
        def scores(kb, l1, l2, slot):
            k = load_kv(kb)[0]
            p1 = jnp.exp2(_dot(k, q1))
            p2 = jnp.exp2(_dot(k, q2))
            p_ref[slot, 0] = p1.astype(BF16)
            p_ref[slot, 1] = p2.astype(BF16)
            return l1 + jnp.sum(p1, axis=0, keepdims=True), l2 + jnp.sum(p2, axis=0, keepdims=True)

        def values(kb, slot):
            v = v_ref[kb]
            acc1_ref[...] += _dot(v, p_ref[slot, 0])
            acc2_ref[...] += _dot(v, p_ref[slot, 1])

        l1, l2 = scores(0, zer, zer, 0)
        for kb in range(1, nkb):
            l1, l2 = scores(kb, l1, l2, kb % 2)
            values(kb - 1, (kb - 1) % 2)
        values(nkb - 1, (nkb - 1) % 2)
        finish(l1, l2)

    @pl.when(unshifted_ref[0] == 0)
    def _():
        def one_map(k, v, qc, m, l, acc_ref):
            s = _dot(k, qc)
            m_new = jnp.maximum(m, jnp.max(s, axis=0, keepdims=True))
            alpha = jnp.exp2(m - m_new)
            p = jnp.exp2(s - m_new)
            acc_ref[...] = alpha * acc_ref[...] + _dot(v, p.astype(BF16))
            return m_new, alpha * l + jnp.sum(p, axis=0, keepdims=True)

        def body(kb, carry):
            k, v = load_kv(kb)
            m1, l1 = one_map(k, v, q1, carry[0], carry[1], acc1_ref)
            m2, l2 = one_map(k, v, q2, carry[2], carry[3], acc2_ref)
            return m1, l1, m2, l2

        neg = jnp.full((1, tq), _NEG_BIG, F32)
        _, l1, _, l2 = lax.fori_loop(0, nkb, body, (neg, zer, neg, zer))
        finish(l1, l2)


def _store_row_tiles(ref, x):
    rows = x.shape[0]
    for j in range(ROW_SUBLANES):
        ref[pl.ds(j, rows, stride=ROW_SUBLANES), :] = x[:, j * LANES:(j + 1) * LANES]


def _load_row_tiles(ref, base, rows):
    return jnp.concatenate(
        [ref[pl.ds(base + j, rows, stride=ROW_SUBLANES), :] for j in range(ROW_SUBLANES)], axis=1)


def _attn_out_router_kernel(o_ref, x_ref, w_ref, g_ref, rt_ref, x2_ref, h2_ref, idx_ref, gate_ref):
    tn = (((0,), (0,)), ((), ()))
    y = lax.dot_general(o_ref[0], w_ref[...], tn, preferred_element_type=F32)
    x2 = x_ref[...] + y
    x2_ref[...] = x2
    h2 = _rms_rows(x2, g_ref[...])
    _store_row_tiles(h2_ref, h2)
    ne = rt_ref.shape[0]
    nt = (((1,), (1,)), ((), ()))
    r = rt_ref[...]
    r_hi = r.astype(BF16)
    r_lo = (r - r_hi.astype(F32)).astype(BF16)
    h_hi = h2.astype(BF16)
    h_lo = (h2 - h_hi.astype(F32)).astype(BF16)
    both = lax.dot_general(jnp.concatenate([r_hi, r_lo], axis=0), h_hi, nt, preferred_element_type=F32)
    logits = (both[0:ne] + both[ne:2 * ne]
              + lax.dot_general(r_hi, h_lo, nt, preferred_element_type=F32))
    eid = lax.broadcasted_iota(jnp.int32, logits.shape, 0)
    m1 = jnp.max(logits, axis=0, keepdims=True)
    i1 = jnp.min(jnp.where(logits == m1, eid, ne), axis=0, keepdims=True)
    rest = jnp.where(eid == i1, -jnp.inf, logits)
    m2 = jnp.max(rest, axis=0, keepdims=True)
    i2 = jnp.min(jnp.where(rest == m2, eid, ne), axis=0, keepdims=True)
    e2 = jnp.exp(m2 - m1)
    den = 1.0 + e2
    idx_ref[0:1, :] = i1
    idx_ref[1:2, :] = i2
    gate_ref[0:1, :] = 1.0 / den
    gate_ref[1:2, :] = e2 / den


def _diff_attention_and_router(x, positions, g_mix, w_in, q_norm, k_norm, lam_params, subln, w_out,
                               g_ffn, router, lambda_init, batch, seq):
    t, d = x.shape
    tm = TOKEN_TILE
    nt = t // tm
    inv_freq = ROPE_THETA ** (-jnp.arange(0, ROT_DIM, 2, dtype=F32) / ROT_DIM)
    q, k, v = pl.pallas_call(
        _qkv_kernel,
        grid=(nt,),
        in_specs=[
            pl.BlockSpec((tm, d), lambda i: (i, 0)),
            _resident((1, d)),
            _resident((3 * d, d)),
            pl.BlockSpec((1, tm), lambda i: (0, i)),
            _resident((ROT_DIM // 2, 1)),
            _resident((HEAD_DIM, 1)),
            _resident((HEAD_DIM, 1)),
        ],
        out_specs=[
            pl.BlockSpec((1, d, tm), lambda i: (i, 0, 0)),
            pl.BlockSpec((tm, d), lambda i: (i, 0)),
            pl.BlockSpec((1, d, tm), lambda i: (i, 0, 0)),
        ],
        out_shape=[
            jax.ShapeDtypeStruct((nt, d, tm), BF16),
            jax.ShapeDtypeStruct((t, d), BF16),
            jax.ShapeDtypeStruct((nt, d, tm), BF16),
        ],
        compiler_params=_cparams("parallel"),
        name="attn_qkv",
    )(x, g_mix, w_in.T, positions.reshape(1, t), inv_freq.reshape(-1, 1),
      q_norm.reshape(-1, 1), k_norm.reshape(-1, 1))

    nq = seq // tm
    score_bound = (HEAD_DIM ** 0.5 * LOG2E * _BF16_ROUNDING_MARGIN
                   * jnp.max(jnp.abs(q_norm)) * jnp.max(jnp.abs(k_norm)))
    unshifted = (score_bound <= _EXP2_SAFE_LOG2).astype(jnp.int32).reshape(1)
    o = pl.pallas_call(
        functools.partial(_attn_kernel, lambda_init),
        grid_spec=pltpu.PrefetchScalarGridSpec(
            num_scalar_prefetch=1,
            grid=(batch, N_HEADS, nq),
            in_specs=[
                pl.BlockSpec((1, V_DIM, tm), lambda b, h, i, f: (b * nq + i, h, 0)),
                pl.BlockSpec((seq, V_DIM), lambda b, h, i, f: (b, h)),
                pl.BlockSpec((nq, V_DIM, tm), lambda b, h, i, f: (b, h, 0)),
                pl.BlockSpec((4, HEAD_DIM), lambda b, h, i, f: (0, 0)),
                pl.BlockSpec((V_DIM, 1), lambda b, h, i, f: (0, 0)),
            ],
            out_specs=pl.BlockSpec((1, V_DIM, tm), lambda b, h, i, f: (b * nq + i, h, 0)),
            scratch_shapes=[pltpu.VMEM((V_DIM, tm), F32), pltpu.VMEM((V_DIM, tm), F32),
                            pltpu.VMEM((2, 2, tm, tm), BF16)],
        ),
        out_shape=jax.ShapeDtypeStruct((nt, d, tm), BF16),
        compiler_params=_cparams("parallel", "parallel", "arbitrary"),
        name="diff_attn",
    )(unshifted, q, k, v, lam_params, subln.reshape(-1, 1))

    x2, h2, idx, gates = pl.pallas_call(
        _attn_out_router_kernel,
        grid=(nt,),
        in_specs=[
            pl.BlockSpec((1, d, tm), lambda i: (i, 0, 0)),
            pl.BlockSpec((tm, d), lambda i: (i, 0)),
            _resident((d, d)),
            _resident((1, d)),
            _resident((N_EXPERTS, d)),
        ],
        out_specs=[
            pl.BlockSpec((tm, d), lambda i: (i, 0)),
            pl.BlockSpec((tm * ROW_SUBLANES, LANES), lambda i: (i, 0)),
            pl.BlockSpec((2, tm), lambda i: (0, i)),
            pl.BlockSpec((2, tm), lambda i: (0, i)),
        ],
        out_shape=[
            jax.ShapeDtypeStruct((t, d), F32),
            jax.ShapeDtypeStruct((t * ROW_SUBLANES, LANES), F32),
            jax.ShapeDtypeStruct((2, t), jnp.int32),
            jax.ShapeDtypeStruct((2, t), F32),
        ],
        compiler_params=_cparams("parallel"),
        name="attn_out_router",
    )(o, x, w_out, g_ffn, router.T)
    return x2, h2, idx.T, gates.T


def _routing_plan(idx, n_tiles):
    hot = (jax.nn.one_hot(idx[:, 0], N_EXPERTS, dtype=jnp.int32)
           + jax.nn.one_hot(idx[:, 1], N_EXPERTS, dtype=jnp.int32))
    incl = jnp.cumsum(hot, axis=0)
    rank = incl - hot
    tiles = (incl[-1] + MOE_TILE - 1) // MOE_TILE
    tile_end = jnp.cumsum(tiles)
    row_start = (tile_end - tiles) * MOE_TILE
    slot = row_start[None, :] + rank
    pos = jnp.take_along_axis(slot, idx, axis=1).T
    tile_id = jnp.arange(n_tiles, dtype=jnp.int32)
    tile_expert = jnp.minimum(jnp.sum(tile_id[:, None] >= tile_end[None, :], axis=1), N_EXPERTS - 1)
    tile_active = (tile_id < tile_end[-1]).astype(jnp.int32)
    last_tile = jnp.where(tiles > 0, tile_end - 1, -1)
    tail = tile_end[-1] + jnp.arange(N_EXPERTS, dtype=jnp.int32)
    pad_tiles = jnp.concatenate([last_tile, jnp.where(tail < n_tiles, tail, -1)])
    return (pos.astype(jnp.int32), tile_expert.astype(jnp.int32), tile_active,
            pad_tiles.astype(jnp.int32))


def _row_tile(ref, row):
    return ref.at[pl.ds(pl.multiple_of(row * ROW_SUBLANES, ROW_SUBLANES), ROW_SUBLANES)]


def _dispatch_kernel(pad_ref, pos_ref, h_hbm, xs_hbm, zero_ref, ring, load_sem, row_sem):
    i = pl.program_id(0)
    last = pl.num_programs(0) - 1
    n = pos_ref.shape[2] // 2
    mt = zero_ref.shape[0]
    nslot = ring.shape[0]
    block_rows = n * ROW_SUBLANES

    def load(step):
        src = h_hbm.at[pl.ds(pl.multiple_of(step * block_rows, block_rows), block_rows)]
        return pltpu.make_async_copy(src, ring.at[step % nslot], load_sem.at[step % nslot])

    def wait_rows(step):
        for _ in (0, 1):
            pltpu.make_async_copy(ring.at[0], xs_hbm.at[pl.ds(0, block_rows)],
                                  row_sem.at[step % nslot]).wait()

    @pl.when(i == 0)
    def _():
        load(0).start()
        zero_ref[...] = jnp.zeros_like(zero_ref)
        fills = [pltpu.make_async_copy(
            zero_ref, xs_hbm.at[pl.ds(pl.multiple_of(jnp.maximum(pad_ref[j], 0) * mt, mt), mt)],
            row_sem.at[0]) for j in range(pad_ref.shape[0])]
        for j, fill in enumerate(fills):
            pl.when(pad_ref[j] >= 0)(fill.start)
        for j, fill in enumerate(fills):
            pl.when(pad_ref[j] >= 0)(fill.wait)

    pl.when(i < last)(lambda: load(i + 1).start())
    load(i).wait()
    block = ring.at[i % nslot]

    def issue(r, carry):
        for c in (0, 1):
            pltpu.make_async_copy(_row_tile(block, r), _row_tile(xs_hbm, pos_ref[0, 0, c * n + r]),
                                  row_sem.at[i % nslot]).start(priority=c)
        return carry

    lax.fori_loop(0, n, issue, 0, unroll=8)
    pl.when(i > 0)(lambda: wait_rows(i - 1))
    pl.when(i == last)(lambda: wait_rows(i))


def _expert_kernel(te_ref, ta_ref, xs_ref, w1_ref, w3_ref, w2_ref, y_ref):
    i = pl.program_id(0)
    mt = xs_ref.shape[0] // ROW_SUBLANES

    @pl.when(ta_ref[i] == 1)
    def _():
        x = _load_row_tiles(xs_ref, 0, mt)
        acc = None
        for f0, fl in _ff_chunks(w1_ref.shape[1]):
            a = _dot(x, w1_ref[:, f0:f0 + fl])
            b = _dot(x, w3_ref[:, f0:f0 + fl])
            act = _silu(a) * b
            part = _dot(act, w2_ref[f0:f0 + fl, :])
            acc = part if acc is None else acc + part
        _store_row_tiles(y_ref, acc)

    @pl.when(ta_ref[i] == 0)
    def _():
        y_ref[...] = jnp.zeros_like(y_ref)


def _combine_kernel(pos_ref, pos_next_ref, x_ref, g_ref, y_hbm, o_ref, buf, sem):
    i = pl.program_id(0)
    n = x_ref.shape[0]

    def gather(p_ref, slot):
        def issue(r, carry):
            for c in (0, 1):
                pltpu.make_async_copy(_row_tile(y_hbm, p_ref[0, 0, c * n + r]),
                                      _row_tile(buf, (slot * 2 + c) * n + r),
                                      sem.at[slot]).start(priority=c)
            return carry
        lax.fori_loop(0, n, issue, 0, unroll=8)

    @pl.when(i == 0)
    def _():
        gather(pos_ref, 0)

    @pl.when(i + 1 < pl.num_programs(0))
    def _():
        gather(pos_next_ref, (i + 1) % 2)

    slot = i % 2
    chunk = n * ROW_SUBLANES
    base = pl.multiple_of(slot * 2 * chunk, chunk)
    pltpu.make_async_copy(y_hbm.at[pl.ds(0, 2 * chunk)], buf.at[pl.ds(base, 2 * chunk)],
                          sem.at[slot]).wait()
    g = g_ref[...]
    o_ref[...] = (x_ref[...] + g[:, 0:1] * _load_row_tiles(buf, base, n)
                  + g[:, 1:2] * _load_row_tiles(buf, base + chunk, n))


def _moe(x2, h2, idx, gates, w1, w3, w2):
    t, d = x2.shape
    fe = w1.shape[2]
    assert d == ROW_SUBLANES * LANES
    rs = ROW_SUBLANES
    n_tiles = (2 * t) // MOE_TILE + N_EXPERTS
    rows = n_tiles * MOE_TILE
    pos, tile_expert, tile_active, pad_tiles = _routing_plan(idx, n_tiles)
    gt = GATHER_TILE
    ng = t // gt
    pos_blocks = pos.reshape(2, ng, gt).transpose(1, 0, 2).reshape(ng, 1, 2 * gt)

    xs = pl.pallas_call(
        _dispatch_kernel,
        grid_spec=pltpu.PrefetchScalarGridSpec(
            num_scalar_prefetch=1,
            grid=(ng,),
            in_specs=[
                pl.BlockSpec((1, 1, 2 * gt), lambda i, pad: (i, 0, 0), memory_space=pltpu.SMEM),
                pl.BlockSpec(memory_space=pl.ANY),
            ],
            out_specs=pl.BlockSpec(memory_space=pl.ANY),
            scratch_shapes=[pltpu.VMEM((MOE_TILE * rs, LANES), F32),
                            pltpu.VMEM((DISPATCH_RING, gt * rs, LANES), F32),
                            pltpu.SemaphoreType.DMA((DISPATCH_RING,)),
                            pltpu.SemaphoreType.DMA((DISPATCH_RING,))],
        ),
        out_shape=jax.ShapeDtypeStruct((rows * rs, LANES), F32),
        compiler_params=_cparams("arbitrary"),
        name="moe_dispatch",
    )(pad_tiles, pos_blocks, h2)

    y = pl.pallas_call(
        _expert_kernel,
        grid_spec=pltpu.PrefetchScalarGridSpec(
            num_scalar_prefetch=2,
            grid=(n_tiles,),
            in_specs=[
                pl.BlockSpec((MOE_TILE * rs, LANES), lambda i, te, ta: (i, 0)),
                pl.BlockSpec((None, d, fe), lambda i, te, ta: (te[i], 0, 0),
                             pipeline_mode=pl.Buffered(1)),
                pl.BlockSpec((None, d, fe), lambda i, te, ta: (te[i], 0, 0),
                             pipeline_mode=pl.Buffered(1)),
                pl.BlockSpec((None, fe, d), lambda i, te, ta: (te[i], 0, 0),
                             pipeline_mode=pl.Buffered(1)),
            ],
            out_specs=pl.BlockSpec((MOE_TILE * rs, LANES), lambda i, te, ta: (i, 0)),
        ),
        out_shape=jax.ShapeDtypeStruct((rows * rs, LANES), F32),
        compiler_params=_cparams("arbitrary"),
        name="moe_experts",
    )(tile_expert, tile_active, xs, w1, w3, w2)

    return pl.pallas_call(
        _combine_kernel,
        grid=(ng,),
        in_specs=[
            pl.BlockSpec((1, 1, 2 * gt), lambda i: (i, 0, 0), memory_space=pltpu.SMEM),
            pl.BlockSpec((1, 1, 2 * gt), lambda i: (jnp.minimum(i + 1, ng - 1), 0, 0),
                         memory_space=pltpu.SMEM),
            pl.BlockSpec((gt, d), lambda i: (i, 0)),
            pl.BlockSpec((gt, 2), lambda i: (i, 0)),
            pl.BlockSpec(memory_space=pl.ANY),
        ],
        out_specs=pl.BlockSpec((gt, d), lambda i: (i, 0)),
        out_shape=jax.ShapeDtypeStruct((t, d), F32),
        scratch_shapes=[pltpu.VMEM((2 * 2 * gt * rs, LANES), F32), pltpu.SemaphoreType.DMA((2,))],
        compiler_params=_cparams("arbitrary"),
        name="moe_combine",
    )(pos_blocks, pos_blocks, x2, gates, y)


def kernel(x, positions, norm_mix, norm_ffn, conv_in, conv_w, conv_out, attn_in, q_norm, k_norm,
           lam_q1, lam_k1, lam_q2, lam_k2, subln, attn_out, ffn_w1, ffn_w3, ffn_w2,
           router, moe_w1, moe_w3, moe_w2):
    batch, seq, d = x.shape
    xt = x.reshape(batch * seq, d)
    bf = lambda w: w.astype(BF16)

    xt = _layer0(xt, norm_mix[0:1], bf(conv_in[0]), conv_w[0], bf(conv_out[0]),
                 norm_ffn[0:1], bf(ffn_w1[0]), bf(ffn_w3[0]), bf(ffn_w2[0]), seq)

    lambda_init = 0.8 - 0.6 * math.exp(-0.3 * 1)
    lam_params = jnp.concatenate([lam_q1, lam_k1, lam_q2, lam_k2], axis=0)
    x2, h2, idx, gates = _diff_attention_and_router(
        xt, positions, norm_mix[1:2], bf(attn_in[0]), q_norm[0], k_norm[0], lam_params, subln[0],
        bf(attn_out[0]), norm_ffn[1:2], router[0], lambda_init, batch, seq)
    out = _moe(x2, h2, idx, gates, moe_w1[0], moe_w3[0], moe_w2[0])
    return out.reshape(batch, seq, d)
```

```python
import functools
import math

import jax
import jax.numpy as jnp
from jax import lax
from jax.experimental import pallas as pl
from jax.experimental.pallas import tpu as pltpu

F32 = jnp.float32
BF16 = jnp.bfloat16

D_MODEL = 1024
N_HEADS = 8
HEAD_DIM = 64
V_DIM = 2 * HEAD_DIM
ROT_DIM = HEAD_DIM // 4
ROPE_THETA = 500000.0
N_EXPERTS = 8
RMS_EPS = 1e-6
LOG2E = 1.4426950408889634
LANES = 128
ROW_SUBLANES = 8

TOKEN_TILE = 512
ATTN_KV_TILE = 512
ATTN_Q_TILES = 2
MOE_TILE = 512
FF_CHUNK = 512
GATHER_TILE = 256
DISPATCH_RING = 3
VMEM_LIMIT = 56 * 1024 * 1024

_NEG_BIG = -1e30
_EXP2_SAFE_LOG2 = 60.0
_BF16_ROUNDING_MARGIN = 1.02


def _cparams(*sem):
    return pltpu.CompilerParams(dimension_semantics=sem, vmem_limit_bytes=VMEM_LIMIT)


def _rms_rows(x, g):
    ms = jnp.mean(x * x, axis=-1, keepdims=True)
    return x * lax.rsqrt(ms + RMS_EPS) * g


def _dot(a, b):
    return jnp.dot(a, b, preferred_element_type=F32)


def _silu(a):
    return a * (1.0 / (1.0 + jnp.exp(-a)))


def _ff_chunks(width):
    out, f0 = [], 0
    while f0 < width:
        fl = min(FF_CHUNK, width - f0)
        out.append((f0, fl))
        f0 += fl
    return out


def _conv_in_kernel(x_ref, g_ref, w_ref, b_ref, u_ref):
    d = x_ref.shape[1]
    h = _rms_rows(x_ref[...], g_ref[...]).astype(BF16)
    b_ref[...] = _dot(h, w_ref[:, 0:d]).astype(BF16)
    c = _dot(h, w_ref[:, d:2 * d])
    v = _dot(h, w_ref[:, 2 * d:3 * d])
    u_ref[...] = (c * v).astype(BF16)


def _conv_out_ffn_kernel(tiles_per_seq, u_ref, up_ref, un_ref, b_ref, x_ref, cw_ref, w_ref,
                         g_ref, w1_ref, w3_ref, w2_ref, o_ref):
    i = pl.program_id(0)
    tm = u_ref.shape[0]
    halo = up_ref.shape[0]
    u = u_ref[...].astype(F32)
    first = (i % tiles_per_seq) == 0
    last = (i % tiles_per_seq) == tiles_per_seq - 1
    prev_row = jnp.where(first, 0.0, up_ref[halo - 1:halo, :].astype(F32))
    next_row = jnp.where(last, 0.0, un_ref[0:1, :].astype(F32))
    rows = lax.broadcasted_iota(jnp.int32, (tm, 1), 0)
    u_m1 = jnp.where(rows == 0, prev_row, pltpu.roll(u, 1, 0))
    u_p1 = jnp.where(rows == tm - 1, next_row, pltpu.roll(u, tm - 1, 0))
    cw = cw_ref[...]
    conv = u_m1 * cw[0:1, :] + u * cw[1:2, :] + u_p1 * cw[2:3, :]
    y = (b_ref[...].astype(F32) * conv).astype(BF16)
    x1 = x_ref[...] + _dot(y, w_ref[...])
    h = _rms_rows(x1, g_ref[...]).astype(BF16)
    acc = x1
    for f0, fl in _ff_chunks(w1_ref.shape[1]):
        a = _dot(h, w1_ref[:, f0:f0 + fl])
        b = _dot(h, w3_ref[:, f0:f0 + fl])
        act = (_silu(a) * b).astype(BF16)
        acc = acc + _dot(act, w2_ref[f0:f0 + fl, :])
    o_ref[...] = acc


def _resident(shape):
    return pl.BlockSpec(shape, lambda i: (0,) * len(shape), pipeline_mode=pl.Buffered(1))


def _layer0(x, g_mix, w_in, conv_w, w_out, g_ffn, w1, w3, w2, seq):
    t, d = x.shape
    f = w1.shape[1]
    tm = TOKEN_TILE
    nt = t // tm
    halo = 16
    b, u = pl.pallas_call(
        _conv_in_kernel,
        grid=(nt,),
        in_specs=[
            pl.BlockSpec((tm, d), lambda i: (i, 0)),
            _resident((1, d)),
            _resident((d, 3 * d)),
        ],
        out_specs=[pl.BlockSpec((tm, d), lambda i: (i, 0))] * 2,
        out_shape=[jax.ShapeDtypeStruct((t, d), BF16)] * 2,
        compiler_params=_cparams("parallel"),
        name="conv_in",
    )(x, g_mix, w_in)
    hb = tm // halo
    nhb = t // halo
    return pl.pallas_call(
        functools.partial(_conv_out_ffn_kernel, seq // tm),
        grid=(nt,),
        in_specs=[
            pl.BlockSpec((tm, d), lambda i: (i, 0)),
            pl.BlockSpec((halo, d), lambda i: (jnp.maximum(i * hb - 1, 0), 0)),
            pl.BlockSpec((halo, d), lambda i: (jnp.minimum((i + 1) * hb, nhb - 1), 0)),
            pl.BlockSpec((tm, d), lambda i: (i, 0)),
            pl.BlockSpec((tm, d), lambda i: (i, 0)),
            _resident((3, d)),
            _resident((d, d)),
            _resident((1, d)),
            _resident((d, f)),
            _resident((d, f)),
            _resident((f, d)),
        ],
        out_specs=pl.BlockSpec((tm, d), lambda i: (i, 0)),
        out_shape=jax.ShapeDtypeStruct((t, d), F32),
        compiler_params=_cparams("parallel"),
        name="conv_out_ffn",
    )(u, u, u, b, x, conv_w, w_out, g_ffn, w1, w3, w2)


def _qkv_kernel(x_ref, g_ref, wt_ref, pos_ref, freq_ref, qn_ref, kn_ref, q_ref, k_ref, v_ref):
    tm, d = x_ref.shape
    h = _rms_rows(x_ref[...], g_ref[...]).astype(BF16)
    nt = (((1,), (1,)), ((), ()))
    ang = pos_ref[...].astype(F32) * freq_ref[...]
    cos = jnp.cos(ang)[None]
    sin = jnp.sin(ang)[None]
    half = ROT_DIM // 2
    groups = d // HEAD_DIM

    def norm_rope(w_rows, gain, scale):
        t = lax.dot_general(w_rows, h, nt, preferred_element_type=F32)
        t = t.reshape(groups, HEAD_DIM, tm)
        ms = jnp.mean(t * t, axis=1, keepdims=True)
        t = t * lax.rsqrt(ms + RMS_EPS) * gain[None]
        t1 = t[:, 0:half, :]
        t2 = t[:, half:ROT_DIM, :]
        t = jnp.concatenate([t1 * cos - t2 * sin, t2 * cos + t1 * sin, t[:, ROT_DIM:, :]], axis=1)
        return (t * scale).reshape(d, tm)

    q = norm_rope(wt_ref[0:d, :], qn_ref[...], (HEAD_DIM ** -0.5) * LOG2E)
    q_ref[0] = q.astype(BF16)
    k = norm_rope(wt_ref[d:2 * d, :], kn_ref[...], 1.0)
    k_ref[...] = k.T.astype(BF16)
    v = lax.dot_general(wt_ref[2 * d:3 * d, :], h, nt, preferred_element_type=F32)
    v_ref[0] = v.astype(BF16)


def _attn_kernel(lambda_init, unshifted_ref, q_ref, k_ref, v_ref, lam_ref, sub_ref, o_ref,
                 acc_ref, p_ref):
    nqt, _, tq = q_ref.shape
    nkb = v_ref.shape[0]
    tk = v_ref.shape[2]
    rows = lax.broadcasted_iota(jnp.int32, (V_DIM, 1), 0)
    acc_ref[...] = jnp.zeros_like(acc_ref)
    zer = jnp.zeros((1, tq), F32)

    def split_q(t):
        q = q_ref[t]
        zero = jnp.zeros_like(q)
        return jnp.where(rows < HEAD_DIM, q, zero), jnp.where(rows >= HEAD_DIM, q, zero)

    def load_kv(kb):
        k = k_ref[pl.ds(pl.multiple_of(kb * tk, tk), tk), :]
        return k, v_ref[kb]

    def finish(t, l1, l2):
        lp = lam_ref[...]
        lam = (jnp.exp(jnp.sum(lp[0:1, :] * lp[1:2, :], axis=-1, keepdims=True))
               - jnp.exp(jnp.sum(lp[2:3, :] * lp[3:4, :], axis=-1, keepdims=True))
               + lambda_init)
        o = acc_ref[t, 0] / l1 - lam * (acc_ref[t, 1] / l2)
        ms = jnp.mean(o * o, axis=0, keepdims=True)
        o = o * lax.rsqrt(ms + RMS_EPS) * sub_ref[...] * (1.0 - lambda_init)
        o_ref[t] = o.astype(BF16)

    @pl.when(unshifted_ref[0] == 1)
    def _():
        def scores(qs, kb, l, slot):
            k = load_kv(kb)[0]
            ps = [jnp.exp2(_dot(k, qc)) for qc in qs]
            for c, p in enumerate(ps):
                p_ref[slot, c] = p.astype(BF16)
            return [lc + jnp.sum(p, axis=0, keepdims=True) for lc, p in zip(l, ps)]

        def values(t, kb, slot):
            v = v_ref[kb]
            for c in (0, 1):
                acc_ref[t, c] += _dot(v, p_ref[slot, c])

        stage = 0
        pending = None
        for t in range(nqt):
            qs = split_q(t)
            l = [zer, zer]
            for kb in range(nkb):
                l = scores(qs, kb, l, stage % 2)
                if pending is not None:
                    values(*pending)
                    if pending[1] == nkb - 1:
                        finish(pending[0], *l_done)
                pending = (t, kb, stage % 2)
                stage += 1
            l_done = l
        values(*pending)
        finish(pending[0], *l_done)

    @pl.when(unshifted_ref[0] == 0)
    def _():
        def one_map(k, v, qc, m, l, t, c):
            s = _dot(k, qc)
            m_new = jnp.maximum(m, jnp.max(s, axis=0, keepdims=True))
            alpha = jnp.exp2(m - m_new)
            p = jnp.exp2(s - m_new)
            acc_ref[t, c] = alpha * acc_ref[t, c] + _dot(v, p.astype(BF16))
            return m_new, alpha * l + jnp.sum(p, axis=0, keepdims=True)

        neg = jnp.full((1, tq), _NEG_BIG, F32)
        for t in range(nqt):
            q1, q2 = split_q(t)

            def body(kb, carry):
                k, v = load_kv(kb)
                m1, l1 = one_map(k, v, q1, carry[0], carry[1], t, 0)
                m2, l2 = one_map(k, v, q2, carry[2], carry[3], t, 1)
                return m1, l1, m2, l2

            _, l1, _, l2 = lax.fori_loop(0, nkb, body, (neg, zer, neg, zer))
            finish(t, l1, l2)


def _store_row_tiles(ref, x):
    rows = x.shape[0]
    for j in range(ROW_SUBLANES):
        ref[pl.ds(j, rows, stride=ROW_SUBLANES), :] = x[:, j * LANES:(j + 1) * LANES]


def _load_row_tiles(ref, base, rows):
    return jnp.concatenate(
        [ref[pl.ds(base + j, rows, stride=ROW_SUBLANES), :] for j in range(ROW_SUBLANES)], axis=1)


def _attn_out_router_kernel(o_ref, x_ref, w_ref, g_ref, rt_ref, x2_ref, h2_ref, idx_ref, gate_ref):
    tn = (((0,), (0,)), ((), ()))
    y = lax.dot_general(o_ref[0], w_ref[...], tn, preferred_element_type=F32)
    x2 = x_ref[...] + y
    x2_ref[...] = x2
    h2 = _rms_rows(x2, g_ref[...])
    _store_row_tiles(h2_ref, h2)
    ne = rt_ref.shape[0]
    nt = (((1,), (1,)), ((), ()))
    r = rt_ref[...]
    r_hi = r.astype(BF16)
    r_lo = (r - r_hi.astype(F32)).astype(BF16)
    h_hi = h2.astype(BF16)
    h_lo = (h2 - h_hi.astype(F32)).astype(BF16)
    both = lax.dot_general(jnp.concatenate([r_hi, r_lo], axis=0), h_hi, nt, preferred_element_type=F32)
    logits = (both[0:ne] + both[ne:2 * ne]
              + lax.dot_general(r_hi, h_lo, nt, preferred_element_type=F32))
    eid = lax.broadcasted_iota(jnp.int32, logits.shape, 0)
    m1 = jnp.max(logits, axis=0, keepdims=True)
    i1 = jnp.min(jnp.where(logits == m1, eid, ne), axis=0, keepdims=True)
    rest = jnp.where(eid == i1, -jnp.inf, logits)
    m2 = jnp.max(rest, axis=0, keepdims=True)
    i2 = jnp.min(jnp.where(rest == m2, eid, ne), axis=0, keepdims=True)
    e2 = jnp.exp(m2 - m1)
    den = 1.0 + e2
    idx_ref[0:1, :] = i1
    idx_ref[1:2, :] = i2
    gate_ref[0:1, :] = 1.0 / den
    gate_ref[1:2, :] = e2 / den


def _diff_attention_and_router(x, positions, g_mix, w_in, q_norm, k_norm, lam_params, subln, w_out,
                               g_ffn, router, lambda_init, batch, seq):
    t, d = x.shape
    tm = TOKEN_TILE
    nt = t // tm
    inv_freq = ROPE_THETA ** (-jnp.arange(0, ROT_DIM, 2, dtype=F32) / ROT_DIM)
    q, k, v = pl.pallas_call(
        _qkv_kernel,
        grid=(nt,),
        in_specs=[
            pl.BlockSpec((tm, d), lambda i: (i, 0)),
            _resident((1, d)),
            _resident((3 * d, d)),
            pl.BlockSpec((1, tm), lambda i: (0, i)),
            _resident((ROT_DIM // 2, 1)),
            _resident((HEAD_DIM, 1)),
            _resident((HEAD_DIM, 1)),
        ],
        out_specs=[
            pl.BlockSpec((1, d, tm), lambda i: (i, 0, 0)),
            pl.BlockSpec((tm, d), lambda i: (i, 0)),
            pl.BlockSpec((1, d, tm), lambda i: (i, 0, 0)),
        ],
        out_shape=[
            jax.ShapeDtypeStruct((nt, d, tm), BF16),
            jax.ShapeDtypeStruct((t, d), BF16),
            jax.ShapeDtypeStruct((nt, d, tm), BF16),
        ],
        compiler_params=_cparams("parallel"),
        name="attn_qkv",
    )(x, g_mix, w_in.T, positions.reshape(1, t), inv_freq.reshape(-1, 1),
      q_norm.reshape(-1, 1), k_norm.reshape(-1, 1))

    nq = seq // tm
    qg = ATTN_Q_TILES
    ngrp = nq // qg
    score_bound = (HEAD_DIM ** 0.5 * LOG2E * _BF16_ROUNDING_MARGIN
                   * jnp.max(jnp.abs(q_norm)) * jnp.max(jnp.abs(k_norm)))
    unshifted = (score_bound <= _EXP2_SAFE_LOG2).astype(jnp.int32).reshape(1)
    o = pl.pallas_call(
        functools.partial(_attn_kernel, lambda_init),
        grid_spec=pltpu.PrefetchScalarGridSpec(
            num_scalar_prefetch=1,
            grid=(batch, N_HEADS, ngrp),
            in_specs=[
                pl.BlockSpec((qg, V_DIM, tm), lambda b, h, i, f: (b * ngrp + i, h, 0)),
                pl.BlockSpec((seq, V_DIM), lambda b, h, i, f: (b, h)),
                pl.BlockSpec((nq, V_DIM, tm), lambda b, h, i, f: (b, h, 0)),
                pl.BlockSpec((4, HEAD_DIM), lambda b, h, i, f: (0, 0)),
                pl.BlockSpec((V_DIM, 1), lambda b, h, i, f: (0, 0)),
            ],
            out_specs=pl.BlockSpec((qg, V_DIM, tm), lambda b, h, i, f: (b * ngrp + i, h, 0)),
            scratch_shapes=[pltpu.VMEM((qg, 2, V_DIM, tm), F32),
                            pltpu.VMEM((2, 2, tm, tm), BF16)],
        ),
        out_shape=jax.ShapeDtypeStruct((nt, d, tm), BF16),
        compiler_params=_cparams("parallel", "parallel", "arbitrary"),
        name="diff_attn",
    )(unshifted, q, k, v, lam_params, subln.reshape(-1, 1))

    x2, h2, idx, gates = pl.pallas_call(
        _attn_out_router_kernel,
        grid=(nt,),
        in_specs=[
            pl.BlockSpec((1, d, tm), lambda i: (i, 0, 0)),
            pl.BlockSpec((tm, d), lambda i: (i, 0)),
            _resident((d, d)),
            _resident((1, d)),
            _resident((N_EXPERTS, d)),
        ],
        out_specs=[
            pl.BlockSpec((tm, d), lambda i: (i, 0)),
            pl.BlockSpec((tm * ROW_SUBLANES, LANES), lambda i: (i, 0)),
            pl.BlockSpec((2, tm), lambda i: (0, i)),
            pl.BlockSpec((2, tm), lambda i: (0, i)),
        ],
        out_shape=[
            jax.ShapeDtypeStruct((t, d), F32),
            jax.ShapeDtypeStruct((t * ROW_SUBLANES, LANES), F32),
            jax.ShapeDtypeStruct((2, t), jnp.int32),
            jax.ShapeDtypeStruct((2, t), F32),
        ],
        compiler_params=_cparams("parallel"),
        name="attn_out_router",
    )(o, x, w_out, g_ffn, router.T)
    return x2, h2, idx.T, gates.T


def _routing_plan(idx, n_tiles):
    hot = (jax.nn.one_hot(idx[:, 0], N_EXPERTS, dtype=jnp.int32)
           + jax.nn.one_hot(idx[:, 1], N_EXPERTS, dtype=jnp.int32))
    incl = jnp.cumsum(hot, axis=0)
    rank = incl - hot
    tiles = (incl[-1] + MOE_TILE - 1) // MOE_TILE
    tile_end = jnp.cumsum(tiles)
    row_start = (tile_end - tiles) * MOE_TILE
    slot = row_start[None, :] + rank
    pos = jnp.take_along_axis(slot, idx, axis=1).T
    tile_id = jnp.arange(n_tiles, dtype=jnp.int32)
    tile_expert = jnp.minimum(jnp.sum(tile_id[:, None] >= tile_end[None, :], axis=1), N_EXPERTS - 1)
    tile_active = (tile_id < tile_end[-1]).astype(jnp.int32)
    last_tile = jnp.where(tiles > 0, tile_end - 1, -1)
    tail = tile_end[-1] + jnp.arange(N_EXPERTS, dtype=jnp.int32)
    pad_tiles = jnp.concatenate([last_tile, jnp.where(tail < n_tiles, tail, -1)])
    return (pos.astype(jnp.int32), tile_expert.astype(jnp.int32), tile_active,
            pad_tiles.astype(jnp.int32))


def _row_tile(ref, row):
    return ref.at[pl.ds(pl.multiple_of(row * ROW_SUBLANES, ROW_SUBLANES), ROW_SUBLANES)]


def _dispatch_kernel(pad_ref, pos_ref, h_hbm, xs_hbm, zero_ref, ring, load_sem, row_sem):
    i = pl.program_id(0)
    last = pl.num_programs(0) - 1
    n = pos_ref.shape[2] // 2
    mt = zero_ref.shape[0]
    nslot = ring.shape[0]
    block_rows = n * ROW_SUBLANES

    def load(step):
        src = h_hbm.at[pl.ds(pl.multiple_of(step * block_rows, block_rows), block_rows)]
        return pltpu.make_async_copy(src, ring.at[step % nslot], load_sem.at[step % nslot])

    def wait_rows(step):
        for _ in (0, 1):
            pltpu.make_async_copy(ring.at[0], xs_hbm.at[pl.ds(0, block_rows)],
                                  row_sem.at[step % nslot]).wait()

    @pl.when(i == 0)
    def _():
        load(0).start()
        zero_ref[...] = jnp.zeros_like(zero_ref)
        fills = [pltpu.make_async_copy(
            zero_ref, xs_hbm.at[pl.ds(pl.multiple_of(jnp.maximum(pad_ref[j], 0) * mt, mt), mt)],
            row_sem.at[0]) for j in range(pad_ref.shape[0])]
        for j, fill in enumerate(fills):
            pl.when(pad_ref[j] >= 0)(fill.start)
        for j, fill in enumerate(fills):
            pl.when(pad_ref[j] >= 0)(fill.wait)

    pl.when(i < last)(lambda: load(i + 1).start())
    load(i).wait()
    block = ring.at[i % nslot]

    def issue(r, carry):
        for c in (0, 1):
            pltpu.make_async_copy(_row_tile(block, r), _row_tile(xs_hbm, pos_ref[0, 0, c * n + r]),
                                  row_sem.at[i % nslot]).start(priority=c)
        return carry

    lax.fori_loop(0, n, issue, 0, unroll=8)
    pl.when(i > 0)(lambda: wait_rows(i - 1))
    pl.when(i == last)(lambda: wait_rows(i))


def _expert_kernel(te_ref, ta_ref, xs_ref, w1_ref, w3_ref, w2_ref, y_ref):
    i = pl.program_id(0)
    mt = xs_ref.shape[0] // ROW_SUBLANES

    @pl.when(ta_ref[i] == 1)
    def _():
        x = _load_row_tiles(xs_ref, 0, mt)
        acc = None
        for f0, fl in _ff_chunks(w1_ref.shape[1]):
            a = _dot(x, w1_ref[:, f0:f0 + fl])
            b = _dot(x, w3_ref[:, f0:f0 + fl])
            act = _silu(a) * b
            part = _dot(act, w2_ref[f0:f0 + fl, :])
            acc = part if acc is None else acc + part
        _store_row_tiles(y_ref, acc)

    @pl.when(ta_ref[i] == 0)
    def _():
        y_ref[...] = jnp.zeros_like(y_ref)


def _combine_kernel(pos_ref, pos_next_ref, x_ref, g_ref, y_hbm, o_ref, buf, sem):
    i = pl.program_id(0)
    n = x_ref.shape[0]

    def gather(p_ref, slot):
        def issue(r, carry):
            for c in (0, 1):
                pltpu.make_async_copy(_row_tile(y_hbm, p_ref[0, 0, c * n + r]),
                                      _row_tile(buf, (slot * 2 + c) * n + r),
                                      sem.at[slot]).start(priority=c)
            return carry
        lax.fori_loop(0, n, issue, 0, unroll=8)

    @pl.when(i == 0)
    def _():
        gather(pos_ref, 0)

    @pl.when(i + 1 < pl.num_programs(0))
    def _():
        gather(pos_next_ref, (i + 1) % 2)

    slot = i % 2
    chunk = n * ROW_SUBLANES
    base = pl.multiple_of(slot * 2 * chunk, chunk)
    pltpu.make_async_copy(y_hbm.at[pl.ds(0, 2 * chunk)], buf.at[pl.ds(base, 2 * chunk)],
                          sem.at[slot]).wait()
    g = g_ref[...]
    o_ref[...] = (x_ref[...] + g[:, 0:1] * _load_row_tiles(buf, base, n)
                  + g[:, 1:2] * _load_row_tiles(buf, base + chunk, n))


def _moe(x2, h2, idx, gates, w1, w3, w2):
    t, d = x2.shape
    fe = w1.shape[2]
    assert d == ROW_SUBLANES * LANES
    rs = ROW_SUBLANES
    n_tiles = (2 * t) // MOE_TILE + N_EXPERTS
    rows = n_tiles * MOE_TILE
    pos, tile_expert, tile_active, pad_tiles = _routing_plan(idx, n_tiles)
    gt = GATHER_TILE
    ng = t // gt
    pos_blocks = pos.reshape(2, ng, gt).transpose(1, 0, 2).reshape(ng, 1, 2 * gt)

    xs = pl.pallas_call(
        _dispatch_kernel,
        grid_spec=pltpu.PrefetchScalarGridSpec(
            num_scalar_prefetch=1,
            grid=(ng,),
            in_specs=[
                pl.BlockSpec((1, 1, 2 * gt), lambda i, pad: (i, 0, 0), memory_space=pltpu.SMEM),
                pl.BlockSpec(memory_space=pl.ANY),
            ],
            out_specs=pl.BlockSpec(memory_space=pl.ANY),
            scratch_shapes=[pltpu.VMEM((MOE_TILE * rs, LANES), F32),
                            pltpu.VMEM((DISPATCH_RING, gt * rs, LANES), F32),
                            pltpu.SemaphoreType.DMA((DISPATCH_RING,)),
                            pltpu.SemaphoreType.DMA((DISPATCH_RING,))],
        ),
        out_shape=jax.ShapeDtypeStruct((rows * rs, LANES), F32),
        compiler_params=_cparams("arbitrary"),
        name="moe_dispatch",
    )(pad_tiles, pos_blocks, h2)

    y = pl.pallas_call(
        _expert_kernel,
        grid_spec=pltpu.PrefetchScalarGridSpec(
            num_scalar_prefetch=2,
            grid=(n_tiles,),
            in_specs=[
                pl.BlockSpec((MOE_TILE * rs, LANES), lambda i, te, ta: (i, 0)),
                pl.BlockSpec((None, d, fe), lambda i, te, ta: (te[i], 0, 0),
                             pipeline_mode=pl.Buffered(1)),
                pl.BlockSpec((None, d, fe), lambda i, te, ta: (te[i], 0, 0),
                             pipeline_mode=pl.Buffered(1)),
                pl.BlockSpec((None, fe, d), lambda i, te, ta: (te[i], 0, 0),
                             pipeline_mode=pl.Buffered(1)),
            ],
            out_specs=pl.BlockSpec((MOE_TILE * rs, LANES), lambda i, te, ta: (i, 0)),
        ),
        out_shape=jax.ShapeDtypeStruct((rows * rs, LANES), F32),
        compiler_params=_cparams("arbitrary"),
        name="moe_experts",
    )(tile_expert, tile_active, xs, w1, w3, w2)

    return pl.pallas_call(
        _combine_kernel,
        grid=(ng,),
        in_specs=[
            pl.BlockSpec((1, 1, 2 * gt), lambda i: (i, 0, 0), memory_space=pltpu.SMEM),
            pl.BlockSpec((1, 1, 2 * gt), lambda i: (jnp.minimum(i + 1, ng - 1), 0, 0),
                         memory_space=pltpu.SMEM),
            pl.BlockSpec((gt, d), lambda i: (i, 0)),
            pl.BlockSpec((gt, 2), lambda i: (i, 0)),
            pl.BlockSpec(memory_space=pl.ANY),
        ],
        out_specs=pl.BlockSpec((gt, d), lambda i: (i, 0)),
        out_shape=jax.ShapeDtypeStruct((t, d), F32),
        scratch_shapes=[pltpu.VMEM((2 * 2 * gt * rs, LANES), F32), pltpu.SemaphoreType.DMA((2,))],
        compiler_params=_cparams("arbitrary"),
        name="moe_combine",
    )(pos_blocks, pos_blocks, x2, gates, y)


def kernel(x, positions, norm_mix, norm_ffn, conv_in, conv_w, conv_out, attn_in, q_norm, k_norm,
           lam_q1, lam_k1, lam_q2, lam_k2, subln, attn_out, ffn_w1, ffn_w3, ffn_w2,
           router, moe_w1, moe_w3, moe_w2):
    batch, seq, d = x.shape
    xt = x.reshape(batch * seq, d)
    bf = lambda w: w.astype(BF16)

    xt = _layer0(xt, norm_mix[0:1], bf(conv_in[0]), conv_w[0], bf(conv_out[0]),
                 norm_ffn[0:1], bf(ffn_w1[0]), bf(ffn_w3[0]), bf(ffn_w2[0]), seq)

    lambda_init = 0.8 - 0.6 * math.exp(-0.3 * 1)
    lam_params = jnp.concatenate([lam_q1, lam_k1, lam_q2, lam_k2], axis=0)
    x2, h2, idx, gates = _diff_attention_and_router(
        xt, positions, norm_mix[1:2], bf(attn_in[0]), q_norm[0], k_norm[0], lam_params, subln[0],
        bf(attn_out[0]), norm_ffn[1:2], router[0], lambda_init, batch, seq)
    out = _moe(x2, h2, idx, gates, moe_w1[0], moe_w3[0], moe_w2[0])
    return out.reshape(batch, seq, d)
```

```python
import functools
import math

import jax
import jax.numpy as jnp
from jax import lax
from jax.experimental import pallas as pl
from jax.experimental.pallas import tpu as pltpu

F32 = jnp.float32
BF16 = jnp.bfloat16

D_MODEL = 1024
N_HEADS = 8
HEAD_DIM = 64
V_DIM = 2 * HEAD_DIM
ROT_DIM = HEAD_DIM // 4
ROPE_THETA = 500000.0
N_EXPERTS = 8
RMS_EPS = 1e-6
LOG2E = 1.4426950408889634
LANES = 128
ROW_SUBLANES = 8

TOKEN_TILE = 512
ATTN_KV_TILE = 512
ATTN_Q_TILES = 2
MOE_TILE = 512
FF_CHUNK = 512
GATHER_TILE = 256
DISPATCH_RING = 3
VMEM_LIMIT = 56 * 1024 * 1024
EXPERT_VMEM_LIMIT = 62 * 1024 * 1024

_NEG_BIG = -1e30
_EXP2_SAFE_LOG2 = 60.0
_BF16_ROUNDING_MARGIN = 1.02


def _cparams(*sem, vmem_limit=VMEM_LIMIT):
    return pltpu.CompilerParams(dimension_semantics=sem, vmem_limit_bytes=vmem_limit)


def _rms_rows(x, g):
    ms = jnp.mean(x * x, axis=-1, keepdims=True)
    return x * lax.rsqrt(ms + RMS_EPS) * g


def _dot(a, b):
    return jnp.dot(a, b, preferred_element_type=F32)


def _silu(a):
    return a * (1.0 / (1.0 + jnp.exp(-a)))


def _ff_chunks(width):
    out, f0 = [], 0
    while f0 < width:
        fl = min(FF_CHUNK, width - f0)
        out.append((f0, fl))
        f0 += fl
    return out


def _conv_in_kernel(x_ref, g_ref, w_ref, b_ref, u_ref):
    d = x_ref.shape[1]
    h = _rms_rows(x_ref[...], g_ref[...]).astype(BF16)
    b_ref[...] = _dot(h, w_ref[:, 0:d]).astype(BF16)
    c = _dot(h, w_ref[:, d:2 * d])
    v = _dot(h, w_ref[:, 2 * d:3 * d])
    u_ref[...] = (c * v).astype(BF16)


def _conv_out_ffn_kernel(tiles_per_seq, u_ref, up_ref, un_ref, b_ref, x_ref, cw_ref, w_ref,
                         g_ref, w1_ref, w3_ref, w2_ref, o_ref):
    i = pl.program_id(0)
    tm = u_ref.shape[0]
    halo = up_ref.shape[0]
    u = u_ref[...].astype(F32)
    first = (i % tiles_per_seq) == 0
    last = (i % tiles_per_seq) == tiles_per_seq - 1
    prev_row = jnp.where(first, 0.0, up_ref[halo - 1:halo, :].astype(F32))
    next_row = jnp.where(last, 0.0, un_ref[0:1, :].astype(F32))
    rows = lax.broadcasted_iota(jnp.int32, (tm, 1), 0)
    u_m1 = jnp.where(rows == 0, prev_row, pltpu.roll(u, 1, 0))
    u_p1 = jnp.where(rows == tm - 1, next_row, pltpu.roll(u, tm - 1, 0))
    cw = cw_ref[...]
    conv = u_m1 * cw[0:1, :] + u * cw[1:2, :] + u_p1 * cw[2:3, :]
    y = (b_ref[...].astype(F32) * conv).astype(BF16)
    x1 = x_ref[...] + _dot(y, w_ref[...])
    h = _rms_rows(x1, g_ref[...]).astype(BF16)
    acc = x1
    for f0, fl in _ff_chunks(w1_ref.shape[1]):
        a = _dot(h, w1_ref[:, f0:f0 + fl])
        b = _dot(h, w3_ref[:, f0:f0 + fl])
        act = (_silu(a) * b).astype(BF16)
        acc = acc + _dot(act, w2_ref[f0:f0 + fl, :])
    o_ref[...] = acc


def _resident(shape):
    return pl.BlockSpec(shape, lambda i: (0,) * len(shape), pipeline_mode=pl.Buffered(1))


def _layer0(x, g_mix, w_in, conv_w, w_out, g_ffn, w1, w3, w2, seq):
    t, d = x.shape
    f = w1.shape[1]
    tm = TOKEN_TILE
    nt = t // tm
    halo = 16
    b, u = pl.pallas_call(
        _conv_in_kernel,
        grid=(nt,),
        in_specs=[
            pl.BlockSpec((tm, d), lambda i: (i, 0)),
            _resident((1, d)),
            _resident((d, 3 * d)),
        ],
        out_specs=[pl.BlockSpec((tm, d), lambda i: (i, 0))] * 2,
        out_shape=[jax.ShapeDtypeStruct((t, d), BF16)] * 2,
        compiler_params=_cparams("parallel"),
        name="conv_in",
    )(x, g_mix, w_in)
    hb = tm // halo
    nhb = t // halo
    return pl.pallas_call(
        functools.partial(_conv_out_ffn_kernel, seq // tm),
        grid=(nt,),
        in_specs=[
            pl.BlockSpec((tm, d), lambda i: (i, 0)),
            pl.BlockSpec((halo, d), lambda i: (jnp.maximum(i * hb - 1, 0), 0)),
            pl.BlockSpec((halo, d), lambda i: (jnp.minimum((i + 1) * hb, nhb - 1), 0)),
            pl.BlockSpec((tm, d), lambda i: (i, 0)),
            pl.BlockSpec((tm, d), lambda i: (i, 0)),
            _resident((3, d)),
            _resident((d, d)),
            _resident((1, d)),
            _resident((d, f)),
            _resident((d, f)),
            _resident((f, d)),
        ],
        out_specs=pl.BlockSpec((tm, d), lambda i: (i, 0)),
        out_shape=jax.ShapeDtypeStruct((t, d), F32),
        compiler_params=_cparams("parallel"),
        name="conv_out_ffn",
    )(u, u, u, b, x, conv_w, w_out, g_ffn, w1, w3, w2)


def _qkv_kernel(x_ref, g_ref, wt_ref, pos_ref, freq_ref, qn_ref, kn_ref, q_ref, k_ref, v_ref):
    tm, d = x_ref.shape
    h = _rms_rows(x_ref[...], g_ref[...]).astype(BF16)
    nt = (((1,), (1,)), ((), ()))
    ang = pos_ref[...].astype(F32) * freq_ref[...]
    cos = jnp.cos(ang)[None]
    sin = jnp.sin(ang)[None]
    half = ROT_DIM // 2
    groups = d // HEAD_DIM

    def norm_rope(w_rows, gain, scale):
        t = lax.dot_general(w_rows, h, nt, preferred_element_type=F32)
        t = t.reshape(groups, HEAD_DIM, tm)
        ms = jnp.mean(t * t, axis=1, keepdims=True)
        t = t * lax.rsqrt(ms + RMS_EPS) * gain[None]
        t1 = t[:, 0:half, :]
        t2 = t[:, half:ROT_DIM, :]
        t = jnp.concatenate([t1 * cos - t2 * sin, t2 * cos + t1 * sin, t[:, ROT_DIM:, :]], axis=1)
        return (t * scale).reshape(d, tm)

    q = norm_rope(wt_ref[0:d, :], qn_ref[...], (HEAD_DIM ** -0.5) * LOG2E)
    q_ref[0] = q.astype(BF16)
    k = norm_rope(wt_ref[d:2 * d, :], kn_ref[...], 1.0)
    k_ref[...] = k.T.astype(BF16)
    v = lax.dot_general(wt_ref[2 * d:3 * d, :], h, nt, preferred_element_type=F32)
    v_ref[0] = v.astype(BF16)


def _attn_kernel(lambda_init, unshifted_ref, q_ref, k_ref, v_ref, lam_ref, sub_ref, o_ref,
                 acc_ref, p_ref):
    nqt, _, tq = q_ref.shape
    nkb = v_ref.shape[0]
    tk = v_ref.shape[2]
    rows = lax.broadcasted_iota(jnp.int32, (V_DIM, 1), 0)
    acc_ref[...] = jnp.zeros_like(acc_ref)
    zer = jnp.zeros((1, tq), F32)

    def split_q(t):
        q = q_ref[t]
        zero = jnp.zeros_like(q)
        return jnp.where(rows < HEAD_DIM, q, zero), jnp.where(rows >= HEAD_DIM, q, zero)

    def load_kv(kb):
        k = k_ref[pl.ds(pl.multiple_of(kb * tk, tk), tk), :]
        return k, v_ref[kb]

    def finish(t, l1, l2):
        lp = lam_ref[...]
        lam = (jnp.exp(jnp.sum(lp[0:1, :] * lp[1:2, :], axis=-1, keepdims=True))
               - jnp.exp(jnp.sum(lp[2:3, :] * lp[3:4, :], axis=-1, keepdims=True))
               + lambda_init)
        o = acc_ref[t, 0] / l1 - lam * (acc_ref[t, 1] / l2)
        ms = jnp.mean(o * o, axis=0, keepdims=True)
        o = o * lax.rsqrt(ms + RMS_EPS) * sub_ref[...] * (1.0 - lambda_init)
        o_ref[t] = o.astype(BF16)

    @pl.when(unshifted_ref[0] == 1)
    def _():
        def scores(qs, kb, l, slot):
            k = load_kv(kb)[0]
            ps = [jnp.exp2(_dot(k, qc)) for qc in qs]
            for c, p in enumerate(ps):
                p_ref[slot, c] = p.astype(BF16)
            return [lc + jnp.sum(p, axis=0, keepdims=True) for lc, p in zip(l, ps)]

        def values(t, kb, slot):
            v = v_ref[kb]
            for c in (0, 1):
                acc_ref[t, c] += _dot(v, p_ref[slot, c])

        stage = 0
        pending = None
        for t in range(nqt):
            qs = split_q(t)
            l = [zer, zer]
            for kb in range(nkb):
                l = scores(qs, kb, l, stage % 2)
                if pending is not None:
                    values(*pending)
                    if pending[1] == nkb - 1:
                        finish(pending[0], *l_done)
                pending = (t, kb, stage % 2)
                stage += 1
            l_done = l
        values(*pending)
        finish(pending[0], *l_done)

    @pl.when(unshifted_ref[0] == 0)
    def _():
        def one_map(k, v, qc, m, l, t, c):
            s = _dot(k, qc)
            m_new = jnp.maximum(m, jnp.max(s, axis=0, keepdims=True))
            alpha = jnp.exp2(m - m_new)
            p = jnp.exp2(s - m_new)
            acc_ref[t, c] = alpha * acc_ref[t, c] + _dot(v, p.astype(BF16))
            return m_new, alpha * l + jnp.sum(p, axis=0, keepdims=True)

        neg = jnp.full((1, tq), _NEG_BIG, F32)
        for t in range(nqt):
            q1, q2 = split_q(t)

            def body(kb, carry):
                k, v = load_kv(kb)
                m1, l1 = one_map(k, v, q1, carry[0], carry[1], t, 0)
                m2, l2 = one_map(k, v, q2, carry[2], carry[3], t, 1)
                return m1, l1, m2, l2

            _, l1, _, l2 = lax.fori_loop(0, nkb, body, (neg, zer, neg, zer))
            finish(t, l1, l2)


def _store_row_tiles(ref, x):
    rows = x.shape[0]
    for j in range(ROW_SUBLANES):
        ref[pl.ds(j, rows, stride=ROW_SUBLANES), :] = x[:, j * LANES:(j + 1) * LANES]


def _load_row_tiles(ref, base, rows):
    return jnp.concatenate(
        [ref[pl.ds(base + j, rows, stride=ROW_SUBLANES), :] for j in range(ROW_SUBLANES)], axis=1)


def _attn_out_router_kernel(o_ref, x_ref, w_ref, g_ref, rt_ref, x2_ref, h2_ref, idx_ref, gate_ref):
    tn = (((0,), (0,)), ((), ()))
    y = lax.dot_general(o_ref[0], w_ref[...], tn, preferred_element_type=F32)
    x2 = x_ref[...] + y
    x2_ref[...] = x2
    h2 = _rms_rows(x2, g_ref[...])
    _store_row_tiles(h2_ref, h2)
    ne = rt_ref.shape[0]
    nt = (((1,), (1,)), ((), ()))
    r = rt_ref[...]
    r_hi = r.astype(BF16)
    r_lo = (r - r_hi.astype(F32)).astype(BF16)
    h_hi = h2.astype(BF16)
    h_lo = (h2 - h_hi.astype(F32)).astype(BF16)
    both = lax.dot_general(jnp.concatenate([r_hi, r_lo], axis=0), h_hi, nt, preferred_element_type=F32)
    logits = (both[0:ne] + both[ne:2 * ne]
              + lax.dot_general(r_hi, h_lo, nt, preferred_element_type=F32))
    eid = lax.broadcasted_iota(jnp.int32, logits.shape, 0)
    m1 = jnp.max(logits, axis=0, keepdims=True)
    i1 = jnp.min(jnp.where(logits == m1, eid, ne), axis=0, keepdims=True)
    rest = jnp.where(eid == i1, -jnp.inf, logits)
    m2 = jnp.max(rest, axis=0, keepdims=True)
    i2 = jnp.min(jnp.where(rest == m2, eid, ne), axis=0, keepdims=True)
    e2 = jnp.exp(m2 - m1)
    den = 1.0 + e2
    idx_ref[0:1, :] = i1
    idx_ref[1:2, :] = i2
    gate_ref[0:1, :] = 1.0 / den
    gate_ref[1:2, :] = e2 / den


def _diff_attention_and_router(x, positions, g_mix, w_in, q_norm, k_norm, lam_params, subln, w_out,
                               g_ffn, router, lambda_init, batch, seq):
    t, d = x.shape
    tm = TOKEN_TILE
    nt = t // tm
    inv_freq = ROPE_THETA ** (-jnp.arange(0, ROT_DIM, 2, dtype=F32) / ROT_DIM)
    q, k, v = pl.pallas_call(
        _qkv_kernel,
        grid=(nt,),
        in_specs=[
            pl.BlockSpec((tm, d), lambda i: (i, 0)),
            _resident((1, d)),
            _resident((3 * d, d)),
            pl.BlockSpec((1, tm), lambda i: (0, i)),
            _resident((ROT_DIM // 2, 1)),
            _resident((HEAD_DIM, 1)),
            _resident((HEAD_DIM, 1)),
        ],
        out_specs=[
            pl.BlockSpec((1, d, tm), lambda i: (i, 0, 0)),
            pl.BlockSpec((tm, d), lambda i: (i, 0)),
            pl.BlockSpec((1, d, tm), lambda i: (i, 0, 0)),
        ],
        out_shape=[
            jax.ShapeDtypeStruct((nt, d, tm), BF16),
            jax.ShapeDtypeStruct((t, d), BF16),
            jax.ShapeDtypeStruct((nt, d, tm), BF16),
        ],
        compiler_params=_cparams("parallel"),
        name="attn_qkv",
    )(x, g_mix, w_in.T, positions.reshape(1, t), inv_freq.reshape(-1, 1),
      q_norm.reshape(-1, 1), k_norm.reshape(-1, 1))

    nq = seq // tm
    qg = ATTN_Q_TILES
    ngrp = nq // qg
    score_bound = (HEAD_DIM ** 0.5 * LOG2E * _BF16_ROUNDING_MARGIN
                   * jnp.max(jnp.abs(q_norm)) * jnp.max(jnp.abs(k_norm)))
    unshifted = (score_bound <= _EXP2_SAFE_LOG2).astype(jnp.int32).reshape(1)
    o = pl.pallas_call(
        functools.partial(_attn_kernel, lambda_init),
        grid_spec=pltpu.PrefetchScalarGridSpec(
            num_scalar_prefetch=1,
            grid=(batch, N_HEADS, ngrp),
            in_specs=[
                pl.BlockSpec((qg, V_DIM, tm), lambda b, h, i, f: (b * ngrp + i, h, 0)),
                pl.BlockSpec((seq, V_DIM), lambda b, h, i, f: (b, h)),
                pl.BlockSpec((nq, V_DIM, tm), lambda b, h, i, f: (b, h, 0)),
                pl.BlockSpec((4, HEAD_DIM), lambda b, h, i, f: (0, 0)),
                pl.BlockSpec((V_DIM, 1), lambda b, h, i, f: (0, 0)),
            ],
            out_specs=pl.BlockSpec((qg, V_DIM, tm), lambda b, h, i, f: (b * ngrp + i, h, 0)),
            scratch_shapes=[pltpu.VMEM((qg, 2, V_DIM, tm), F32),
                            pltpu.VMEM((2, 2, tm, tm), BF16)],
        ),
        out_shape=jax.ShapeDtypeStruct((nt, d, tm), BF16),
        compiler_params=_cparams("parallel", "parallel", "arbitrary"),
        name="diff_attn",
    )(unshifted, q, k, v, lam_params, subln.reshape(-1, 1))

    x2, h2, idx, gates = pl.pallas_call(
        _attn_out_router_kernel,
        grid=(nt,),
        in_specs=[
            pl.BlockSpec((1, d, tm), lambda i: (i, 0, 0)),
            pl.BlockSpec((tm, d), lambda i: (i, 0)),
            _resident((d, d)),
            _resident((1, d)),
            _resident((N_EXPERTS, d)),
        ],
        out_specs=[
            pl.BlockSpec((tm, d), lambda i: (i, 0)),
            pl.BlockSpec((tm * ROW_SUBLANES, LANES), lambda i: (i, 0)),
            pl.BlockSpec((2, tm), lambda i: (0, i)),
            pl.BlockSpec((2, tm), lambda i: (0, i)),
        ],
        out_shape=[
            jax.ShapeDtypeStruct((t, d), F32),
            jax.ShapeDtypeStruct((t * ROW_SUBLANES, LANES), F32),
            jax.ShapeDtypeStruct((2, t), jnp.int32),
            jax.ShapeDtypeStruct((2, t), F32),
        ],
        compiler_params=_cparams("parallel"),
        name="attn_out_router",
    )(o, x, w_out, g_ffn, router.T)
    return x2, h2, idx.T, gates.T


def _routing_plan(idx, n_tiles):
    hot = (jax.nn.one_hot(idx[:, 0], N_EXPERTS, dtype=jnp.int32)
           + jax.nn.one_hot(idx[:, 1], N_EXPERTS, dtype=jnp.int32))
    incl = jnp.cumsum(hot, axis=0)
    rank = incl - hot
    tiles = (incl[-1] + MOE_TILE - 1) // MOE_TILE
    tile_end = jnp.cumsum(tiles)
    row_start = (tile_end - tiles) * MOE_TILE
    slot = row_start[None, :] + rank
    pos = jnp.take_along_axis(slot, idx, axis=1).T
    tile_id = jnp.arange(n_tiles, dtype=jnp.int32)
    tile_expert = jnp.minimum(jnp.sum(tile_id[:, None] >= tile_end[None, :], axis=1), N_EXPERTS - 1)
    new_expert = jnp.concatenate([jnp.ones((1,), bool), tile_expert[1:] != tile_expert[:-1]])
    tile_state = jnp.where(tile_id < tile_end[-1], 1 + new_expert.astype(jnp.int32), 0)
    last_tile = jnp.where(tiles > 0, tile_end - 1, -1)
    tail = tile_end[-1] + jnp.arange(N_EXPERTS, dtype=jnp.int32)
    pad_tiles = jnp.concatenate([last_tile, jnp.where(tail < n_tiles, tail, -1)])
    return (pos.astype(jnp.int32), tile_expert.astype(jnp.int32), tile_state.astype(jnp.int32),
            pad_tiles.astype(jnp.int32))


def _row_tile(ref, row):
    return ref.at[pl.ds(pl.multiple_of(row * ROW_SUBLANES, ROW_SUBLANES), ROW_SUBLANES)]


def _dispatch_kernel(pad_ref, pos_ref, h_hbm, xs_hbm, zero_ref, ring, load_sem, row_sem):
    i = pl.program_id(0)
    last = pl.num_programs(0) - 1
    n = pos_ref.shape[2] // 2
    mt = zero_ref.shape[0]
    nslot = ring.shape[0]
    block_rows = n * ROW_SUBLANES

    def load(step):
        src = h_hbm.at[pl.ds(pl.multiple_of(step * block_rows, block_rows), block_rows)]
        return pltpu.make_async_copy(src, ring.at[step % nslot], load_sem.at[step % nslot])

    def wait_rows(step):
        for _ in (0, 1):
            pltpu.make_async_copy(ring.at[0], xs_hbm.at[pl.ds(0, block_rows)],
                                  row_sem.at[step % nslot]).wait()

    @pl.when(i == 0)
    def _():
        load(0).start()
        zero_ref[...] = jnp.zeros_like(zero_ref)
        fills = [pltpu.make_async_copy(
            zero_ref, xs_hbm.at[pl.ds(pl.multiple_of(jnp.maximum(pad_ref[j], 0) * mt, mt), mt)],
            row_sem.at[0]) for j in range(pad_ref.shape[0])]
        for j, fill in enumerate(fills):
            pl.when(pad_ref[j] >= 0)(fill.start)
        for j, fill in enumerate(fills):
            pl.when(pad_ref[j] >= 0)(fill.wait)

    pl.when(i < last)(lambda: load(i + 1).start())
    load(i).wait()
    block = ring.at[i % nslot]

    def issue(r, carry):
        for c in (0, 1):
            pltpu.make_async_copy(_row_tile(block, r), _row_tile(xs_hbm, pos_ref[0, 0, c * n + r]),
                                  row_sem.at[i % nslot]).start(priority=c)
        return carry

    lax.fori_loop(0, n, issue, 0, unroll=8)
    pl.when(i > 0)(lambda: wait_rows(i - 1))
    pl.when(i == last)(lambda: wait_rows(i))


def _expert_kernel(te_ref, ts_ref, xs_ref, w1_hbm, w3_hbm, w2_hbm, y_ref, w1_ref, w3_ref, w2_ref, sem):
    i = pl.program_id(0)
    mt = xs_ref.shape[0] // ROW_SUBLANES
    chunks = _ff_chunks(w1_ref.shape[1])

    def swiglu_tile(before_chunk):
        x = _load_row_tiles(xs_ref, 0, mt)
        acc = None
        for c, (f0, fl) in enumerate(chunks):
            before_chunk(c)
            a = _dot(x, w1_ref[:, f0:f0 + fl])
            b = _dot(x, w3_ref[:, f0:f0 + fl])
            act = _silu(a) * b
            part = _dot(act, w2_ref[f0:f0 + fl, :])
            acc = part if acc is None else acc + part
        _store_row_tiles(y_ref, acc)

    @pl.when(ts_ref[i] == 2)
    def _():
        e = te_ref[i]
        fetch = []
        for c, (f0, fl) in enumerate(chunks):
            cols = pl.ds(f0, fl)
            fetch.append([
                pltpu.make_async_copy(w1_hbm.at[e, :, cols], w1_ref.at[:, cols], sem.at[c]),
                pltpu.make_async_copy(w3_hbm.at[e, :, cols], w3_ref.at[:, cols], sem.at[c]),
                pltpu.make_async_copy(w2_hbm.at[e, cols, :], w2_ref.at[cols, :], sem.at[c]),
            ])
        for copies in fetch:
            for copy in copies:
                copy.start()

        def wait_chunk(c):
            for copy in fetch[c]:
                copy.wait()

        swiglu_tile(wait_chunk)

    @pl.when(ts_ref[i] == 1)
    def _():
        swiglu_tile(lambda c: None)

    @pl.when(ts_ref[i] == 0)
    def _():
        y_ref[...] = jnp.zeros_like(y_ref)


def _combine_kernel(pos_ref, pos_next_ref, x_ref, g_ref, y_hbm, o_ref, buf, sem):
    i = pl.program_id(0)
    n = x_ref.shape[0]

    def gather(p_ref, slot):
        def issue(r, carry):
            for c in (0, 1):
                pltpu.make_async_copy(_row_tile(y_hbm, p_ref[0, 0, c * n + r]),
                                      _row_tile(buf, (slot * 2 + c) * n + r),
                                      sem.at[slot]).start(priority=c)
            return carry
        lax.fori_loop(0, n, issue, 0, unroll=8)

    @pl.when(i == 0)
    def _():
        gather(pos_ref, 0)

    @pl.when(i + 1 < pl.num_programs(0))
    def _():
        gather(pos_next_ref, (i + 1) % 2)

    slot = i % 2
    chunk = n * ROW_SUBLANES
    base = pl.multiple_of(slot * 2 * chunk, chunk)
    pltpu.make_async_copy(y_hbm.at[pl.ds(0, 2 * chunk)], buf.at[pl.ds(base, 2 * chunk)],
                          sem.at[slot]).wait()
    g = g_ref[...]
    o_ref[...] = (x_ref[...] + g[:, 0:1] * _load_row_tiles(buf, base, n)
                  + g[:, 1:2] * _load_row_tiles(buf, base + chunk, n))


def _moe(x2, h2, idx, gates, w1, w3, w2):
    t, d = x2.shape
    fe = w1.shape[2]
    assert d == ROW_SUBLANES * LANES
    rs = ROW_SUBLANES
    n_tiles = (2 * t) // MOE_TILE + N_EXPERTS
    rows = n_tiles * MOE_TILE
    pos, tile_expert, tile_state, pad_tiles = _routing_plan(idx, n_tiles)
    gt = GATHER_TILE
    ng = t // gt
    pos_blocks = pos.reshape(2, ng, gt).transpose(1, 0, 2).reshape(ng, 1, 2 * gt)

    xs = pl.pallas_call(
        _dispatch_kernel,
        grid_spec=pltpu.PrefetchScalarGridSpec(
            num_scalar_prefetch=1,
            grid=(ng,),
            in_specs=[
                pl.BlockSpec((1, 1, 2 * gt), lambda i, pad: (i, 0, 0), memory_space=pltpu.SMEM),
                pl.BlockSpec(memory_space=pl.ANY),
            ],
            out_specs=pl.BlockSpec(memory_space=pl.ANY),
            scratch_shapes=[pltpu.VMEM((MOE_TILE * rs, LANES), F32),
                            pltpu.VMEM((DISPATCH_RING, gt * rs, LANES), F32),
                            pltpu.SemaphoreType.DMA((DISPATCH_RING,)),
                            pltpu.SemaphoreType.DMA((DISPATCH_RING,))],
        ),
        out_shape=jax.ShapeDtypeStruct((rows * rs, LANES), F32),
        compiler_params=_cparams("arbitrary"),
        name="moe_dispatch",
    )(pad_tiles, pos_blocks, h2)

    y = pl.pallas_call(
        _expert_kernel,
        grid_spec=pltpu.PrefetchScalarGridSpec(
            num_scalar_prefetch=2,
            grid=(n_tiles,),
            in_specs=[
                pl.BlockSpec((MOE_TILE * rs, LANES), lambda i, te, ts: (i, 0)),
                pl.BlockSpec(memory_space=pl.ANY),
                pl.BlockSpec(memory_space=pl.ANY),
                pl.BlockSpec(memory_space=pl.ANY),
            ],
            out_specs=pl.BlockSpec((MOE_TILE * rs, LANES), lambda i, te, ts: (i, 0)),
            scratch_shapes=[pltpu.VMEM((d, fe), F32), pltpu.VMEM((d, fe), F32),
                            pltpu.VMEM((fe, d), F32),
                            pltpu.SemaphoreType.DMA((len(_ff_chunks(fe)),))],
        ),
        out_shape=jax.ShapeDtypeStruct((rows * rs, LANES), F32),
        compiler_params=_cparams("arbitrary", vmem_limit=EXPERT_VMEM_LIMIT),
        name="moe_experts",
    )(tile_expert, tile_state, xs, w1, w3, w2)

    return pl.pallas_call(
        _combine_kernel,
        grid=(ng,),
        in_specs=[
            pl.BlockSpec((1, 1, 2 * gt), lambda i: (i, 0, 0), memory_space=pltpu.SMEM),
            pl.BlockSpec((1, 1, 2 * gt), lambda i: (jnp.minimum(i + 1, ng - 1), 0, 0),
                         memory_space=pltpu.SMEM),
            pl.BlockSpec((gt, d), lambda i: (i, 0)),
            pl.BlockSpec((gt, 2), lambda i: (i, 0)),
            pl.BlockSpec(memory_space=pl.ANY),
        ],
        out_specs=pl.BlockSpec((gt, d), lambda i: (i, 0)),
        out_shape=jax.ShapeDtypeStruct((t, d), F32),
        scratch_shapes=[pltpu.VMEM((2 * 2 * gt * rs, LANES), F32), pltpu.SemaphoreType.DMA((2,))],
        compiler_params=_cparams("arbitrary"),
        name="moe_combine",
    )(pos_blocks, pos_blocks, x2, gates, y)


def kernel(x, positions, norm_mix, norm_ffn, conv_in, conv_w, conv_out, attn_in, q_norm, k_norm,
           lam_q1, lam_k1, lam_q2, lam_k2, subln, attn_out, ffn_w1, ffn_w3, ffn_w2,
           router, moe_w1, moe_w3, moe_w2):
    batch, seq, d = x.shape
    xt = x.reshape(batch * seq, d)
    bf = lambda w: w.astype(BF16)

    xt = _layer0(xt, norm_mix[0:1], bf(conv_in[0]), conv_w[0], bf(conv_out[0]),
                 norm_ffn[0:1], bf(ffn_w1[0]), bf(ffn_w3[0]), bf(ffn_w2[0]), seq)

    lambda_init = 0.8 - 0.6 * math.exp(-0.3 * 1)
    lam_params = jnp.concatenate([lam_q1, lam_k1, lam_q2, lam_k2], axis=0)
    x2, h2, idx, gates = _diff_attention_and_router(
        xt, positions, norm_mix[1:2], bf(attn_in[0]), q_norm[0], k_norm[0], lam_params, subln[0],
        bf(attn_out[0]), norm_ffn[1:2], router[0], lambda_init, batch, seq)
    out = _moe(x2, h2, idx, gates, moe_w1[0], moe_w3[0], moe_w2[0])
    return out.reshape(batch, seq, d)
```

```python
import functools
import math

import jax
import jax.numpy as jnp
from jax import lax
from jax.experimental import pallas as pl
from jax.experimental.pallas import tpu as pltpu

F32 = jnp.float32
BF16 = jnp.bfloat16

D_MODEL = 1024
N_HEADS = 8
HEAD_DIM = 64
V_DIM = 2 * HEAD_DIM
ROT_DIM = HEAD_DIM // 4
ROPE_THETA = 500000.0
N_EXPERTS = 8
RMS_EPS = 1e-6
LOG2E = 1.4426950408889634
LANES = 128
ROW_SUBLANES = 8

TOKEN_TILE = 512
ATTN_KV_TILE = 512
ATTN_Q_TILES = 2
MOE_TILE = 512
FF_CHUNK = 512
GATHER_TILE = 256
DISPATCH_RING = 3
VMEM_LIMIT = 56 * 1024 * 1024
EXPERT_VMEM_LIMIT = 62 * 1024 * 1024

_NEG_BIG = -1e30
_EXP2_SAFE_LOG2 = 60.0
_BF16_ROUNDING_MARGIN = 1.02


def _cparams(*sem, vmem_limit=VMEM_LIMIT):
    return pltpu.CompilerParams(dimension_semantics=sem, vmem_limit_bytes=vmem_limit)


def _rms_rows(x, g):
    ms = jnp.mean(x * x, axis=-1, keepdims=True)
    return x * lax.rsqrt(ms + RMS_EPS) * g


def _dot(a, b):
    return jnp.dot(a, b, preferred_element_type=F32)


def _silu(a):
    return a * (1.0 / (1.0 + jnp.exp(-a)))


def _ff_chunks(width):
    out, f0 = [], 0
    while f0 < width:
        fl = min(FF_CHUNK, width - f0)
        out.append((f0, fl))
        f0 += fl
    return out


def _conv_in_kernel(x_ref, g_ref, w_ref, b_ref, u_ref):
    d = x_ref.shape[1]
    h = _rms_rows(x_ref[...], g_ref[...])
    b_ref[...] = _dot(h, w_ref[:, 0:d]).astype(BF16)
    c = _dot(h, w_ref[:, d:2 * d])
    v = _dot(h, w_ref[:, 2 * d:3 * d])
    u_ref[...] = (c * v).astype(BF16)


def _conv_out_ffn_kernel(tiles_per_seq, u_ref, up_ref, un_ref, b_ref, x_ref, cw_ref, w_ref,
                         g_ref, w1_ref, w3_ref, w2_ref, o_ref):
    i = pl.program_id(0)
    tm = u_ref.shape[0]
    halo = up_ref.shape[0]
    u = u_ref[...].astype(F32)
    first = (i % tiles_per_seq) == 0
    last = (i % tiles_per_seq) == tiles_per_seq - 1
    prev_row = jnp.where(first, 0.0, up_ref[halo - 1:halo, :].astype(F32))
    next_row = jnp.where(last, 0.0, un_ref[0:1, :].astype(F32))
    rows = lax.broadcasted_iota(jnp.int32, (tm, 1), 0)
    u_m1 = jnp.where(rows == 0, prev_row, pltpu.roll(u, 1, 0))
    u_p1 = jnp.where(rows == tm - 1, next_row, pltpu.roll(u, tm - 1, 0))
    cw = cw_ref[...]
    conv = u_m1 * cw[0:1, :] + u * cw[1:2, :] + u_p1 * cw[2:3, :]
    y = (b_ref[...].astype(F32) * conv).astype(BF16)
    x1 = x_ref[...] + _dot(y, w_ref[...])
    h = _rms_rows(x1, g_ref[...]).astype(BF16)
    acc = x1
    for f0, fl in _ff_chunks(w1_ref.shape[1]):
        a = _dot(h, w1_ref[:, f0:f0 + fl])
        b = _dot(h, w3_ref[:, f0:f0 + fl])
        act = (_silu(a) * b).astype(BF16)
        acc = acc + _dot(act, w2_ref[f0:f0 + fl, :])
    o_ref[...] = acc


def _resident(shape):
    return pl.BlockSpec(shape, lambda i: (0,) * len(shape), pipeline_mode=pl.Buffered(1))


def _layer0(x, g_mix, w_in, conv_w, w_out, g_ffn, w1, w3, w2, seq):
    t, d = x.shape
    f = w1.shape[1]
    tm = TOKEN_TILE
    nt = t // tm
    halo = 16
    b, u = pl.pallas_call(
        _conv_in_kernel,
        grid=(nt,),
        in_specs=[
            pl.BlockSpec((tm, d), lambda i: (i, 0)),
            _resident((1, d)),
            _resident((d, 3 * d)),
        ],
        out_specs=[pl.BlockSpec((tm, d), lambda i: (i, 0))] * 2,
        out_shape=[jax.ShapeDtypeStruct((t, d), BF16)] * 2,
        compiler_params=_cparams("parallel"),
        name="conv_in",
    )(x, g_mix, w_in)
    hb = tm // halo
    nhb = t // halo
    return pl.pallas_call(
        functools.partial(_conv_out_ffn_kernel, seq // tm),
        grid=(nt,),
        in_specs=[
            pl.BlockSpec((tm, d), lambda i: (i, 0)),
            pl.BlockSpec((halo, d), lambda i: (jnp.maximum(i * hb - 1, 0), 0)),
            pl.BlockSpec((halo, d), lambda i: (jnp.minimum((i + 1) * hb, nhb - 1), 0)),
            pl.BlockSpec((tm, d), lambda i: (i, 0)),
            pl.BlockSpec((tm, d), lambda i: (i, 0)),
            _resident((3, d)),
            _resident((d, d)),
            _resident((1, d)),
            _resident((d, f)),
            _resident((d, f)),
            _resident((f, d)),
        ],
        out_specs=pl.BlockSpec((tm, d), lambda i: (i, 0)),
        out_shape=jax.ShapeDtypeStruct((t, d), F32),
        compiler_params=_cparams("parallel"),
        name="conv_out_ffn",
    )(u, u, u, b, x, conv_w, w_out, g_ffn, w1, w3, w2)


def _qkv_kernel(x_ref, g_ref, w_ref, pos_ref, freq_ref, qn_ref, kn_ref, q_ref, k_ref, v_ref, wt_ref):
    tm, d = x_ref.shape

    @pl.when(pl.program_id(0) == 0)
    def _():
        for j in range(3):
            wt_ref[j * d:(j + 1) * d, :] = w_ref[:, j * d:(j + 1) * d].T.astype(BF16)

    h = _rms_rows(x_ref[...], g_ref[...]).astype(BF16)
    nt = (((1,), (1,)), ((), ()))
    ang = pos_ref[...].astype(F32) * freq_ref[...]
    cos = jnp.cos(ang)[None]
    sin = jnp.sin(ang)[None]
    half = ROT_DIM // 2
    groups = d // HEAD_DIM

    def norm_rope(w_rows, gain, scale):
        t = lax.dot_general(w_rows, h, nt, preferred_element_type=F32)
        t = t.reshape(groups, HEAD_DIM, tm)
        ms = jnp.mean(t * t, axis=1, keepdims=True)
        t = t * lax.rsqrt(ms + RMS_EPS) * gain[None]
        t1 = t[:, 0:half, :]
        t2 = t[:, half:ROT_DIM, :]
        t = jnp.concatenate([t1 * cos - t2 * sin, t2 * cos + t1 * sin, t[:, ROT_DIM:, :]], axis=1)
        return (t * scale).reshape(d, tm)

    q = norm_rope(wt_ref[0:d, :], qn_ref[...], (HEAD_DIM ** -0.5) * LOG2E)
    q_ref[0] = q.astype(BF16)
    k = norm_rope(wt_ref[d:2 * d, :], kn_ref[...], 1.0)
    k_ref[...] = k.T.astype(BF16)
    v = lax.dot_general(wt_ref[2 * d:3 * d, :], h, nt, preferred_element_type=F32)
    v_ref[0] = v.astype(BF16)


def _attn_kernel(lambda_init, unshifted_ref, q_ref, k_ref, v_ref, lam_ref, sub_ref, o_ref,
                 acc_ref, p_ref):
    nqt, _, tq = q_ref.shape
    nkb = v_ref.shape[0]
    tk = v_ref.shape[2]
    rows = lax.broadcasted_iota(jnp.int32, (V_DIM, 1), 0)
    acc_ref[...] = jnp.zeros_like(acc_ref)
    zer = jnp.zeros((1, tq), F32)

    def split_q(t):
        q = q_ref[t]
        zero = jnp.zeros_like(q)
        return jnp.where(rows < HEAD_DIM, q, zero), jnp.where(rows >= HEAD_DIM, q, zero)

    def load_kv(kb):
        k = k_ref[pl.ds(pl.multiple_of(kb * tk, tk), tk), :]
        return k, v_ref[kb]

    def finish(t, l1, l2):
        lp = lam_ref[...]
        lam = (jnp.exp(jnp.sum(lp[0:1, :] * lp[1:2, :], axis=-1, keepdims=True))
               - jnp.exp(jnp.sum(lp[2:3, :] * lp[3:4, :], axis=-1, keepdims=True))
               + lambda_init)
        o = acc_ref[t, 0] / l1 - lam * (acc_ref[t, 1] / l2)
        ms = jnp.mean(o * o, axis=0, keepdims=True)
        o = o * lax.rsqrt(ms + RMS_EPS) * sub_ref[...] * (1.0 - lambda_init)
        o_ref[t] = o.astype(BF16)

    @pl.when(unshifted_ref[0] == 1)
    def _():
        def scores(qs, kb, l, slot):
            k = load_kv(kb)[0]
            ps = [jnp.exp2(_dot(k, qc)) for qc in qs]
            for c, p in enumerate(ps):
                p_ref[slot, c] = p.astype(BF16)
            return [lc + jnp.sum(p, axis=0, keepdims=True) for lc, p in zip(l, ps)]

        def values(t, kb, slot):
            v = v_ref[kb]
            for c in (0, 1):
                acc_ref[t, c] += _dot(v, p_ref[slot, c])

        stage = 0
        pending = None
        for t in range(nqt):
            qs = split_q(t)
            l = [zer, zer]
            for kb in range(nkb):
                l = scores(qs, kb, l, stage % 2)
                if pending is not None:
                    values(*pending)
                    if pending[1] == nkb - 1:
                        finish(pending[0], *l_done)
                pending = (t, kb, stage % 2)
                stage += 1
            l_done = l
        values(*pending)
        finish(pending[0], *l_done)

    @pl.when(unshifted_ref[0] == 0)
    def _():
        def one_map(k, v, qc, m, l, t, c):
            s = _dot(k, qc)
            m_new = jnp.maximum(m, jnp.max(s, axis=0, keepdims=True))
            alpha = jnp.exp2(m - m_new)
            p = jnp.exp2(s - m_new)
            acc_ref[t, c] = alpha * acc_ref[t, c] + _dot(v, p.astype(BF16))
            return m_new, alpha * l + jnp.sum(p, axis=0, keepdims=True)

        neg = jnp.full((1, tq), _NEG_BIG, F32)
        for t in range(nqt):
            q1, q2 = split_q(t)

            def body(kb, carry):
                k, v = load_kv(kb)
                m1, l1 = one_map(k, v, q1, carry[0], carry[1], t, 0)
                m2, l2 = one_map(k, v, q2, carry[2], carry[3], t, 1)
                return m1, l1, m2, l2

            _, l1, _, l2 = lax.fori_loop(0, nkb, body, (neg, zer, neg, zer))
            finish(t, l1, l2)


def _store_row_tiles(ref, x):
    rows = x.shape[0]
    for j in range(ROW_SUBLANES):
        ref[pl.ds(j, rows, stride=ROW_SUBLANES), :] = x[:, j * LANES:(j + 1) * LANES]


def _load_row_tiles(ref, base, rows):
    return jnp.concatenate(
        [ref[pl.ds(base + j, rows, stride=ROW_SUBLANES), :] for j in range(ROW_SUBLANES)], axis=1)


def _attn_out_router_kernel(o_ref, x_ref, w_ref, g_ref, rt_ref, x2_ref, h2_ref, idx_ref, gate_ref):
    tn = (((0,), (0,)), ((), ()))
    y = lax.dot_general(o_ref[0], w_ref[...], tn, preferred_element_type=F32)
    x2 = x_ref[...] + y
    x2_ref[...] = x2
    h2 = _rms_rows(x2, g_ref[...])
    _store_row_tiles(h2_ref, h2)
    ne = rt_ref.shape[0]
    nt = (((1,), (1,)), ((), ()))
    r = rt_ref[...]
    r_hi = r.astype(BF16)
    r_lo = (r - r_hi.astype(F32)).astype(BF16)
    h_hi = h2.astype(BF16)
    h_lo = (h2 - h_hi.astype(F32)).astype(BF16)
    both = lax.dot_general(jnp.concatenate([r_hi, r_lo], axis=0), h_hi, nt, preferred_element_type=F32)
    logits = (both[0:ne] + both[ne:2 * ne]
              + lax.dot_general(r_hi, h_lo, nt, preferred_element_type=F32))
    eid = lax.broadcasted_iota(jnp.int32, logits.shape, 0)
    m1 = jnp.max(logits, axis=0, keepdims=True)
    i1 = jnp.min(jnp.where(logits == m1, eid, ne), axis=0, keepdims=True)
    rest = jnp.where(eid == i1, -jnp.inf, logits)
    m2 = jnp.max(rest, axis=0, keepdims=True)
    i2 = jnp.min(jnp.where(rest == m2, eid, ne), axis=0, keepdims=True)
    e2 = jnp.exp(m2 - m1)
    den = 1.0 + e2
    idx_ref[0:1, :] = i1
    idx_ref[1:2, :] = i2
    gate_ref[0:1, :] = 1.0 / den
    gate_ref[1:2, :] = e2 / den


def _diff_attention_and_router(x, positions, g_mix, w_in, q_norm, k_norm, lam_params, subln, w_out,
                               g_ffn, router, lambda_init, batch, seq):
    t, d = x.shape
    tm = TOKEN_TILE
    nt = t // tm
    inv_freq = ROPE_THETA ** (-jnp.arange(0, ROT_DIM, 2, dtype=F32) / ROT_DIM)
    q, k, v = pl.pallas_call(
        _qkv_kernel,
        grid=(nt,),
        in_specs=[
            pl.BlockSpec((tm, d), lambda i: (i, 0)),
            _resident((1, d)),
            _resident((d, 3 * d)),
            pl.BlockSpec((1, tm), lambda i: (0, i)),
            _resident((ROT_DIM // 2, 1)),
            _resident((HEAD_DIM, 1)),
            _resident((HEAD_DIM, 1)),
        ],
        out_specs=[
            pl.BlockSpec((1, d, tm), lambda i: (i, 0, 0)),
            pl.BlockSpec((tm, d), lambda i: (i, 0)),
            pl.BlockSpec((1, d, tm), lambda i: (i, 0, 0)),
        ],
        out_shape=[
            jax.ShapeDtypeStruct((nt, d, tm), BF16),
            jax.ShapeDtypeStruct((t, d), BF16),
            jax.ShapeDtypeStruct((nt, d, tm), BF16),
        ],
        scratch_shapes=[pltpu.VMEM((3 * d, d), BF16)],
        compiler_params=_cparams("arbitrary"),
        name="attn_qkv",
    )(x, g_mix, w_in, positions.reshape(1, t), inv_freq.reshape(-1, 1),
      q_norm.reshape(-1, 1), k_norm.reshape(-1, 1))

    nq = seq // tm
    qg = ATTN_Q_TILES
    ngrp = nq // qg
    score_bound = (HEAD_DIM ** 0.5 * LOG2E * _BF16_ROUNDING_MARGIN
                   * jnp.max(jnp.abs(q_norm)) * jnp.max(jnp.abs(k_norm)))
    unshifted = (score_bound <= _EXP2_SAFE_LOG2).astype(jnp.int32).reshape(1)
    o = pl.pallas_call(
        functools.partial(_attn_kernel, lambda_init),
        grid_spec=pltpu.PrefetchScalarGridSpec(
            num_scalar_prefetch=1,
            grid=(batch, N_HEADS, ngrp),
            in_specs=[
                pl.BlockSpec((qg, V_DIM, tm), lambda b, h, i, f: (b * ngrp + i, h, 0)),
                pl.BlockSpec((seq, V_DIM), lambda b, h, i, f: (b, h)),
                pl.BlockSpec((nq, V_DIM, tm), lambda b, h, i, f: (b, h, 0)),
                pl.BlockSpec((4, HEAD_DIM), lambda b, h, i, f: (0, 0)),
                pl.BlockSpec((V_DIM, 1), lambda b, h, i, f: (0, 0)),
            ],
            out_specs=pl.BlockSpec((qg, V_DIM, tm), lambda b, h, i, f: (b * ngrp + i, h, 0)),
            scratch_shapes=[pltpu.VMEM((qg, 2, V_DIM, tm), F32),
                            pltpu.VMEM((2, 2, tm, tm), BF16)],
        ),
        out_shape=jax.ShapeDtypeStruct((nt, d, tm), BF16),
        compiler_params=_cparams("parallel", "parallel", "arbitrary"),
        name="diff_attn",
    )(unshifted, q, k, v, lam_params, subln.reshape(-1, 1))

    x2, h2, idx, gates = pl.pallas_call(
        _attn_out_router_kernel,
        grid=(nt,),
        in_specs=[
            pl.BlockSpec((1, d, tm), lambda i: (i, 0, 0)),
            pl.BlockSpec((tm, d), lambda i: (i, 0)),
            _resident((d, d)),
            _resident((1, d)),
            _resident((N_EXPERTS, d)),
        ],
        out_specs=[
            pl.BlockSpec((tm, d), lambda i: (i, 0)),
            pl.BlockSpec((tm * ROW_SUBLANES, LANES), lambda i: (i, 0)),
            pl.BlockSpec((2, tm), lambda i: (0, i)),
            pl.BlockSpec((2, tm), lambda i: (0, i)),
        ],
        out_shape=[
            jax.ShapeDtypeStruct((t, d), F32),
            jax.ShapeDtypeStruct((t * ROW_SUBLANES, LANES), F32),
            jax.ShapeDtypeStruct((2, t), jnp.int32),
            jax.ShapeDtypeStruct((2, t), F32),
        ],
        compiler_params=_cparams("parallel"),
        name="attn_out_router",
    )(o, x, w_out, g_ffn, router.T)
    return x2, h2, idx, gates.T


def _routing_plan(idx, n_tiles):
    experts = jnp.arange(N_EXPERTS, dtype=jnp.int32)[:, None]
    chosen = [(idx[c][None, :] == experts).astype(jnp.int32) for c in (0, 1)]
    hot = chosen[0] + chosen[1]
    incl = jnp.cumsum(hot, axis=1)
    rank = incl - hot
    tiles = (incl[:, -1] + MOE_TILE - 1) // MOE_TILE
    tile_end = jnp.cumsum(tiles)
    row_start = (tile_end - tiles) * MOE_TILE
    slot = row_start[:, None] + rank
    pos = jnp.stack([jnp.sum(slot * one_hot, axis=0) for one_hot in chosen])
    tile_id = jnp.arange(n_tiles, dtype=jnp.int32)
    tile_expert = jnp.minimum(jnp.sum(tile_id[:, None] >= tile_end[None, :], axis=1), N_EXPERTS - 1)
    new_expert = jnp.concatenate([jnp.ones((1,), bool), tile_expert[1:] != tile_expert[:-1]])
    tile_state = jnp.where(tile_id < tile_end[-1], 1 + new_expert.astype(jnp.int32), 0)
    last_tile = jnp.where(tiles > 0, tile_end - 1, -1)
    tail = tile_end[-1] + jnp.arange(N_EXPERTS, dtype=jnp.int32)
    pad_tiles = jnp.concatenate([last_tile, jnp.where(tail < n_tiles, tail, -1)])
    return (pos.astype(jnp.int32), tile_expert.astype(jnp.int32), tile_state.astype(jnp.int32),
            pad_tiles.astype(jnp.int32))


def _row_tile(ref, row):
    return ref.at[pl.ds(pl.multiple_of(row * ROW_SUBLANES, ROW_SUBLANES), ROW_SUBLANES)]


def _dispatch_kernel(pad_ref, pos_ref, h_hbm, xs_hbm, zero_ref, ring, load_sem, row_sem):
    i = pl.program_id(0)
    last = pl.num_programs(0) - 1
    n = pos_ref.shape[2] // 2
    mt = zero_ref.shape[0]
    nslot = ring.shape[0]
    block_rows = n * ROW_SUBLANES

    def load(step):
        src = h_hbm.at[pl.ds(pl.multiple_of(step * block_rows, block_rows), block_rows)]
        return pltpu.make_async_copy(src, ring.at[step % nslot], load_sem.at[step % nslot])

    def wait_rows(step):
        for _ in (0, 1):
            pltpu.make_async_copy(ring.at[0], xs_hbm.at[pl.ds(0, block_rows)],
                                  row_sem.at[step % nslot]).wait()

    @pl.when(i == 0)
    def _():
        load(0).start()
        zero_ref[...] = jnp.zeros_like(zero_ref)
        fills = [pltpu.make_async_copy(
            zero_ref, xs_hbm.at[pl.ds(pl.multiple_of(jnp.maximum(pad_ref[j], 0) * mt, mt), mt)],
            row_sem.at[0]) for j in range(pad_ref.shape[0])]
        for j, fill in enumerate(fills):
            pl.when(pad_ref[j] >= 0)(fill.start)
        for j, fill in enumerate(fills):
            pl.when(pad_ref[j] >= 0)(fill.wait)

    pl.when(i < last)(lambda: load(i + 1).start())
    load(i).wait()
    block = ring.at[i % nslot]

    def issue(r, carry):
        for c in (0, 1):
            pltpu.make_async_copy(_row_tile(block, r), _row_tile(xs_hbm, pos_ref[0, 0, c * n + r]),
                                  row_sem.at[i % nslot]).start(priority=c)
        return carry

    lax.fori_loop(0, n, issue, 0, unroll=8)
    pl.when(i > 0)(lambda: wait_rows(i - 1))
    pl.when(i == last)(lambda: wait_rows(i))


def _expert_kernel(te_ref, ts_ref, xs_ref, w1_hbm, w3_hbm, w2_hbm, y_ref, w1_ref, w3_ref, w2_ref, sem):
    i = pl.program_id(0)
    mt = xs_ref.shape[0] // ROW_SUBLANES
    chunks = _ff_chunks(w1_ref.shape[1])

    def swiglu_tile(before_chunk):
        x = _load_row_tiles(xs_ref, 0, mt)
        acc = None
        for c, (f0, fl) in enumerate(chunks):
            before_chunk(c)
            a = _dot(x, w1_ref[:, f0:f0 + fl])
            b = _dot(x, w3_ref[:, f0:f0 + fl])
            act = _silu(a) * b
            part = _dot(act, w2_ref[f0:f0 + fl, :])
            acc = part if acc is None else acc + part
        _store_row_tiles(y_ref, acc)

    @pl.when(ts_ref[i] == 2)
    def _():
        e = te_ref[i]
        fetch = []
        for c, (f0, fl) in enumerate(chunks):
            cols = pl.ds(f0, fl)
            fetch.append([
                pltpu.make_async_copy(w1_hbm.at[e, :, cols], w1_ref.at[:, cols], sem.at[c]),
                pltpu.make_async_copy(w3_hbm.at[e, :, cols], w3_ref.at[:, cols], sem.at[c]),
                pltpu.make_async_copy(w2_hbm.at[e, cols, :], w2_ref.at[cols, :], sem.at[c]),
            ])
        for copies in fetch:
            for copy in copies:
                copy.start()

        def wait_chunk(c):
            for copy in fetch[c]:
                copy.wait()

        swiglu_tile(wait_chunk)

    @pl.when(ts_ref[i] == 1)
    def _():
        swiglu_tile(lambda c: None)

    @pl.when(ts_ref[i] == 0)
    def _():
        y_ref[...] = jnp.zeros_like(y_ref)


def _combine_kernel(pos_ref, pos_next_ref, x_ref, g_ref, y_hbm, o_ref, buf, sem):
    i = pl.program_id(0)
    n = x_ref.shape[0]

    def gather(p_ref, slot):
        def issue(r, carry):
            for c in (0, 1):
                pltpu.make_async_copy(_row_tile(y_hbm, p_ref[0, 0, c * n + r]),
                                      _row_tile(buf, (slot * 2 + c) * n + r),
                                      sem.at[slot]).start(priority=c)
            return carry
        lax.fori_loop(0, n, issue, 0, unroll=8)

    @pl.when(i == 0)
    def _():
        gather(pos_ref, 0)

    @pl.when(i + 1 < pl.num_programs(0))
    def _():
        gather(pos_next_ref, (i + 1) % 2)

    slot = i % 2
    chunk = n * ROW_SUBLANES
    base = pl.multiple_of(slot * 2 * chunk, chunk)
    pltpu.make_async_copy(y_hbm.at[pl.ds(0, 2 * chunk)], buf.at[pl.ds(base, 2 * chunk)],
                          sem.at[slot]).wait()
    g = g_ref[...]
    o_ref[...] = (x_ref[...] + g[:, 0:1] * _load_row_tiles(buf, base, n)
                  + g[:, 1:2] * _load_row_tiles(buf, base + chunk, n))


def _moe(x2, h2, idx, gates, w1, w3, w2):
    t, d = x2.shape
    fe = w1.shape[2]
    assert d == ROW_SUBLANES * LANES
    rs = ROW_SUBLANES
    n_tiles = (2 * t) // MOE_TILE + N_EXPERTS
    rows = n_tiles * MOE_TILE
    pos, tile_expert, tile_state, pad_tiles = _routing_plan(idx, n_tiles)
    gt = GATHER_TILE
    ng = t // gt
    pos_blocks = pos.reshape(2, ng, gt).transpose(1, 0, 2).reshape(ng, 1, 2 * gt)

    xs = pl.pallas_call(
        _dispatch_kernel,
        grid_spec=pltpu.PrefetchScalarGridSpec(
            num_scalar_prefetch=1,
            grid=(ng,),
            in_specs=[
                pl.BlockSpec((1, 1, 2 * gt), lambda i, pad: (i, 0, 0), memory_space=pltpu.SMEM),
                pl.BlockSpec(memory_space=pl.ANY),
            ],
            out_specs=pl.BlockSpec(memory_space=pl.ANY),
            scratch_shapes=[pltpu.VMEM((MOE_TILE * rs, LANES), F32),
                            pltpu.VMEM((DISPATCH_RING, gt * rs, LANES), F32),
                            pltpu.SemaphoreType.DMA((DISPATCH_RING,)),
                            pltpu.SemaphoreType.DMA((DISPATCH_RING,))],
        ),
        out_shape=jax.ShapeDtypeStruct((rows * rs, LANES), F32),
        compiler_params=_cparams("arbitrary"),
        name="moe_dispatch",
    )(pad_tiles, pos_blocks, h2)

    y = pl.pallas_call(
        _expert_kernel,
        grid_spec=pltpu.PrefetchScalarGridSpec(
            num_scalar_prefetch=2,
            grid=(n_tiles,),
            in_specs=[
                pl.BlockSpec((MOE_TILE * rs, LANES), lambda i, te, ts: (i, 0)),
                pl.BlockSpec(memory_space=pl.ANY),
                pl.BlockSpec(memory_space=pl.ANY),
                pl.BlockSpec(memory_space=pl.ANY),
            ],
            out_specs=pl.BlockSpec((MOE_TILE * rs, LANES), lambda i, te, ts: (i, 0)),
            scratch_shapes=[pltpu.VMEM((d, fe), F32), pltpu.VMEM((d, fe), F32),
                            pltpu.VMEM((fe, d), F32),
                            pltpu.SemaphoreType.DMA((len(_ff_chunks(fe)),))],
        ),
        out_shape=jax.ShapeDtypeStruct((rows * rs, LANES), F32),
        compiler_params=_cparams("arbitrary", vmem_limit=EXPERT_VMEM_LIMIT),
        name="moe_experts",
    )(tile_expert, tile_state, xs, w1, w3, w2)

    return pl.pallas_call(
        _combine_kernel,
        grid=(ng,),
        in_specs=[
            pl.BlockSpec((1, 1, 2 * gt), lambda i: (i, 0, 0), memory_space=pltpu.SMEM),
            pl.BlockSpec((1, 1, 2 * gt), lambda i: (jnp.minimum(i + 1, ng - 1), 0, 0),
                         memory_space=pltpu.SMEM),
            pl.BlockSpec((gt, d), lambda i: (i, 0)),
            pl.BlockSpec((gt, 2), lambda i: (i, 0)),
            pl.BlockSpec(memory_space=pl.ANY),
        ],
        out_specs=pl.BlockSpec((gt, d), lambda i: (i, 0)),
        out_shape=jax.ShapeDtypeStruct((t, d), F32),
        scratch_shapes=[pltpu.VMEM((2 * 2 * gt * rs, LANES), F32), pltpu.SemaphoreType.DMA((2,))],
        compiler_params=_cparams("arbitrary"),
        name="moe_combine",
    )(pos_blocks, pos_blocks, x2, gates, y)


def kernel(x, positions, norm_mix, norm_ffn, conv_in, conv_w, conv_out, attn_in, q_norm, k_norm,
           lam_q1, lam_k1, lam_q2, lam_k2, subln, attn_out, ffn_w1, ffn_w3, ffn_w2,
           router, moe_w1, moe_w3, moe_w2):
    batch, seq, d = x.shape
    xt = x.reshape(batch * seq, d)
    bf = lambda w: w.astype(BF16)

    xt = _layer0(xt, norm_mix[0:1], conv_in[0], conv_w[0], bf(conv_out[0]),
                 norm_ffn[0:1], bf(ffn_w1[0]), bf(ffn_w3[0]), bf(ffn_w2[0]), seq)

    lambda_init = 0.8 - 0.6 * math.exp(-0.3 * 1)
    lam_params = jnp.concatenate([lam_q1, lam_k1, lam_q2, lam_k2], axis=0)
    x2, h2, idx, gates = _diff_attention_and_router(
        xt, positions, norm_mix[1:2], attn_in[0], q_norm[0], k_norm[0], lam_params, subln[0],
        bf(attn_out[0]), norm_ffn[1:2], router[0], lambda_init, batch, seq)
    out = _moe(x2, h2, idx, gates, moe_w1[0], moe_w3[0], moe_w2[0])
    return out.reshape(batch, seq, d)
```

```python
import functools
import math

import jax
import jax.numpy as jnp
from jax import lax
from jax.experimental import pallas as pl
from jax.experimental.pallas import tpu as pltpu

F32 = jnp.float32
BF16 = jnp.bfloat16

D_MODEL = 1024
N_HEADS = 8
HEAD_DIM = 64
V_DIM = 2 * HEAD_DIM
ROT_DIM = HEAD_DIM // 4
ROPE_THETA = 500000.0
N_EXPERTS = 8
RMS_EPS = 1e-6
LOG2E = 1.4426950408889634
LANES = 128
ROW_SUBLANES = 8

TOKEN_TILE = 512
ATTN_KV_TILE = 256
ATTN_Q_TILES = 2
MOE_TILE = 512
FF_CHUNK = 512
GATHER_TILE = 512
DISPATCH_RING = 3
VMEM_LIMIT = 56 * 1024 * 1024
EXPERT_VMEM_LIMIT = 62 * 1024 * 1024

_NEG_BIG = -1e30
_EXP2_SAFE_LOG2 = 60.0
_BF16_ROUNDING_MARGIN = 1.02


def _cparams(*sem, vmem_limit=VMEM_LIMIT):
    return pltpu.CompilerParams(dimension_semantics=sem, vmem_limit_bytes=vmem_limit)


def _rms_rows(x, g):
    ms = jnp.mean(x * x, axis=-1, keepdims=True)
    return x * lax.rsqrt(ms + RMS_EPS) * g


def _dot(a, b):
    return jnp.dot(a, b, preferred_element_type=F32)


def _silu(a):
    return a * (1.0 / (1.0 + jnp.exp(-a)))


def _ff_chunks(width):
    out, f0 = [], 0
    while f0 < width:
        fl = min(FF_CHUNK, width - f0)
        out.append((f0, fl))
        f0 += fl
    return out


def _conv_in_kernel(x_ref, g_ref, w_ref, b_ref, u_ref):
    d = x_ref.shape[1]
    h = _rms_rows(x_ref[...], g_ref[...])
    b_ref[...] = _dot(h, w_ref[:, 0:d]).astype(BF16)
    c = _dot(h, w_ref[:, d:2 * d])
    v = _dot(h, w_ref[:, 2 * d:3 * d])
    u_ref[...] = (c * v).astype(BF16)


def _conv_out_ffn_kernel(tiles_per_seq, u_ref, up_ref, un_ref, b_ref, x_ref, cw_ref, w_ref,
                         g_ref, w1_ref, w3_ref, w2_ref, o_ref):
    i = pl.program_id(0)
    tm = u_ref.shape[0]
    halo = up_ref.shape[0]
    u = u_ref[...].astype(F32)
    first = (i % tiles_per_seq) == 0
    last = (i % tiles_per_seq) == tiles_per_seq - 1
    prev_row = jnp.where(first, 0.0, up_ref[halo - 1:halo, :].astype(F32))
    next_row = jnp.where(last, 0.0, un_ref[0:1, :].astype(F32))
    rows = lax.broadcasted_iota(jnp.int32, (tm, 1), 0)
    u_m1 = jnp.where(rows == 0, prev_row, pltpu.roll(u, 1, 0))
    u_p1 = jnp.where(rows == tm - 1, next_row, pltpu.roll(u, tm - 1, 0))
    cw = cw_ref[...]
    conv = u_m1 * cw[0:1, :] + u * cw[1:2, :] + u_p1 * cw[2:3, :]
    y = (b_ref[...].astype(F32) * conv).astype(BF16)
    x1 = x_ref[...] + _dot(y, w_ref[...])
    h = _rms_rows(x1, g_ref[...])
    acc = x1
    for f0, fl in _ff_chunks(w1_ref.shape[1]):
        a = _dot(h, w1_ref[:, f0:f0 + fl])
        b = _dot(h, w3_ref[:, f0:f0 + fl])
        acc = acc + _dot(_silu(a) * b, w2_ref[f0:f0 + fl, :])
    o_ref[...] = acc


def _resident(shape):
    return pl.BlockSpec(shape, lambda i: (0,) * len(shape), pipeline_mode=pl.Buffered(1))


def _layer0(x, g_mix, w_in, conv_w, w_out, g_ffn, w1, w3, w2, seq):
    t, d = x.shape
    f = w1.shape[1]
    tm = TOKEN_TILE
    nt = t // tm
    halo = 16
    b, u = pl.pallas_call(
        _conv_in_kernel,
        grid=(nt,),
        in_specs=[
            pl.BlockSpec((tm, d), lambda i: (i, 0)),
            _resident((1, d)),
            _resident((d, 3 * d)),
        ],
        out_specs=[pl.BlockSpec((tm, d), lambda i: (i, 0))] * 2,
        out_shape=[jax.ShapeDtypeStruct((t, d), BF16)] * 2,
        compiler_params=_cparams("parallel"),
        name="conv_in",
    )(x, g_mix, w_in)
    hb = tm // halo
    nhb = t // halo
    return pl.pallas_call(
        functools.partial(_conv_out_ffn_kernel, seq // tm),
        grid=(nt,),
        in_specs=[
            pl.BlockSpec((tm, d), lambda i: (i, 0)),
            pl.BlockSpec((halo, d), lambda i: (jnp.maximum(i * hb - 1, 0), 0)),
            pl.BlockSpec((halo, d), lambda i: (jnp.minimum((i + 1) * hb, nhb - 1), 0)),
            pl.BlockSpec((tm, d), lambda i: (i, 0)),
            pl.BlockSpec((tm, d), lambda i: (i, 0)),
            _resident((3, d)),
            _resident((d, d)),
            _resident((1, d)),
            _resident((d, f)),
            _resident((d, f)),
            _resident((f, d)),
        ],
        out_specs=pl.BlockSpec((tm, d), lambda i: (i, 0)),
        out_shape=jax.ShapeDtypeStruct((t, d), F32),
        compiler_params=_cparams("parallel"),
        name="conv_out_ffn",
    )(u, u, u, b, x, conv_w, w_out, g_ffn, w1, w3, w2)


def _qkv_kernel(x_ref, g_ref, w_ref, pos_ref, freq_ref, qn_ref, kn_ref, q_ref, k_ref, v_ref, wt_ref):
    tm, d = x_ref.shape

    @pl.when(pl.program_id(0) == 0)
    def _():
        for j in range(3):
            wt_ref[j * d:(j + 1) * d, :] = w_ref[:, j * d:(j + 1) * d].T.astype(BF16)

    h = _rms_rows(x_ref[...], g_ref[...]).astype(BF16)
    nt = (((1,), (1,)), ((), ()))
    ang = pos_ref[...].astype(F32) * freq_ref[...]
    cos = jnp.cos(ang)[None]
    sin = jnp.sin(ang)[None]
    half = ROT_DIM // 2
    groups = d // HEAD_DIM

    def norm_rope(w_rows, gain, scale):
        t = lax.dot_general(w_rows, h, nt, preferred_element_type=F32)
        t = t.reshape(groups, HEAD_DIM, tm)
        ms = jnp.mean(t * t, axis=1, keepdims=True)
        t = t * lax.rsqrt(ms + RMS_EPS) * gain[None]
        t1 = t[:, 0:half, :]
        t2 = t[:, half:ROT_DIM, :]
        t = jnp.concatenate([t1 * cos - t2 * sin, t2 * cos + t1 * sin, t[:, ROT_DIM:, :]], axis=1)
        return (t * scale).reshape(d, tm)

    q = norm_rope(wt_ref[0:d, :], qn_ref[...], (HEAD_DIM ** -0.5) * LOG2E)
    q_ref[0] = q.astype(BF16)
    k = norm_rope(wt_ref[d:2 * d, :], kn_ref[...], 1.0)
    k_ref[...] = k.T.astype(BF16)
    v = lax.dot_general(wt_ref[2 * d:3 * d, :], h, nt, preferred_element_type=F32)
    v_ref[0] = v.astype(BF16)


def _attn_kernel(lambda_init, unshifted_ref, q_ref, k_ref, v_ref, lam_ref, sub_ref, o_ref,
                 acc_ref, p_ref):
    nqt, _, tq = q_ref.shape
    nkb = v_ref.shape[0]
    tk = v_ref.shape[2]
    rows = lax.broadcasted_iota(jnp.int32, (V_DIM, 1), 0)
    acc_ref[...] = jnp.zeros_like(acc_ref)
    zer = jnp.zeros((1, tq), F32)

    def split_q(t):
        q = q_ref[t]
        zero = jnp.zeros_like(q)
        return jnp.where(rows < HEAD_DIM, q, zero), jnp.where(rows >= HEAD_DIM, q, zero)

    def load_kv(kb):
        k = k_ref[pl.ds(pl.multiple_of(kb * tk, tk), tk), :]
        return k, v_ref[kb]

    def finish(t, l1, l2):
        lp = lam_ref[...]
        lam = (jnp.exp(jnp.sum(lp[0:1, :] * lp[1:2, :], axis=-1, keepdims=True))
               - jnp.exp(jnp.sum(lp[2:3, :] * lp[3:4, :], axis=-1, keepdims=True))
               + lambda_init)
        o = acc_ref[t, 0] / l1 - lam * (acc_ref[t, 1] / l2)
        ms = jnp.mean(o * o, axis=0, keepdims=True)
        o = o * lax.rsqrt(ms + RMS_EPS) * sub_ref[...] * (1.0 - lambda_init)
        o_ref[t] = o.astype(BF16)

    @pl.when(unshifted_ref[0] == 1)
    def _():
        ts = p_ref.shape[2]
        per = tk // ts
        nks = nkb * per

        def scores(qs, j, l, slot):
            k = k_ref[j * ts:(j + 1) * ts, :]
            ps = [jnp.exp2(_dot(k, qc)) for qc in qs]
            for c, p in enumerate(ps):
                p_ref[slot, c] = p.astype(BF16)
            return [lc + jnp.sum(p, axis=0, keepdims=True) for lc, p in zip(l, ps)]

        def values(t, j, slot):
            v = v_ref[j // per][:, (j % per) * ts:(j % per + 1) * ts]
            for c in (0, 1):
                acc_ref[t, c] += _dot(v, p_ref[slot, c])

        stage = 0
        pending = None
        for t in range(nqt):
            qs = split_q(t)
            l = [zer, zer]
            for j in range(nks):
                l = scores(qs, j, l, stage % 2)
                if pending is not None:
                    values(*pending)
                    if pending[1] == nks - 1:
                        finish(pending[0], *l_done)
                pending = (t, j, stage % 2)
                stage += 1
            l_done = l
        values(*pending)
        finish(pending[0], *l_done)

    @pl.when(unshifted_ref[0] == 0)
    def _():
        def one_map(k, v, qc, m, l, t, c):
            s = _dot(k, qc)
            m_new = jnp.maximum(m, jnp.max(s, axis=0, keepdims=True))
            alpha = jnp.exp2(m - m_new)
            p = jnp.exp2(s - m_new)
            acc_ref[t, c] = alpha * acc_ref[t, c] + _dot(v, p.astype(BF16))
            return m_new, alpha * l + jnp.sum(p, axis=0, keepdims=True)

        neg = jnp.full((1, tq), _NEG_BIG, F32)
        for t in range(nqt):
            q1, q2 = split_q(t)

            def body(kb, carry):
                k, v = load_kv(kb)
                m1, l1 = one_map(k, v, q1, carry[0], carry[1], t, 0)
                m2, l2 = one_map(k, v, q2, carry[2], carry[3], t, 1)
                return m1, l1, m2, l2

            _, l1, _, l2 = lax.fori_loop(0, nkb, body, (neg, zer, neg, zer))
            finish(t, l1, l2)


def _store_row_tiles(ref, x):
    rows = x.shape[0]
    for j in range(ROW_SUBLANES):
        ref[pl.ds(j, rows, stride=ROW_SUBLANES), :] = x[:, j * LANES:(j + 1) * LANES]


def _load_row_tiles(ref, base, rows):
    return jnp.concatenate(
        [ref[pl.ds(base + j, rows, stride=ROW_SUBLANES), :] for j in range(ROW_SUBLANES)], axis=1)


def _attn_out_router_kernel(o_ref, x_ref, w_ref, g_ref, rt_ref, x2_ref, h2_ref, idx_ref, gate_ref):
    tn = (((0,), (0,)), ((), ()))
    y = lax.dot_general(o_ref[0], w_ref[...], tn, preferred_element_type=F32)
    x2 = x_ref[...] + y
    x2_ref[...] = x2
    h2 = _rms_rows(x2, g_ref[...])
    _store_row_tiles(h2_ref, h2)
    ne = rt_ref.shape[0]
    nt = (((1,), (1,)), ((), ()))
    r = rt_ref[...]
    r_hi = r.astype(BF16)
    r_lo = (r - r_hi.astype(F32)).astype(BF16)
    h_hi = h2.astype(BF16)
    h_lo = (h2 - h_hi.astype(F32)).astype(BF16)
    both = lax.dot_general(jnp.concatenate([r_hi, r_lo], axis=0), h_hi, nt, preferred_element_type=F32)
    logits = (both[0:ne] + both[ne:2 * ne]
              + lax.dot_general(r_hi, h_lo, nt, preferred_element_type=F32))
    eid = lax.broadcasted_iota(jnp.int32, logits.shape, 0)
    m1 = jnp.max(logits, axis=0, keepdims=True)
    i1 = jnp.min(jnp.where(logits == m1, eid, ne), axis=0, keepdims=True)
    rest = jnp.where(eid == i1, -jnp.inf, logits)
    m2 = jnp.max(rest, axis=0, keepdims=True)
    i2 = jnp.min(jnp.where(rest == m2, eid, ne), axis=0, keepdims=True)
    e2 = jnp.exp(m2 - m1)
    den = 1.0 + e2
    idx_ref[0:1, :] = i1
    idx_ref[1:2, :] = i2
    gate_ref[0:1, :] = 1.0 / den
    gate_ref[1:2, :] = e2 / den


def _diff_attention_and_router(x, positions, g_mix, w_in, q_norm, k_norm, lam_params, subln, w_out,
                               g_ffn, router, lambda_init, batch, seq):
    t, d = x.shape
    tm = TOKEN_TILE
    nt = t // tm
    inv_freq = ROPE_THETA ** (-jnp.arange(0, ROT_DIM, 2, dtype=F32) / ROT_DIM)
    q, k, v = pl.pallas_call(
        _qkv_kernel,
        grid=(nt,),
        in_specs=[
            pl.BlockSpec((tm, d), lambda i: (i, 0)),
            _resident((1, d)),
            _resident((d, 3 * d)),
            pl.BlockSpec((1, tm), lambda i: (0, i)),
            _resident((ROT_DIM // 2, 1)),
            _resident((HEAD_DIM, 1)),
            _resident((HEAD_DIM, 1)),
        ],
        out_specs=[
            pl.BlockSpec((1, d, tm), lambda i: (i, 0, 0)),
            pl.BlockSpec((tm, d), lambda i: (i, 0)),
            pl.BlockSpec((1, d, tm), lambda i: (i, 0, 0)),
        ],
        out_shape=[
            jax.ShapeDtypeStruct((nt, d, tm), BF16),
            jax.ShapeDtypeStruct((t, d), BF16),
            jax.ShapeDtypeStruct((nt, d, tm), BF16),
        ],
        scratch_shapes=[pltpu.VMEM((3 * d, d), BF16)],
        compiler_params=_cparams("arbitrary"),
        name="attn_qkv",
    )(x, g_mix, w_in, positions.reshape(1, t), inv_freq.reshape(-1, 1),
      q_norm.reshape(-1, 1), k_norm.reshape(-1, 1))

    nq = seq // tm
    qg = ATTN_Q_TILES
    ngrp = nq // qg
    score_bound = (HEAD_DIM ** 0.5 * LOG2E * _BF16_ROUNDING_MARGIN
                   * jnp.max(jnp.abs(q_norm)) * jnp.max(jnp.abs(k_norm)))
    unshifted = (score_bound <= _EXP2_SAFE_LOG2).astype(jnp.int32).reshape(1)
    o = pl.pallas_call(
        functools.partial(_attn_kernel, lambda_init),
        grid_spec=pltpu.PrefetchScalarGridSpec(
            num_scalar_prefetch=1,
            grid=(batch, N_HEADS, ngrp),
            in_specs=[
                pl.BlockSpec((qg, V_DIM, tm), lambda b, h, i, f: (b * ngrp + i, h, 0)),
                pl.BlockSpec((seq, V_DIM), lambda b, h, i, f: (b, h)),
                pl.BlockSpec((nq, V_DIM, tm), lambda b, h, i, f: (b, h, 0)),
                pl.BlockSpec((4, HEAD_DIM), lambda b, h, i, f: (0, 0)),
                pl.BlockSpec((V_DIM, 1), lambda b, h, i, f: (0, 0)),
            ],
            out_specs=pl.BlockSpec((qg, V_DIM, tm), lambda b, h, i, f: (b * ngrp + i, h, 0)),
            scratch_shapes=[pltpu.VMEM((qg, 2, V_DIM, tm), F32),
                            pltpu.VMEM((2, 2, ATTN_KV_TILE, tm), BF16)],
        ),
        out_shape=jax.ShapeDtypeStruct((nt, d, tm), BF16),
        compiler_params=_cparams("parallel", "parallel", "arbitrary"),
        name="diff_attn",
    )(unshifted, q, k, v, lam_params, subln.reshape(-1, 1))

    x2, h2, idx, gates = pl.pallas_call(
        _attn_out_router_kernel,
        grid=(nt,),
        in_specs=[
            pl.BlockSpec((1, d, tm), lambda i: (i, 0, 0)),
            pl.BlockSpec((tm, d), lambda i: (i, 0)),
            _resident((d, d)),
            _resident((1, d)),
            _resident((N_EXPERTS, d)),
        ],
        out_specs=[
            pl.BlockSpec((tm, d), lambda i: (i, 0)),
            pl.BlockSpec((tm * ROW_SUBLANES, LANES), lambda i: (i, 0)),
            pl.BlockSpec((2, tm), lambda i: (0, i)),
            pl.BlockSpec((2, tm), lambda i: (0, i)),
        ],
        out_shape=[
            jax.ShapeDtypeStruct((t, d), F32),
            jax.ShapeDtypeStruct((t * ROW_SUBLANES, LANES), F32),
            jax.ShapeDtypeStruct((2, t), jnp.int32),
            jax.ShapeDtypeStruct((2, t), F32),
        ],
        compiler_params=_cparams("parallel"),
        name="attn_out_router",
    )(o, x, w_out, g_ffn, router.T)
    return x2, h2, idx, gates.T


def _routing_plan(idx, n_tiles):
    experts = jnp.arange(N_EXPERTS, dtype=jnp.int32)[:, None]
    chosen = [(idx[c][None, :] == experts).astype(jnp.int32) for c in (0, 1)]
    hot = chosen[0] + chosen[1]
    incl = jnp.cumsum(hot, axis=1)
    rank = incl - hot
    tiles = (incl[:, -1] + MOE_TILE - 1) // MOE_TILE
    tile_end = jnp.cumsum(tiles)
    row_start = (tile_end - tiles) * MOE_TILE
    slot = row_start[:, None] + rank
    pos = jnp.stack([jnp.sum(slot * one_hot, axis=0) for one_hot in chosen])
    tile_id = jnp.arange(n_tiles, dtype=jnp.int32)
    tile_expert = jnp.minimum(jnp.sum(tile_id[:, None] >= tile_end[None, :], axis=1), N_EXPERTS - 1)
    new_expert = jnp.concatenate([jnp.ones((1,), bool), tile_expert[1:] != tile_expert[:-1]])
    tile_state = jnp.where(tile_id < tile_end[-1], 1 + new_expert.astype(jnp.int32), 0)
    last_tile = jnp.where(tiles > 0, tile_end - 1, -1)
    tail = tile_end[-1] + jnp.arange(N_EXPERTS, dtype=jnp.int32)
    pad_tiles = jnp.concatenate([last_tile, jnp.where(tail < n_tiles, tail, -1)])
    return (pos.astype(jnp.int32), tile_expert.astype(jnp.int32), tile_state.astype(jnp.int32),
            pad_tiles.astype(jnp.int32))


def _row_tile(ref, row):
    return ref.at[pl.ds(pl.multiple_of(row * ROW_SUBLANES, ROW_SUBLANES), ROW_SUBLANES)]


def _dispatch_kernel(pad_ref, pos_ref, h_hbm, xs_hbm, zero_ref, ring, load_sem, row_sem):
    i = pl.program_id(0)
    last = pl.num_programs(0) - 1
    n = pos_ref.shape[2] // 2
    mt = zero_ref.shape[0]
    nslot = ring.shape[0]
    block_rows = n * ROW_SUBLANES

    def load(step):
        src = h_hbm.at[pl.ds(pl.multiple_of(step * block_rows, block_rows), block_rows)]
        return pltpu.make_async_copy(src, ring.at[step % nslot], load_sem.at[step % nslot])

    def wait_rows(step):
        for _ in (0, 1):
            pltpu.make_async_copy(ring.at[0], xs_hbm.at[pl.ds(0, block_rows)],
                                  row_sem.at[step % nslot]).wait()

    @pl.when(i == 0)
    def _():
        load(0).start()
        zero_ref[...] = jnp.zeros_like(zero_ref)
        fills = [pltpu.make_async_copy(
            zero_ref, xs_hbm.at[pl.ds(pl.multiple_of(jnp.maximum(pad_ref[j], 0) * mt, mt), mt)],
            row_sem.at[0]) for j in range(pad_ref.shape[0])]
        for j, fill in enumerate(fills):
            pl.when(pad_ref[j] >= 0)(fill.start)
        for j, fill in enumerate(fills):
            pl.when(pad_ref[j] >= 0)(fill.wait)

    pl.when(i < last)(lambda: load(i + 1).start())
    load(i).wait()
    block = ring.at[i % nslot]

    def issue(r, carry):
        for c in (0, 1):
            pltpu.make_async_copy(_row_tile(block, r), _row_tile(xs_hbm, pos_ref[0, 0, c * n + r]),
                                  row_sem.at[i % nslot]).start(priority=c)
        return carry

    lax.fori_loop(0, n, issue, 0, unroll=8)
    pl.when(i > 0)(lambda: wait_rows(i - 1))
    pl.when(i == last)(lambda: wait_rows(i))


def _expert_kernel(te_ref, ts_ref, xs_ref, w1_hbm, w3_hbm, w2_hbm, y_ref, w1_ref, w3_ref, w2_ref, sem):
    i = pl.program_id(0)
    mt = xs_ref.shape[0] // ROW_SUBLANES
    chunks = _ff_chunks(w1_ref.shape[1])

    def swiglu_tile(before_chunk):
        x = _load_row_tiles(xs_ref, 0, mt)
        acc = None
        for c, (f0, fl) in enumerate(chunks):
            before_chunk(c)
            a = _dot(x, w1_ref[:, f0:f0 + fl])
            b = _dot(x, w3_ref[:, f0:f0 + fl])
            act = _silu(a) * b
            part = _dot(act, w2_ref[f0:f0 + fl, :])
            acc = part if acc is None else acc + part
        _store_row_tiles(y_ref, acc)

    @pl.when(ts_ref[i] == 2)
    def _():
        e = te_ref[i]
        fetch = []
        for c, (f0, fl) in enumerate(chunks):
            cols = pl.ds(f0, fl)
            fetch.append([
                pltpu.make_async_copy(w1_hbm.at[e, :, cols], w1_ref.at[:, cols], sem.at[c]),
                pltpu.make_async_copy(w3_hbm.at[e, :, cols], w3_ref.at[:, cols], sem.at[c]),
                pltpu.make_async_copy(w2_hbm.at[e, cols, :], w2_ref.at[cols, :], sem.at[c]),
            ])
        for copies in fetch:
            for copy in copies:
                copy.start()

        def wait_chunk(c):
            for copy in fetch[c]:
                copy.wait()

        swiglu_tile(wait_chunk)

    @pl.when(ts_ref[i] == 1)
    def _():
        swiglu_tile(lambda c: None)

    @pl.when(ts_ref[i] == 0)
    def _():
        y_ref[...] = jnp.zeros_like(y_ref)


def _combine_kernel(pos_ref, pos_next_ref, x_ref, g_ref, y_hbm, o_ref, buf, sem):
    i = pl.program_id(0)
    n = x_ref.shape[0]

    def gather(p_ref, slot):
        def issue(r, carry):
            for c in (0, 1):
                pltpu.make_async_copy(_row_tile(y_hbm, p_ref[0, 0, c * n + r]),
                                      _row_tile(buf, (slot * 2 + c) * n + r),
                                      sem.at[slot]).start(priority=c)
            return carry
        lax.fori_loop(0, n, issue, 0, unroll=8)

    @pl.when(i == 0)
    def _():
        gather(pos_ref, 0)

    @pl.when(i + 1 < pl.num_programs(0))
    def _():
        gather(pos_next_ref, (i + 1) % 2)

    slot = i % 2
    chunk = n * ROW_SUBLANES
    base = pl.multiple_of(slot * 2 * chunk, chunk)
    pltpu.make_async_copy(y_hbm.at[pl.ds(0, 2 * chunk)], buf.at[pl.ds(base, 2 * chunk)],
                          sem.at[slot]).wait()
    g = g_ref[...]
    o_ref[...] = (x_ref[...] + g[:, 0:1] * _load_row_tiles(buf, base, n)
                  + g[:, 1:2] * _load_row_tiles(buf, base + chunk, n))


def _moe(x2, h2, idx, gates, w1, w3, w2):
    t, d = x2.shape
    fe = w1.shape[2]
    assert d == ROW_SUBLANES * LANES
    rs = ROW_SUBLANES
    n_tiles = (2 * t) // MOE_TILE + N_EXPERTS
    rows = n_tiles * MOE_TILE
    pos, tile_expert, tile_state, pad_tiles = _routing_plan(idx, n_tiles)
    gt = GATHER_TILE
    ng = t // gt
    pos_blocks = pos.reshape(2, ng, gt).transpose(1, 0, 2).reshape(ng, 1, 2 * gt)

    xs = pl.pallas_call(
        _dispatch_kernel,
        grid_spec=pltpu.PrefetchScalarGridSpec(
            num_scalar_prefetch=1,
            grid=(ng,),
            in_specs=[
                pl.BlockSpec((1, 1, 2 * gt), lambda i, pad: (i, 0, 0), memory_space=pltpu.SMEM),
                pl.BlockSpec(memory_space=pl.ANY),
            ],
            out_specs=pl.BlockSpec(memory_space=pl.ANY),
            scratch_shapes=[pltpu.VMEM((MOE_TILE * rs, LANES), F32),
                            pltpu.VMEM((DISPATCH_RING, gt * rs, LANES), F32),
                            pltpu.SemaphoreType.DMA((DISPATCH_RING,)),
                            pltpu.SemaphoreType.DMA((DISPATCH_RING,))],
        ),
        out_shape=jax.ShapeDtypeStruct((rows * rs, LANES), F32),
        compiler_params=_cparams("arbitrary"),
        name="moe_dispatch",
    )(pad_tiles, pos_blocks, h2)

    y = pl.pallas_call(
        _expert_kernel,
        grid_spec=pltpu.PrefetchScalarGridSpec(
            num_scalar_prefetch=2,
            grid=(n_tiles,),
            in_specs=[
                pl.BlockSpec((MOE_TILE * rs, LANES), lambda i, te, ts: (i, 0)),
                pl.BlockSpec(memory_space=pl.ANY),
                pl.BlockSpec(memory_space=pl.ANY),
                pl.BlockSpec(memory_space=pl.ANY),
            ],
            out_specs=pl.BlockSpec((MOE_TILE * rs, LANES), lambda i, te, ts: (i, 0)),
            scratch_shapes=[pltpu.VMEM((d, fe), F32), pltpu.VMEM((d, fe), F32),
                            pltpu.VMEM((fe, d), F32),
                            pltpu.SemaphoreType.DMA((len(_ff_chunks(fe)),))],
        ),
        out_shape=jax.ShapeDtypeStruct((rows * rs, LANES), F32),
        compiler_params=_cparams("arbitrary", vmem_limit=EXPERT_VMEM_LIMIT),
        name="moe_experts",
    )(tile_expert, tile_state, xs, w1, w3, w2)

    return pl.pallas_call(
        _combine_kernel,
        grid=(ng,),
        in_specs=[
            pl.BlockSpec((1, 1, 2 * gt), lambda i: (i, 0, 0), memory_space=pltpu.SMEM),
            pl.BlockSpec((1, 1, 2 * gt), lambda i: (jnp.minimum(i + 1, ng - 1), 0, 0),
                         memory_space=pltpu.SMEM),
            pl.BlockSpec((gt, d), lambda i: (i, 0)),
            pl.BlockSpec((gt, 2), lambda i: (i, 0)),
            pl.BlockSpec(memory_space=pl.ANY),
        ],
        out_specs=pl.BlockSpec((gt, d), lambda i: (i, 0)),
        out_shape=jax.ShapeDtypeStruct((t, d), F32),
        scratch_shapes=[pltpu.VMEM((2 * 2 * gt * rs, LANES), F32), pltpu.SemaphoreType.DMA((2,))],
        compiler_params=_cparams("arbitrary"),
        name="moe_combine",
    )(pos_blocks, pos_blocks, x2, gates, y)


def kernel(x, positions, norm_mix, norm_ffn, conv_in, conv_w, conv_out, attn_in, q_norm, k_norm,
           lam_q1, lam_k1, lam_q2, lam_k2, subln, attn_out, ffn_w1, ffn_w3, ffn_w2,
           router, moe_w1, moe_w3, moe_w2):
    batch, seq, d = x.shape
    xt = x.reshape(batch * seq, d)
    bf = lambda w: w.astype(BF16)

    xt = _layer0(xt, norm_mix[0:1], conv_in[0], conv_w[0], bf(conv_out[0]),
                 norm_ffn[0:1], ffn_w1[0], ffn_w3[0], ffn_w2[0], seq)

    lambda_init = 0.8 - 0.6 * math.exp(-0.3 * 1)
    lam_params = jnp.concatenate([lam_q1, lam_k1, lam_q2, lam_k2], axis=0)
    x2, h2, idx, gates = _diff_attention_and_router(
        xt, positions, norm_mix[1:2], attn_in[0], q_norm[0], k_norm[0], lam_params, subln[0],
        bf(attn_out[0]), norm_ffn[1:2], router[0], lambda_init, batch, seq)
    out = _moe(x2, h2, idx, gates, moe_w1[0], moe_w3[0], moe_w2[0])
    return out.reshape(batch, seq, d)
```

```python
import functools
import math

import jax
import jax.numpy as jnp
from jax import lax
from jax.experimental import pallas as pl
from jax.experimental.pallas import tpu as pltpu

F32 = jnp.float32
BF16 = jnp.bfloat16

D_MODEL = 1024
N_HEADS = 8
HEAD_DIM = 64
V_DIM = 2 * HEAD_DIM
ROT_DIM = HEAD_DIM // 4
ROPE_THETA = 500000.0
N_EXPERTS = 8
RMS_EPS = 1e-6
LOG2E = 1.4426950408889634
LANES = 128
ROW_SUBLANES = 8

TOKEN_TILE = 512
ATTN_KV_TILE = 256
ATTN_Q_LANES = 512
ATTN_Q_TILES = 2
MOE_TILE = 512
FF_CHUNK = 512
GATHER_TILE = 512
DISPATCH_RING = 3
VMEM_LIMIT = 56 * 1024 * 1024
EXPERT_VMEM_LIMIT = 62 * 1024 * 1024

_NEG_BIG = -1e30
_EXP2_SAFE_LOG2 = 60.0
_BF16_ROUNDING_MARGIN = 1.02


def _cparams(*sem, vmem_limit=VMEM_LIMIT):
    return pltpu.CompilerParams(dimension_semantics=sem, vmem_limit_bytes=vmem_limit)


def _rms_rows(x, g):
    ms = jnp.mean(x * x, axis=-1, keepdims=True)
    return x * lax.rsqrt(ms + RMS_EPS) * g


def _dot(a, b):
    return lax.dot_general(a, b, (((1,), (0,)), ((), ())), preferred_element_type=F32)


def _silu(a):
    return a * (1.0 / (1.0 + jnp.exp(-a)))


def _ff_chunks(width):
    out, f0 = [], 0
    while f0 < width:
        fl = min(FF_CHUNK, width - f0)
        out.append((f0, fl))
        f0 += fl
    return out


def _conv_in_kernel(x_ref, g_ref, w_ref, b_ref, u_ref):
    d = x_ref.shape[1]
    h = _rms_rows(x_ref[...], g_ref[...]).astype(BF16)
    b_ref[...] = _dot(h, w_ref[:, 0:d]).astype(BF16)
    c = _dot(h, w_ref[:, d:2 * d])
    v = _dot(h, w_ref[:, 2 * d:3 * d])
    u_ref[...] = (c * v).astype(BF16)


def _conv_out_ffn_kernel(tiles_per_seq, u_ref, up_ref, un_ref, b_ref, x_ref, cw_ref, w_ref,
                         g_ref, w1_ref, w3_ref, w2_ref, o_ref):
    i = pl.program_id(0)
    tm = u_ref.shape[0]
    halo = up_ref.shape[0]
    u = u_ref[...].astype(F32)
    first = (i % tiles_per_seq) == 0
    last = (i % tiles_per_seq) == tiles_per_seq - 1
    prev_row = jnp.where(first, 0.0, up_ref[halo - 1:halo, :].astype(F32))
    next_row = jnp.where(last, 0.0, un_ref[0:1, :].astype(F32))
    rows = lax.broadcasted_iota(jnp.int32, (tm, 1), 0)
    u_m1 = jnp.where(rows == 0, prev_row, pltpu.roll(u, 1, 0))
    u_p1 = jnp.where(rows == tm - 1, next_row, pltpu.roll(u, tm - 1, 0))
    cw = cw_ref[...]
    conv = u_m1 * cw[0:1, :] + u * cw[1:2, :] + u_p1 * cw[2:3, :]
    y = (b_ref[...].astype(F32) * conv).astype(BF16)
    x1 = x_ref[...] + _dot(y, w_ref[...])
    h = _rms_rows(x1, g_ref[...]).astype(BF16)
    acc = x1
    for f0, fl in _ff_chunks(w1_ref.shape[1]):
        a = _dot(h, w1_ref[:, f0:f0 + fl])
        b = _dot(h, w3_ref[:, f0:f0 + fl])
        acc = acc + _dot((_silu(a) * b).astype(BF16), w2_ref[f0:f0 + fl, :])
    o_ref[...] = acc


def _resident(shape):
    return pl.BlockSpec(shape, lambda i: (0,) * len(shape), pipeline_mode=pl.Buffered(1))


def _layer0(x, g_mix, w_in, conv_w, w_out, g_ffn, w1, w3, w2, seq):
    t, d = x.shape
    f = w1.shape[1]
    tm = TOKEN_TILE
    nt = t // tm
    halo = 16
    b, u = pl.pallas_call(
        _conv_in_kernel,
        grid=(nt,),
        in_specs=[
            pl.BlockSpec((tm, d), lambda i: (i, 0)),
            _resident((1, d)),
            _resident((d, 3 * d)),
        ],
        out_specs=[pl.BlockSpec((tm, d), lambda i: (i, 0))] * 2,
        out_shape=[jax.ShapeDtypeStruct((t, d), BF16)] * 2,
        compiler_params=_cparams("parallel"),
        name="conv_in",
    )(x, g_mix, w_in)
    hb = tm // halo
    nhb = t // halo
    return pl.pallas_call(
        functools.partial(_conv_out_ffn_kernel, seq // tm),
        grid=(nt,),
        in_specs=[
            pl.BlockSpec((tm, d), lambda i: (i, 0)),
            pl.BlockSpec((halo, d), lambda i: (jnp.maximum(i * hb - 1, 0), 0)),
            pl.BlockSpec((halo, d), lambda i: (jnp.minimum((i + 1) * hb, nhb - 1), 0)),
            pl.BlockSpec((tm, d), lambda i: (i, 0)),
            pl.BlockSpec((tm, d), lambda i: (i, 0)),
            _resident((3, d)),
            _resident((d, d)),
            _resident((1, d)),
            _resident((d, f)),
            _resident((d, f)),
            _resident((f, d)),
        ],
        out_specs=pl.BlockSpec((tm, d), lambda i: (i, 0)),
        out_shape=jax.ShapeDtypeStruct((t, d), F32),
        compiler_params=_cparams("parallel"),
        name="conv_out_ffn",
    )(u, u, u, b, x, conv_w, w_out, g_ffn, w1, w3, w2)


def _qkv_kernel(x_ref, g_ref, w_ref, pos_ref, freq_ref, qn_ref, kn_ref, q_ref, k_ref, v_ref, wt_ref):
    tm, d = x_ref.shape

    @pl.when(pl.program_id(0) == 0)
    def _():
        for j in range(3):
            wt_ref[j * d:(j + 1) * d, :] = w_ref[:, j * d:(j + 1) * d].T.astype(BF16)

    h = _rms_rows(x_ref[...], g_ref[...]).astype(BF16)
    nt = (((1,), (1,)), ((), ()))
    ang = pos_ref[...].astype(F32) * freq_ref[...]
    cos = jnp.cos(ang)[None]
    sin = jnp.sin(ang)[None]
    half = ROT_DIM // 2
    groups = d // HEAD_DIM

    def norm_rope(w_rows, gain, scale):
        t = lax.dot_general(w_rows, h, nt, preferred_element_type=F32)
        t = t.reshape(groups, HEAD_DIM, tm)
        ms = jnp.mean(t * t, axis=1, keepdims=True)
        t = t * lax.rsqrt(ms + RMS_EPS) * gain[None]
        t1 = t[:, 0:half, :]
        t2 = t[:, half:ROT_DIM, :]
        t = jnp.concatenate([t1 * cos - t2 * sin, t2 * cos + t1 * sin, t[:, ROT_DIM:, :]], axis=1)
        return (t * scale).reshape(d, tm)

    q = norm_rope(wt_ref[0:d, :], qn_ref[...], (HEAD_DIM ** -0.5) * LOG2E)
    q_ref[0] = q.astype(BF16)
    k = norm_rope(wt_ref[d:2 * d, :], kn_ref[...], 1.0)
    k_ref[...] = k.T.astype(BF16)
    v = lax.dot_general(wt_ref[2 * d:3 * d, :], h, nt, preferred_element_type=F32)
    v_ref[0] = v.astype(BF16)


def _attn_kernel(lambda_init, unshifted_ref, q_ref, k_ref, v_ref, lam_ref, sub_ref, o_ref,
                 acc_ref, p_ref):
    nqt, _, tq = q_ref.shape
    nkb = v_ref.shape[0]
    tk = v_ref.shape[2]
    rows = lax.broadcasted_iota(jnp.int32, (V_DIM, 1), 0)
    acc_ref[...] = jnp.zeros_like(acc_ref)
    zer = jnp.zeros((1, tq), F32)

    def split_q(t):
        q = q_ref[t]
        zero = jnp.zeros_like(q)
        return jnp.where(rows < HEAD_DIM, q, zero), jnp.where(rows >= HEAD_DIM, q, zero)

    def load_kv(kb):
        k = k_ref[pl.ds(pl.multiple_of(kb * tk, tk), tk), :]
        return k, v_ref[kb]

    def finish(t, l1, l2):
        lp = lam_ref[...]
        lam = (jnp.exp(jnp.sum(lp[0:1, :] * lp[1:2, :], axis=-1, keepdims=True))
               - jnp.exp(jnp.sum(lp[2:3, :] * lp[3:4, :], axis=-1, keepdims=True))
               + lambda_init)
        o = acc_ref[t, 0] / l1 - lam * (acc_ref[t, 1] / l2)
        ms = jnp.mean(o * o, axis=0, keepdims=True)
        o = o * lax.rsqrt(ms + RMS_EPS) * sub_ref[...] * (1.0 - lambda_init)
        o_ref[t] = o.astype(BF16)

    @pl.when(unshifted_ref[0] == 1)
    def _():
        _, _, ts, tql = p_ref.shape
        per = tk // ts
        nks = nkb * per
        nsub = tq // tql
        zer_sub = jnp.zeros((1, tql), F32)

        def scores(qs, j, l, slot):
            k = k_ref[j * ts:(j + 1) * ts, :]
            ps = [jnp.exp2(_dot(k, qc)) for qc in qs]
            for c, p in enumerate(ps):
                p_ref[slot, c] = p.astype(BF16)
            return [lc + jnp.sum(p, axis=0, keepdims=True) for lc, p in zip(l, ps)]

        def values(t, h, j, slot):
            v = v_ref[j // per][:, (j % per) * ts:(j % per + 1) * ts]
            for c in (0, 1):
                acc_ref[t, c, :, h * tql:(h + 1) * tql] += _dot(v, p_ref[slot, c])

        stage = 0
        pending = None
        for t in range(nqt):
            q_maps = split_q(t)
            l_sub = []
            for h in range(nsub):
                qs = [qm[:, h * tql:(h + 1) * tql] for qm in q_maps]
                l = [zer_sub, zer_sub]
                for j in range(nks):
                    l = scores(qs, j, l, stage % 2)
                    if pending is not None:
                        values(*pending)
                        if pending[1:3] == (nsub - 1, nks - 1):
                            finish(pending[0], *l_done)
                    pending = (t, h, j, stage % 2)
                    stage += 1
                l_sub.append(l)
            l_done = [jnp.concatenate([l[c] for l in l_sub], axis=1) for c in (0, 1)]
        values(*pending)
        finish(pending[0], *l_done)

    @pl.when(unshifted_ref[0] == 0)
    def _():
        def one_map(k, v, qc, m, l, t, c):
            s = _dot(k, qc)
            m_new = jnp.maximum(m, jnp.max(s, axis=0, keepdims=True))
            alpha = jnp.exp2(m - m_new)
            p = jnp.exp2(s - m_new)
            acc_ref[t, c] = alpha * acc_ref[t, c] + _dot(v, p.astype(BF16))
            return m_new, alpha * l + jnp.sum(p, axis=0, keepdims=True)

        neg = jnp.full((1, tq), _NEG_BIG, F32)
        for t in range(nqt):
            q1, q2 = split_q(t)

            def body(kb, carry):
                k, v = load_kv(kb)
                m1, l1 = one_map(k, v, q1, carry[0], carry[1], t, 0)
                m2, l2 = one_map(k, v, q2, carry[2], carry[3], t, 1)
                return m1, l1, m2, l2

            _, l1, _, l2 = lax.fori_loop(0, nkb, body, (neg, zer, neg, zer))
            finish(t, l1, l2)


def _store_row_tiles(ref, x):
    rows = x.shape[0]
    for j in range(ROW_SUBLANES):
        ref[pl.ds(j, rows, stride=ROW_SUBLANES), :] = x[:, j * LANES:(j + 1) * LANES]


def _load_row_tiles(ref, base, rows):
    return jnp.concatenate(
        [ref[pl.ds(base + j, rows, stride=ROW_SUBLANES), :] for j in range(ROW_SUBLANES)], axis=1)


def _attn_out_router_kernel(o_ref, x_ref, w_ref, g_ref, rt_ref, x2_ref, h2_ref, idx_ref, gate_ref):
    tn = (((0,), (0,)), ((), ()))
    y = lax.dot_general(o_ref[0], w_ref[...], tn, preferred_element_type=F32)
    x2 = x_ref[...] + y
    x2_ref[...] = x2
    h2 = _rms_rows(x2, g_ref[...])
    _store_row_tiles(h2_ref, h2)
    ne = rt_ref.shape[0]
    nt = (((1,), (1,)), ((), ()))
    r = rt_ref[...]
    r_hi = r.astype(BF16)
    r_lo = (r - r_hi.astype(F32)).astype(BF16)
    h_hi = h2.astype(BF16)
    h_lo = (h2 - h_hi.astype(F32)).astype(BF16)
    both = lax.dot_general(jnp.concatenate([r_hi, r_lo], axis=0), h_hi, nt, preferred_element_type=F32)
    logits = (both[0:ne] + both[ne:2 * ne]
              + lax.dot_general(r_hi, h_lo, nt, preferred_element_type=F32))
    eid = lax.broadcasted_iota(jnp.int32, logits.shape, 0)
    m1 = jnp.max(logits, axis=0, keepdims=True)
    i1 = jnp.min(jnp.where(logits == m1, eid, ne), axis=0, keepdims=True)
    rest = jnp.where(eid == i1, -jnp.inf, logits)
    m2 = jnp.max(rest, axis=0, keepdims=True)
    i2 = jnp.min(jnp.where(rest == m2, eid, ne), axis=0, keepdims=True)
    e2 = jnp.exp(m2 - m1)
    den = 1.0 + e2
    idx_ref[0:1, :] = i1
    idx_ref[1:2, :] = i2
    gate_ref[0:1, :] = 1.0 / den
    gate_ref[1:2, :] = e2 / den


def _diff_attention_and_router(x, positions, g_mix, w_in, q_norm, k_norm, lam_params, subln, w_out,
                               g_ffn, router, lambda_init, batch, seq):
    t, d = x.shape
    tm = TOKEN_TILE
    nt = t // tm
    inv_freq = ROPE_THETA ** (-jnp.arange(0, ROT_DIM, 2, dtype=F32) / ROT_DIM)
    q, k, v = pl.pallas_call(
        _qkv_kernel,
        grid=(nt,),
        in_specs=[
            pl.BlockSpec((tm, d), lambda i: (i, 0)),
            _resident((1, d)),
            _resident((d, 3 * d)),
            pl.BlockSpec((1, tm), lambda i: (0, i)),
            _resident((ROT_DIM // 2, 1)),
            _resident((HEAD_DIM, 1)),
            _resident((HEAD_DIM, 1)),
        ],
        out_specs=[
            pl.BlockSpec((1, d, tm), lambda i: (i, 0, 0)),
            pl.BlockSpec((tm, d), lambda i: (i, 0)),
            pl.BlockSpec((1, d, tm), lambda i: (i, 0, 0)),
        ],
        out_shape=[
            jax.ShapeDtypeStruct((nt, d, tm), BF16),
            jax.ShapeDtypeStruct((t, d), BF16),
            jax.ShapeDtypeStruct((nt, d, tm), BF16),
        ],
        scratch_shapes=[pltpu.VMEM((3 * d, d), BF16)],
        compiler_params=_cparams("arbitrary"),
        name="attn_qkv",
    )(x, g_mix, w_in, positions.reshape(1, t), inv_freq.reshape(-1, 1),
      q_norm.reshape(-1, 1), k_norm.reshape(-1, 1))

    nq = seq // tm
    qg = ATTN_Q_TILES
    ngrp = nq // qg
    score_bound = (HEAD_DIM ** 0.5 * LOG2E * _BF16_ROUNDING_MARGIN
                   * jnp.max(jnp.abs(q_norm)) * jnp.max(jnp.abs(k_norm)))
    unshifted = (score_bound <= _EXP2_SAFE_LOG2).astype(jnp.int32).reshape(1)
    o = pl.pallas_call(
        functools.partial(_attn_kernel, lambda_init),
        grid_spec=pltpu.PrefetchScalarGridSpec(
            num_scalar_prefetch=1,
            grid=(batch, N_HEADS, ngrp),
            in_specs=[
                pl.BlockSpec((qg, V_DIM, tm), lambda b, h, i, f: (b * ngrp + i, h, 0)),
                pl.BlockSpec((seq, V_DIM), lambda b, h, i, f: (b, h)),
                pl.BlockSpec((nq, V_DIM, tm), lambda b, h, i, f: (b, h, 0)),
                pl.BlockSpec((4, HEAD_DIM), lambda b, h, i, f: (0, 0)),
                pl.BlockSpec((V_DIM, 1), lambda b, h, i, f: (0, 0)),
            ],
            out_specs=pl.BlockSpec((qg, V_DIM, tm), lambda b, h, i, f: (b * ngrp + i, h, 0)),
            scratch_shapes=[pltpu.VMEM((qg, 2, V_DIM, tm), F32),
                            pltpu.VMEM((2, 2, ATTN_KV_TILE, ATTN_Q_LANES), BF16)],
        ),
        out_shape=jax.ShapeDtypeStruct((nt, d, tm), BF16),
        compiler_params=_cparams("parallel", "parallel", "arbitrary"),
        name="diff_attn",
    )(unshifted, q, k, v, lam_params, subln.reshape(-1, 1))

    x2, h2, idx, gates = pl.pallas_call(
        _attn_out_router_kernel,
        grid=(nt,),
        in_specs=[
            pl.BlockSpec((1, d, tm), lambda i: (i, 0, 0)),
            pl.BlockSpec((tm, d), lambda i: (i, 0)),
            _resident((d, d)),
            _resident((1, d)),
            _resident((N_EXPERTS, d)),
        ],
        out_specs=[
            pl.BlockSpec((tm, d), lambda i: (i, 0)),
            pl.BlockSpec((tm * ROW_SUBLANES, LANES), lambda i: (i, 0)),
            pl.BlockSpec((2, tm), lambda i: (0, i)),
            pl.BlockSpec((2, tm), lambda i: (0, i)),
        ],
        out_shape=[
            jax.ShapeDtypeStruct((t, d), F32),
            jax.ShapeDtypeStruct((t * ROW_SUBLANES, LANES), F32),
            jax.ShapeDtypeStruct((2, t), jnp.int32),
            jax.ShapeDtypeStruct((2, t), F32),
        ],
        compiler_params=_cparams("parallel"),
        name="attn_out_router",
    )(o, x, w_out, g_ffn, router.T)
    return x2, h2, idx, gates.T


def _routing_plan(idx, n_tiles):
    experts = jnp.arange(N_EXPERTS, dtype=jnp.int32)[:, None]
    chosen = [(idx[c][None, :] == experts).astype(jnp.int32) for c in (0, 1)]
    hot = chosen[0] + chosen[1]
    incl = jnp.cumsum(hot, axis=1)
    rank = incl - hot
    tiles = (incl[:, -1] + MOE_TILE - 1) // MOE_TILE
    tile_end = jnp.cumsum(tiles)
    row_start = (tile_end - tiles) * MOE_TILE
    slot = row_start[:, None] + rank
    pos = jnp.stack([jnp.sum(slot * one_hot, axis=0) for one_hot in chosen])
    tile_id = jnp.arange(n_tiles, dtype=jnp.int32)
    tile_expert = jnp.minimum(jnp.sum(tile_id[:, None] >= tile_end[None, :], axis=1), N_EXPERTS - 1)
    new_expert = jnp.concatenate([jnp.ones((1,), bool), tile_expert[1:] != tile_expert[:-1]])
    tile_state = jnp.where(tile_id < tile_end[-1], 1 + new_expert.astype(jnp.int32), 0)
    last_tile = jnp.where(tiles > 0, tile_end - 1, -1)
    tail = tile_end[-1] + jnp.arange(N_EXPERTS, dtype=jnp.int32)
    pad_tiles = jnp.concatenate([last_tile, jnp.where(tail < n_tiles, tail, -1)])
    return (pos.astype(jnp.int32), tile_expert.astype(jnp.int32), tile_state.astype(jnp.int32),
            pad_tiles.astype(jnp.int32))


def _row_tile(ref, row):
    return ref.at[pl.ds(pl.multiple_of(row * ROW_SUBLANES, ROW_SUBLANES), ROW_SUBLANES)]


def _dispatch_kernel(pad_ref, pos_ref, h_hbm, xs_hbm, zero_ref, ring, load_sem, row_sem):
    i = pl.program_id(0)
    last = pl.num_programs(0) - 1
    n = pos_ref.shape[2] // 2
    mt = zero_ref.shape[0]
    nslot = ring.shape[0]
    block_rows = n * ROW_SUBLANES

    def load(step):
        src = h_hbm.at[pl.ds(pl.multiple_of(step * block_rows, block_rows), block_rows)]
        return pltpu.make_async_copy(src, ring.at[step % nslot], load_sem.at[step % nslot])

    def wait_rows(step):
        for _ in (0, 1):
            pltpu.make_async_copy(ring.at[0], xs_hbm.at[pl.ds(0, block_rows)],
                                  row_sem.at[step % nslot]).wait()

    @pl.when(i == 0)
    def _():
        load(0).start()
        zero_ref[...] = jnp.zeros_like(zero_ref)
        fills = [pltpu.make_async_copy(
            zero_ref, xs_hbm.at[pl.ds(pl.multiple_of(jnp.maximum(pad_ref[j], 0) * mt, mt), mt)],
            row_sem.at[0]) for j in range(pad_ref.shape[0])]
        for j, fill in enumerate(fills):
            pl.when(pad_ref[j] >= 0)(fill.start)
        for j, fill in enumerate(fills):
            pl.when(pad_ref[j] >= 0)(fill.wait)

    pl.when(i < last)(lambda: load(i + 1).start())
    load(i).wait()
    block = ring.at[i % nslot]

    def issue(r, carry):
        for c in (0, 1):
            pltpu.make_async_copy(_row_tile(block, r), _row_tile(xs_hbm, pos_ref[0, 0, c * n + r]),
                                  row_sem.at[i % nslot]).start(priority=c)
        return carry

    lax.fori_loop(0, n, issue, 0, unroll=8)
    pl.when(i > 0)(lambda: wait_rows(i - 1))
    pl.when(i == last)(lambda: wait_rows(i))


def _expert_kernel(te_ref, ts_ref, xs_ref, w1_hbm, w3_hbm, w2_hbm, y_ref, w1_ref, w3_ref, w2_ref, sem):
    i = pl.program_id(0)
    mt = xs_ref.shape[0] // ROW_SUBLANES
    chunks = _ff_chunks(w1_ref.shape[1])

    def swiglu_tile(before_chunk):
        x = _load_row_tiles(xs_ref, 0, mt).astype(BF16)
        acc = None
        for c, (f0, fl) in enumerate(chunks):
            before_chunk(c)
            a = _dot(x, w1_ref[:, f0:f0 + fl])
            b = _dot(x, w3_ref[:, f0:f0 + fl])
            act = (_silu(a) * b).astype(BF16)
            part = _dot(act, w2_ref[f0:f0 + fl, :])
            acc = part if acc is None else acc + part
        _store_row_tiles(y_ref, acc)

    @pl.when(ts_ref[i] == 2)
    def _():
        e = te_ref[i]
        fetch = []
        for c, (f0, fl) in enumerate(chunks):
            cols = pl.ds(f0, fl)
            fetch.append([
                pltpu.make_async_copy(w1_hbm.at[e, :, cols], w1_ref.at[:, cols], sem.at[c]),
                pltpu.make_async_copy(w3_hbm.at[e, :, cols], w3_ref.at[:, cols], sem.at[c]),
                pltpu.make_async_copy(w2_hbm.at[e, cols, :], w2_ref.at[cols, :], sem.at[c]),
            ])
        for copies in fetch:
            for copy in copies:
                copy.start()

        def wait_chunk(c):
            for copy in fetch[c]:
                copy.wait()

        swiglu_tile(wait_chunk)

    @pl.when(ts_ref[i] == 1)
    def _():
        swiglu_tile(lambda c: None)

    @pl.when(ts_ref[i] == 0)
    def _():
        y_ref[...] = jnp.zeros_like(y_ref)


def _combine_kernel(pos_ref, pos_next_ref, x_ref, g_ref, y_hbm, o_ref, buf, sem):
    i = pl.program_id(0)
    n = x_ref.shape[0]

    def gather(p_ref, slot):
        def issue(r, carry):
            for c in (0, 1):
                pltpu.make_async_copy(_row_tile(y_hbm, p_ref[0, 0, c * n + r]),
                                      _row_tile(buf, (slot * 2 + c) * n + r),
                                      sem.at[slot]).start(priority=c)
            return carry
        lax.fori_loop(0, n, issue, 0, unroll=8)

    @pl.when(i == 0)
    def _():
        gather(pos_ref, 0)

    @pl.when(i + 1 < pl.num_programs(0))
    def _():
        gather(pos_next_ref, (i + 1) % 2)

    slot = i % 2
    chunk = n * ROW_SUBLANES
    base = pl.multiple_of(slot * 2 * chunk, chunk)
    pltpu.make_async_copy(y_hbm.at[pl.ds(0, 2 * chunk)], buf.at[pl.ds(base, 2 * chunk)],
                          sem.at[slot]).wait()
    g = g_ref[...]
    o_ref[...] = (x_ref[...] + g[:, 0:1] * _load_row_tiles(buf, base, n)
                  + g[:, 1:2] * _load_row_tiles(buf, base + chunk, n))


def _moe(x2, h2, idx, gates, w1, w3, w2):
    t, d = x2.shape
    fe = w1.shape[2]
    assert d == ROW_SUBLANES * LANES
    rs = ROW_SUBLANES
    n_tiles = (2 * t) // MOE_TILE + N_EXPERTS
    rows = n_tiles * MOE_TILE
    pos, tile_expert, tile_state, pad_tiles = _routing_plan(idx, n_tiles)
    gt = GATHER_TILE
    ng = t // gt
    pos_blocks = pos.reshape(2, ng, gt).transpose(1, 0, 2).reshape(ng, 1, 2 * gt)

    xs = pl.pallas_call(
        _dispatch_kernel,
        grid_spec=pltpu.PrefetchScalarGridSpec(
            num_scalar_prefetch=1,
            grid=(ng,),
            in_specs=[
                pl.BlockSpec((1, 1, 2 * gt), lambda i, pad: (i, 0, 0), memory_space=pltpu.SMEM),
                pl.BlockSpec(memory_space=pl.ANY),
            ],
            out_specs=pl.BlockSpec(memory_space=pl.ANY),
            scratch_shapes=[pltpu.VMEM((MOE_TILE * rs, LANES), F32),
                            pltpu.VMEM((DISPATCH_RING, gt * rs, LANES), F32),
                            pltpu.SemaphoreType.DMA((DISPATCH_RING,)),
                            pltpu.SemaphoreType.DMA((DISPATCH_RING,))],
        ),
        out_shape=jax.ShapeDtypeStruct((rows * rs, LANES), F32),
        compiler_params=_cparams("arbitrary"),
        name="moe_dispatch",
    )(pad_tiles, pos_blocks, h2)

    y = pl.pallas_call(
        _expert_kernel,
        grid_spec=pltpu.PrefetchScalarGridSpec(
            num_scalar_prefetch=2,
            grid=(n_tiles,),
            in_specs=[
                pl.BlockSpec((MOE_TILE * rs, LANES), lambda i, te, ts: (i, 0)),
                pl.BlockSpec(memory_space=pl.ANY),
                pl.BlockSpec(memory_space=pl.ANY),
                pl.BlockSpec(memory_space=pl.ANY),
            ],
            out_specs=pl.BlockSpec((MOE_TILE * rs, LANES), lambda i, te, ts: (i, 0)),
            scratch_shapes=[pltpu.VMEM((d, fe), F32), pltpu.VMEM((d, fe), F32),
                            pltpu.VMEM((fe, d), F32),
                            pltpu.SemaphoreType.DMA((len(_ff_chunks(fe)),))],
        ),
        out_shape=jax.ShapeDtypeStruct((rows * rs, LANES), F32),
        compiler_params=_cparams("arbitrary", vmem_limit=EXPERT_VMEM_LIMIT),
        name="moe_experts",
    )(tile_expert, tile_state, xs, w1, w3, w2)

    return pl.pallas_call(
        _combine_kernel,
        grid=(ng,),
        in_specs=[
            pl.BlockSpec((1, 1, 2 * gt), lambda i: (i, 0, 0), memory_space=pltpu.SMEM),
            pl.BlockSpec((1, 1, 2 * gt), lambda i: (jnp.minimum(i + 1, ng - 1), 0, 0),
                         memory_space=pltpu.SMEM),
            pl.BlockSpec((gt, d), lambda i: (i, 0)),
            pl.BlockSpec((gt, 2), lambda i: (i, 0)),
            pl.BlockSpec(memory_space=pl.ANY),
        ],
        out_specs=pl.BlockSpec((gt, d), lambda i: (i, 0)),
        out_shape=jax.ShapeDtypeStruct((t, d), F32),
        scratch_shapes=[pltpu.VMEM((2 * 2 * gt * rs, LANES), F32), pltpu.SemaphoreType.DMA((2,))],
        compiler_params=_cparams("arbitrary"),
        name="moe_combine",
    )(pos_blocks, pos_blocks, x2, gates, y)


def kernel(x, positions, norm_mix, norm_ffn, conv_in, conv_w, conv_out, attn_in, q_norm, k_norm,
           lam_q1, lam_k1, lam_q2, lam_k2, subln, attn_out, ffn_w1, ffn_w3, ffn_w2,
           router, moe_w1, moe_w3, moe_w2):
    batch, seq, d = x.shape
    xt = x.reshape(batch * seq, d)
    bf = lambda w: w.astype(BF16)

    xt = _layer0(xt, norm_mix[0:1], conv_in[0], conv_w[0], bf(conv_out[0]),
                 norm_ffn[0:1], ffn_w1[0], ffn_w3[0], ffn_w2[0], seq)

    lambda_init = 0.8 - 0.6 * math.exp(-0.3 * 1)
    lam_params = jnp.concatenate([lam_q1, lam_k1, lam_q2, lam_k2], axis=0)
    x2, h2, idx, gates = _diff_attention_and_router(
        xt, positions, norm_mix[1:2], attn_in[0], q_norm[0], k_norm[0], lam_params, subln[0],
        bf(attn_out[0]), norm_ffn[1:2], router[0], lambda_init, batch, seq)
    out = _moe(x2, h2, idx, gates, moe_w1[0], moe_w3[0], moe_w2[0])
    return out.reshape(batch, seq, d)
```

```python
import functools
import math

import jax
import jax.numpy as jnp
from jax import lax
from jax.experimental import pallas as pl
from jax.experimental.pallas import tpu as pltpu

F32 = jnp.float32
BF16 = jnp.bfloat16

D_MODEL = 1024
N_HEADS = 8
HEAD_DIM = 64
V_DIM = 2 * HEAD_DIM
ROT_DIM = HEAD_DIM // 4
ROPE_THETA = 500000.0
N_EXPERTS = 8
RMS_EPS = 1e-6
LOG2E = 1.4426950408889634
LANES = 128
ROW_SUBLANES = 8

TOKEN_TILE = 512
ATTN_KV_TILE = 256
ATTN_Q_LANES = 512
ATTN_Q_TILES = 2
MOE_TILE = 512
FF_CHUNK = 512
DENSE_FF_CHUNK = 256
GATHER_TILE = 512
DISPATCH_RING = 3
VMEM_LIMIT = 56 * 1024 * 1024
EXPERT_VMEM_LIMIT = 62 * 1024 * 1024

_NEG_BIG = -1e30
_EXP2_SAFE_LOG2 = 60.0
_BF16_ROUNDING_MARGIN = 1.02


def _cparams(*sem, vmem_limit=VMEM_LIMIT):
    return pltpu.CompilerParams(dimension_semantics=sem, vmem_limit_bytes=vmem_limit)


def _rms_rows(x, g):
    ms = jnp.mean(x * x, axis=-1, keepdims=True)
    return x * lax.rsqrt(ms + RMS_EPS) * g


def _dot(a, b):
    return lax.dot_general(a, b, (((1,), (0,)), ((), ())), preferred_element_type=F32)


def _silu(a):
    return a * (1.0 / (1.0 + jnp.exp(-a)))


def _ff_chunks(width, chunk=FF_CHUNK):
    out, f0 = [], 0
    while f0 < width:
        fl = min(chunk, width - f0)
        out.append((f0, fl))
        f0 += fl
    return out


def _conv_in_kernel(x_ref, g_ref, w_ref, b_ref, u_ref):
    d = x_ref.shape[1]
    h = _rms_rows(x_ref[...], g_ref[...]).astype(BF16)
    b_ref[...] = _dot(h, w_ref[:, 0:d]).astype(BF16)
    c = _dot(h, w_ref[:, d:2 * d])
    v = _dot(h, w_ref[:, 2 * d:3 * d])
    u_ref[...] = (c * v).astype(BF16)


def _conv_out_ffn_kernel(tiles_per_seq, u_ref, up_ref, un_ref, b_ref, x_ref, cw_ref, w_ref,
                         g_ref, w1_ref, w3_ref, w2_ref, o_ref):
    i = pl.program_id(0)
    tm = u_ref.shape[0]
    halo = up_ref.shape[0]
    u = u_ref[...].astype(F32)
    first = (i % tiles_per_seq) == 0
    last = (i % tiles_per_seq) == tiles_per_seq - 1
    prev_row = jnp.where(first, 0.0, up_ref[halo - 1:halo, :].astype(F32))
    next_row = jnp.where(last, 0.0, un_ref[0:1, :].astype(F32))
    rows = lax.broadcasted_iota(jnp.int32, (tm, 1), 0)
    u_m1 = jnp.where(rows == 0, prev_row, pltpu.roll(u, 1, 0))
    u_p1 = jnp.where(rows == tm - 1, next_row, pltpu.roll(u, tm - 1, 0))
    cw = cw_ref[...]
    conv = u_m1 * cw[0:1, :] + u * cw[1:2, :] + u_p1 * cw[2:3, :]
    y = (b_ref[...].astype(F32) * conv).astype(BF16)
    x1 = x_ref[...] + _dot(y, w_ref[...])
    h = _rms_rows(x1, g_ref[...]).astype(BF16)
    acc = x1
    for f0, fl in _ff_chunks(w1_ref.shape[1], DENSE_FF_CHUNK):
        a = _dot(h, w1_ref[:, f0:f0 + fl])
        b = _dot(h, w3_ref[:, f0:f0 + fl])
        acc = acc + _dot((_silu(a) * b).astype(BF16), w2_ref[f0:f0 + fl, :])
    o_ref[...] = acc


def _resident(shape):
    return pl.BlockSpec(shape, lambda i: (0,) * len(shape), pipeline_mode=pl.Buffered(1))


def _layer0(x, g_mix, w_in, conv_w, w_out, g_ffn, w1, w3, w2, seq):
    t, d = x.shape
    f = w1.shape[1]
    tm = TOKEN_TILE
    nt = t // tm
    halo = 16
    b, u = pl.pallas_call(
        _conv_in_kernel,
        grid=(nt,),
        in_specs=[
            pl.BlockSpec((tm, d), lambda i: (i, 0)),
            _resident((1, d)),
            _resident((d, 3 * d)),
        ],
        out_specs=[pl.BlockSpec((tm, d), lambda i: (i, 0))] * 2,
        out_shape=[jax.ShapeDtypeStruct((t, d), BF16)] * 2,
        compiler_params=_cparams("parallel"),
        name="conv_in",
    )(x, g_mix, w_in)
    hb = tm // halo
    nhb = t // halo
    return pl.pallas_call(
        functools.partial(_conv_out_ffn_kernel, seq // tm),
        grid=(nt,),
        in_specs=[
            pl.BlockSpec((tm, d), lambda i: (i, 0)),
            pl.BlockSpec((halo, d), lambda i: (jnp.maximum(i * hb - 1, 0), 0)),
            pl.BlockSpec((halo, d), lambda i: (jnp.minimum((i + 1) * hb, nhb - 1), 0)),
            pl.BlockSpec((tm, d), lambda i: (i, 0)),
            pl.BlockSpec((tm, d), lambda i: (i, 0)),
            _resident((3, d)),
            _resident((d, d)),
            _resident((1, d)),
            _resident((d, f)),
            _resident((d, f)),
            _resident((f, d)),
        ],
        out_specs=pl.BlockSpec((tm, d), lambda i: (i, 0)),
        out_shape=jax.ShapeDtypeStruct((t, d), F32),
        compiler_params=_cparams("parallel"),
        name="conv_out_ffn",
    )(u, u, u, b, x, conv_w, w_out, g_ffn, w1, w3, w2)


def _qkv_kernel(x_ref, g_ref, w_ref, pos_ref, freq_ref, qn_ref, kn_ref, q_ref, k_ref, v_ref, wt_ref):
    tm, d = x_ref.shape

    @pl.when(pl.program_id(0) == 0)
    def _():
        for j in range(3):
            wt_ref[j * d:(j + 1) * d, :] = w_ref[:, j * d:(j + 1) * d].T.astype(BF16)

    h = _rms_rows(x_ref[...], g_ref[...]).astype(BF16)
    nt = (((1,), (1,)), ((), ()))
    ang = pos_ref[...].astype(F32) * freq_ref[...]
    cos = jnp.cos(ang)[None]
    sin = jnp.sin(ang)[None]
    half = ROT_DIM // 2
    groups = d // HEAD_DIM

    def norm_rope(w_rows, gain, scale):
        t = lax.dot_general(w_rows, h, nt, preferred_element_type=F32)
        t = t.reshape(groups, HEAD_DIM, tm)
        ms = jnp.mean(t * t, axis=1, keepdims=True)
        t = t * lax.rsqrt(ms + RMS_EPS) * gain[None]
        t1 = t[:, 0:half, :]
        t2 = t[:, half:ROT_DIM, :]
        t = jnp.concatenate([t1 * cos - t2 * sin, t2 * cos + t1 * sin, t[:, ROT_DIM:, :]], axis=1)
        return (t * scale).reshape(d, tm)

    q = norm_rope(wt_ref[0:d, :], qn_ref[...], (HEAD_DIM ** -0.5) * LOG2E)
    q_ref[0] = q.astype(BF16)
    k = norm_rope(wt_ref[d:2 * d, :], kn_ref[...], 1.0)
    k_ref[...] = k.T.astype(BF16)
    v = lax.dot_general(wt_ref[2 * d:3 * d, :], h, nt, preferred_element_type=F32)
    v_ref[0] = v.astype(BF16)


def _attn_kernel(lambda_init, unshifted_ref, q_ref, k_ref, v_ref, lam_ref, sub_ref, o_ref,
                 acc_ref, p_ref):
    nqt, _, tq = q_ref.shape
    nkb = v_ref.shape[0]
    tk = v_ref.shape[2]
    rows = lax.broadcasted_iota(jnp.int32, (V_DIM, 1), 0)
    acc_ref[...] = jnp.zeros_like(acc_ref)
    zer = jnp.zeros((1, tq), F32)

    def split_q(t):
        q = q_ref[t]
        zero = jnp.zeros_like(q)
        return jnp.where(rows < HEAD_DIM, q, zero), jnp.where(rows >= HEAD_DIM, q, zero)

    def load_kv(kb):
        k = k_ref[pl.ds(pl.multiple_of(kb * tk, tk), tk), :]
        return k, v_ref[kb]

    def finish(t, l1, l2):
        lp = lam_ref[...]
        lam = (jnp.exp(jnp.sum(lp[0:1, :] * lp[1:2, :], axis=-1, keepdims=True))
               - jnp.exp(jnp.sum(lp[2:3, :] * lp[3:4, :], axis=-1, keepdims=True))
               + lambda_init)
        o = acc_ref[t, 0] / l1 - lam * (acc_ref[t, 1] / l2)
        ms = jnp.mean(o * o, axis=0, keepdims=True)
        o = o * lax.rsqrt(ms + RMS_EPS) * sub_ref[...] * (1.0 - lambda_init)
        o_ref[t] = o.astype(BF16)

    @pl.when(unshifted_ref[0] == 1)
    def _():
        _, _, ts, tql = p_ref.shape
        per = tk // ts
        nks = nkb * per
        nsub = tq // tql
        zer_sub = jnp.zeros((1, tql), F32)

        def scores(qs, j, l, slot):
            k = k_ref[j * ts:(j + 1) * ts, :]
            ps = [jnp.exp2(_dot(k, qc)) for qc in qs]
            for c, p in enumerate(ps):
                p_ref[slot, c] = p.astype(BF16)
            return [lc + jnp.sum(p, axis=0, keepdims=True) for lc, p in zip(l, ps)]

        def values(t, h, j, slot):
            v = v_ref[j // per][:, (j % per) * ts:(j % per + 1) * ts]
            for c in (0, 1):
                acc_ref[t, c, :, h * tql:(h + 1) * tql] += _dot(v, p_ref[slot, c])

        stage = 0
        pending = None
        for t in range(nqt):
            q_maps = split_q(t)
            l_sub = []
            for h in range(nsub):
                qs = [qm[:, h * tql:(h + 1) * tql] for qm in q_maps]
                l = [zer_sub, zer_sub]
                for j in range(nks):
                    l = scores(qs, j, l, stage % 2)
                    if pending is not None:
                        values(*pending)
                        if pending[1:3] == (nsub - 1, nks - 1):
                            finish(pending[0], *l_done)
                    pending = (t, h, j, stage % 2)
                    stage += 1
                l_sub.append(l)
            l_done = [jnp.concatenate([l[c] for l in l_sub], axis=1) for c in (0, 1)]
        values(*pending)
        finish(pending[0], *l_done)

    @pl.when(unshifted_ref[0] == 0)
    def _():
        def one_map(k, v, qc, m, l, t, c):
            s = _dot(k, qc)
            m_new = jnp.maximum(m, jnp.max(s, axis=0, keepdims=True))
            alpha = jnp.exp2(m - m_new)
            p = jnp.exp2(s - m_new)
            acc_ref[t, c] = alpha * acc_ref[t, c] + _dot(v, p.astype(BF16))
            return m_new, alpha * l + jnp.sum(p, axis=0, keepdims=True)

        neg = jnp.full((1, tq), _NEG_BIG, F32)
        for t in range(nqt):
            q1, q2 = split_q(t)

            def body(kb, carry):
                k, v = load_kv(kb)
                m1, l1 = one_map(k, v, q1, carry[0], carry[1], t, 0)
                m2, l2 = one_map(k, v, q2, carry[2], carry[3], t, 1)
                return m1, l1, m2, l2

            _, l1, _, l2 = lax.fori_loop(0, nkb, body, (neg, zer, neg, zer))
            finish(t, l1, l2)


def _store_row_tiles(ref, x):
    rows = x.shape[0]
    for j in range(ROW_SUBLANES):
        ref[pl.ds(j, rows, stride=ROW_SUBLANES), :] = x[:, j * LANES:(j + 1) * LANES]


def _load_row_tiles(ref, base, rows):
    return jnp.concatenate(
        [ref[pl.ds(base + j, rows, stride=ROW_SUBLANES), :] for j in range(ROW_SUBLANES)], axis=1)


def _attn_out_router_kernel(o_ref, x_ref, w_ref, g_ref, rt_ref, x2_ref, h2_ref, idx_ref, gate_ref):
    tn = (((0,), (0,)), ((), ()))
    y = lax.dot_general(o_ref[0], w_ref[...], tn, preferred_element_type=F32)
    x2 = x_ref[...] + y
    x2_ref[...] = x2
    h2 = _rms_rows(x2, g_ref[...])
    _store_row_tiles(h2_ref, h2)
    ne = rt_ref.shape[0]
    nt = (((1,), (1,)), ((), ()))
    r = rt_ref[...]
    r_hi = r.astype(BF16)
    r_lo = (r - r_hi.astype(F32)).astype(BF16)
    h_hi = h2.astype(BF16)
    h_lo = (h2 - h_hi.astype(F32)).astype(BF16)
    both = lax.dot_general(jnp.concatenate([r_hi, r_lo], axis=0), h_hi, nt, preferred_element_type=F32)
    logits = (both[0:ne] + both[ne:2 * ne]
              + lax.dot_general(r_hi, h_lo, nt, preferred_element_type=F32))
    eid = lax.broadcasted_iota(jnp.int32, logits.shape, 0)
    m1 = jnp.max(logits, axis=0, keepdims=True)
    i1 = jnp.min(jnp.where(logits == m1, eid, ne), axis=0, keepdims=True)
    rest = jnp.where(eid == i1, -jnp.inf, logits)
    m2 = jnp.max(rest, axis=0, keepdims=True)
    i2 = jnp.min(jnp.where(rest == m2, eid, ne), axis=0, keepdims=True)
    e2 = jnp.exp(m2 - m1)
    den = 1.0 + e2
    idx_ref[0:1, :] = i1
    idx_ref[1:2, :] = i2
    gate_ref[0:1, :] = 1.0 / den
    gate_ref[1:2, :] = e2 / den


def _diff_attention_and_router(x, positions, g_mix, w_in, q_norm, k_norm, lam_params, subln, w_out,
                               g_ffn, router, lambda_init, batch, seq):
    t, d = x.shape
    tm = TOKEN_TILE
    nt = t // tm
    inv_freq = ROPE_THETA ** (-jnp.arange(0, ROT_DIM, 2, dtype=F32) / ROT_DIM)
    q, k, v = pl.pallas_call(
        _qkv_kernel,
        grid=(nt,),
        in_specs=[
            pl.BlockSpec((tm, d), lambda i: (i, 0)),
            _resident((1, d)),
            _resident((d, 3 * d)),
            pl.BlockSpec((1, tm), lambda i: (0, i)),
            _resident((ROT_DIM // 2, 1)),
            _resident((HEAD_DIM, 1)),
            _resident((HEAD_DIM, 1)),
        ],
        out_specs=[
            pl.BlockSpec((1, d, tm), lambda i: (i, 0, 0)),
            pl.BlockSpec((tm, d), lambda i: (i, 0)),
            pl.BlockSpec((1, d, tm), lambda i: (i, 0, 0)),
        ],
        out_shape=[
            jax.ShapeDtypeStruct((nt, d, tm), BF16),
            jax.ShapeDtypeStruct((t, d), BF16),
            jax.ShapeDtypeStruct((nt, d, tm), BF16),
        ],
        scratch_shapes=[pltpu.VMEM((3 * d, d), BF16)],
        compiler_params=_cparams("arbitrary"),
        name="attn_qkv",
    )(x, g_mix, w_in, positions.reshape(1, t), inv_freq.reshape(-1, 1),
      q_norm.reshape(-1, 1), k_norm.reshape(-1, 1))

    nq = seq // tm
    qg = ATTN_Q_TILES
    ngrp = nq // qg
    score_bound = (HEAD_DIM ** 0.5 * LOG2E * _BF16_ROUNDING_MARGIN
                   * jnp.max(jnp.abs(q_norm)) * jnp.max(jnp.abs(k_norm)))
    unshifted = (score_bound <= _EXP2_SAFE_LOG2).astype(jnp.int32).reshape(1)
    o = pl.pallas_call(
        functools.partial(_attn_kernel, lambda_init),
        grid_spec=pltpu.PrefetchScalarGridSpec(
            num_scalar_prefetch=1,
            grid=(batch, N_HEADS, ngrp),
            in_specs=[
                pl.BlockSpec((qg, V_DIM, tm), lambda b, h, i, f: (b * ngrp + i, h, 0)),
                pl.BlockSpec((seq, V_DIM), lambda b, h, i, f: (b, h)),
                pl.BlockSpec((nq, V_DIM, tm), lambda b, h, i, f: (b, h, 0)),
                pl.BlockSpec((4, HEAD_DIM), lambda b, h, i, f: (0, 0)),
                pl.BlockSpec((V_DIM, 1), lambda b, h, i, f: (0, 0)),
            ],
            out_specs=pl.BlockSpec((qg, V_DIM, tm), lambda b, h, i, f: (b * ngrp + i, h, 0)),
            scratch_shapes=[pltpu.VMEM((qg, 2, V_DIM, tm), F32),
                            pltpu.VMEM((2, 2, ATTN_KV_TILE, ATTN_Q_LANES), BF16)],
        ),
        out_shape=jax.ShapeDtypeStruct((nt, d, tm), BF16),
        compiler_params=_cparams("parallel", "parallel", "arbitrary"),
        name="diff_attn",
    )(unshifted, q, k, v, lam_params, subln.reshape(-1, 1))

    x2, h2, idx, gates = pl.pallas_call(
        _attn_out_router_kernel,
        grid=(nt,),
        in_specs=[
            pl.BlockSpec((1, d, tm), lambda i: (i, 0, 0)),
            pl.BlockSpec((tm, d), lambda i: (i, 0)),
            _resident((d, d)),
            _resident((1, d)),
            _resident((N_EXPERTS, d)),
        ],
        out_specs=[
            pl.BlockSpec((tm, d), lambda i: (i, 0)),
            pl.BlockSpec((tm * ROW_SUBLANES, LANES), lambda i: (i, 0)),
            pl.BlockSpec((2, tm), lambda i: (0, i)),
            pl.BlockSpec((2, tm), lambda i: (0, i)),
        ],
        out_shape=[
            jax.ShapeDtypeStruct((t, d), F32),
            jax.ShapeDtypeStruct((t * ROW_SUBLANES, LANES), F32),
            jax.ShapeDtypeStruct((2, t), jnp.int32),
            jax.ShapeDtypeStruct((2, t), F32),
        ],
        compiler_params=_cparams("parallel"),
        name="attn_out_router",
    )(o, x, w_out, g_ffn, router.T)
    return x2, h2, idx, gates.T


def _routing_plan(idx, n_tiles):
    experts = jnp.arange(N_EXPERTS, dtype=jnp.int32)[:, None]
    chosen = [(idx[c][None, :] == experts).astype(jnp.int32) for c in (0, 1)]
    hot = chosen[0] + chosen[1]
    incl = jnp.cumsum(hot, axis=1)
    rank = incl - hot
    tiles = (incl[:, -1] + MOE_TILE - 1) // MOE_TILE
    tile_end = jnp.cumsum(tiles)
    row_start = (tile_end - tiles) * MOE_TILE
    slot = row_start[:, None] + rank
    pos = jnp.stack([jnp.sum(slot * one_hot, axis=0) for one_hot in chosen])
    tile_id = jnp.arange(n_tiles, dtype=jnp.int32)
    tile_expert = jnp.minimum(jnp.sum(tile_id[:, None] >= tile_end[None, :], axis=1), N_EXPERTS - 1)
    new_expert = jnp.concatenate([jnp.ones((1,), bool), tile_expert[1:] != tile_expert[:-1]])
    tile_state = jnp.where(tile_id < tile_end[-1], 1 + new_expert.astype(jnp.int32), 0)
    last_tile = jnp.where(tiles > 0, tile_end - 1, -1)
    tail = tile_end[-1] + jnp.arange(N_EXPERTS, dtype=jnp.int32)
    pad_tiles = jnp.concatenate([last_tile, jnp.where(tail < n_tiles, tail, -1)])
    return (pos.astype(jnp.int32), tile_expert.astype(jnp.int32), tile_state.astype(jnp.int32),
            pad_tiles.astype(jnp.int32))


def _row_tile(ref, row):
    return ref.at[pl.ds(pl.multiple_of(row * ROW_SUBLANES, ROW_SUBLANES), ROW_SUBLANES)]


def _dispatch_kernel(pad_ref, pos_ref, h_hbm, xs_hbm, zero_ref, ring, load_sem, row_sem):
    i = pl.program_id(0)
    last = pl.num_programs(0) - 1
    n = pos_ref.shape[2] // 2
    mt = zero_ref.shape[0]
    nslot = ring.shape[0]
    block_rows = n * ROW_SUBLANES

    def load(step):
        src = h_hbm.at[pl.ds(pl.multiple_of(step * block_rows, block_rows), block_rows)]
        return pltpu.make_async_copy(src, ring.at[step % nslot], load_sem.at[step % nslot])

    def wait_rows(step):
        for _ in (0, 1):
            pltpu.make_async_copy(ring.at[0], xs_hbm.at[pl.ds(0, block_rows)],
                                  row_sem.at[step % nslot]).wait()

    @pl.when(i == 0)
    def _():
        load(0).start()
        zero_ref[...] = jnp.zeros_like(zero_ref)
        fills = [pltpu.make_async_copy(
            zero_ref, xs_hbm.at[pl.ds(pl.multiple_of(jnp.maximum(pad_ref[j], 0) * mt, mt), mt)],
            row_sem.at[0]) for j in range(pad_ref.shape[0])]
        for j, fill in enumerate(fills):
            pl.when(pad_ref[j] >= 0)(fill.start)
        for j, fill in enumerate(fills):
            pl.when(pad_ref[j] >= 0)(fill.wait)

    pl.when(i < last)(lambda: load(i + 1).start())
    load(i).wait()
    block = ring.at[i % nslot]

    def issue(r, carry):
        for c in (0, 1):
            pltpu.make_async_copy(_row_tile(block, r), _row_tile(xs_hbm, pos_ref[0, 0, c * n + r]),
                                  row_sem.at[i % nslot]).start(priority=c)
        return carry

    lax.fori_loop(0, n, issue, 0, unroll=8)
    pl.when(i > 0)(lambda: wait_rows(i - 1))
    pl.when(i == last)(lambda: wait_rows(i))


def _expert_kernel(te_ref, ts_ref, xs_ref, w1_hbm, w3_hbm, w2_hbm, y_ref, w1_ref, w3_ref, w2_ref, sem):
    i = pl.program_id(0)
    mt = xs_ref.shape[0] // ROW_SUBLANES
    chunks = _ff_chunks(w1_ref.shape[1])

    def swiglu_tile(before_chunk):
        x = _load_row_tiles(xs_ref, 0, mt).astype(BF16)
        acc = None
        for c, (f0, fl) in enumerate(chunks):
            before_chunk(c)
            a = _dot(x, w1_ref[:, f0:f0 + fl])
            b = _dot(x, w3_ref[:, f0:f0 + fl])
            act = (_silu(a) * b).astype(BF16)
            part = _dot(act, w2_ref[f0:f0 + fl, :])
            acc = part if acc is None else acc + part
        _store_row_tiles(y_ref, acc)

    @pl.when(ts_ref[i] == 2)
    def _():
        e = te_ref[i]
        fetch = []
        for c, (f0, fl) in enumerate(chunks):
            cols = pl.ds(f0, fl)
            fetch.append([
                pltpu.make_async_copy(w1_hbm.at[e, :, cols], w1_ref.at[:, cols], sem.at[c]),
                pltpu.make_async_copy(w3_hbm.at[e, :, cols], w3_ref.at[:, cols], sem.at[c]),
                pltpu.make_async_copy(w2_hbm.at[e, cols, :], w2_ref.at[cols, :], sem.at[c]),
            ])
        for copies in fetch:
            for copy in copies:
                copy.start()

        def wait_chunk(c):
            for copy in fetch[c]:
                copy.wait()

        swiglu_tile(wait_chunk)

    @pl.when(ts_ref[i] == 1)
    def _():
        swiglu_tile(lambda c: None)

    @pl.when(ts_ref[i] == 0)
    def _():
        y_ref[...] = jnp.zeros_like(y_ref)


def _combine_kernel(pos_ref, pos_next_ref, x_ref, g_ref, y_hbm, o_ref, buf, sem):
    i = pl.program_id(0)
    n = x_ref.shape[0]

    def gather(p_ref, slot):
        def issue(r, carry):
            for c in (0, 1):
                pltpu.make_async_copy(_row_tile(y_hbm, p_ref[0, 0, c * n + r]),
                                      _row_tile(buf, (slot * 2 + c) * n + r),
                                      sem.at[slot]).start(priority=c)
            return carry
        lax.fori_loop(0, n, issue, 0, unroll=8)

    @pl.when(i == 0)
    def _():
        gather(pos_ref, 0)

    @pl.when(i + 1 < pl.num_programs(0))
    def _():
        gather(pos_next_ref, (i + 1) % 2)

    slot = i % 2
    chunk = n * ROW_SUBLANES
    base = pl.multiple_of(slot * 2 * chunk, chunk)
    pltpu.make_async_copy(y_hbm.at[pl.ds(0, 2 * chunk)], buf.at[pl.ds(base, 2 * chunk)],
                          sem.at[slot]).wait()
    g = g_ref[...]
    o_ref[...] = (x_ref[...] + g[:, 0:1] * _load_row_tiles(buf, base, n)
                  + g[:, 1:2] * _load_row_tiles(buf, base + chunk, n))


def _moe(x2, h2, idx, gates, w1, w3, w2):
    t, d = x2.shape
    fe = w1.shape[2]
    assert d == ROW_SUBLANES * LANES
    rs = ROW_SUBLANES
    n_tiles = (2 * t) // MOE_TILE + N_EXPERTS
    rows = n_tiles * MOE_TILE
    pos, tile_expert, tile_state, pad_tiles = _routing_plan(idx, n_tiles)
    gt = GATHER_TILE
    ng = t // gt
    pos_blocks = pos.reshape(2, ng, gt).transpose(1, 0, 2).reshape(ng, 1, 2 * gt)

    xs = pl.pallas_call(
        _dispatch_kernel,
        grid_spec=pltpu.PrefetchScalarGridSpec(
            num_scalar_prefetch=1,
            grid=(ng,),
            in_specs=[
                pl.BlockSpec((1, 1, 2 * gt), lambda i, pad: (i, 0, 0), memory_space=pltpu.SMEM),
                pl.BlockSpec(memory_space=pl.ANY),
            ],
            out_specs=pl.BlockSpec(memory_space=pl.ANY),
            scratch_shapes=[pltpu.VMEM((MOE_TILE * rs, LANES), F32),
                            pltpu.VMEM((DISPATCH_RING, gt * rs, LANES), F32),
                            pltpu.SemaphoreType.DMA((DISPATCH_RING,)),
                            pltpu.SemaphoreType.DMA((DISPATCH_RING,))],
        ),
        out_shape=jax.ShapeDtypeStruct((rows * rs, LANES), F32),
        compiler_params=_cparams("arbitrary"),
        name="moe_dispatch",
    )(pad_tiles, pos_blocks, h2)

    y = pl.pallas_call(
        _expert_kernel,
        grid_spec=pltpu.PrefetchScalarGridSpec(
            num_scalar_prefetch=2,
            grid=(n_tiles,),
            in_specs=[
                pl.BlockSpec((MOE_TILE * rs, LANES), lambda i, te, ts: (i, 0)),
                pl.BlockSpec(memory_space=pl.ANY),
                pl.BlockSpec(memory_space=pl.ANY),
                pl.BlockSpec(memory_space=pl.ANY),
            ],
            out_specs=pl.BlockSpec((MOE_TILE * rs, LANES), lambda i, te, ts: (i, 0)),
            scratch_shapes=[pltpu.VMEM((d, fe), F32), pltpu.VMEM((d, fe), F32),
                            pltpu.VMEM((fe, d), F32),
                            pltpu.SemaphoreType.DMA((len(_ff_chunks(fe)),))],
        ),
        out_shape=jax.ShapeDtypeStruct((rows * rs, LANES), F32),
        compiler_params=_cparams("arbitrary", vmem_limit=EXPERT_VMEM_LIMIT),
        name="moe_experts",
    )(tile_expert, tile_state, xs, w1, w3, w2)

    return pl.pallas_call(
        _combine_kernel,
        grid=(ng,),
        in_specs=[
            pl.BlockSpec((1, 1, 2 * gt), lambda i: (i, 0, 0), memory_space=pltpu.SMEM),
            pl.BlockSpec((1, 1, 2 * gt), lambda i: (jnp.minimum(i + 1, ng - 1), 0, 0),
                         memory_space=pltpu.SMEM),
            pl.BlockSpec((gt, d), lambda i: (i, 0)),
            pl.BlockSpec((gt, 2), lambda i: (i, 0)),
            pl.BlockSpec(memory_space=pl.ANY),
        ],
        out_specs=pl.BlockSpec((gt, d), lambda i: (i, 0)),
        out_shape=jax.ShapeDtypeStruct((t, d), F32),
        scratch_shapes=[pltpu.VMEM((2 * 2 * gt * rs, LANES), F32), pltpu.SemaphoreType.DMA((2,))],
        compiler_params=_cparams("arbitrary"),
        name="moe_combine",
    )(pos_blocks, pos_blocks, x2, gates, y)


def kernel(x, positions, norm_mix, norm_ffn, conv_in, conv_w, conv_out, attn_in, q_norm, k_norm,
           lam_q1, lam_k1, lam_q2, lam_k2, subln, attn_out, ffn_w1, ffn_w3, ffn_w2,
           router, moe_w1, moe_w3, moe_w2):
    batch, seq, d = x.shape
    xt = x.reshape(batch * seq, d)
    bf = lambda w: w.astype(BF16)

    xt = _layer0(xt, norm_mix[0:1], conv_in[0], conv_w[0], bf(conv_out[0]),
                 norm_ffn[0:1], ffn_w1[0], ffn_w3[0], ffn_w2[0], seq)

    lambda_init = 0.8 - 0.6 * math.exp(-0.3 * 1)
    lam_params = jnp.concatenate([lam_q1, lam_k1, lam_q2, lam_k2], axis=0)
    x2, h2, idx, gates = _diff_attention_and_router(
        xt, positions, norm_mix[1:2], attn_in[0], q_norm[0], k_norm[0], lam_params, subln[0],
        bf(attn_out[0]), norm_ffn[1:2], router[0], lambda_init, batch, seq)
    out = _moe(x2, h2, idx, gates, moe_w1[0], moe_w3[0], moe_w2[0])
    return out.reshape(batch, seq, d)
```

```python
import functools
import math

import jax
import jax.numpy as jnp
from jax import lax
from jax.experimental import pallas as pl
from jax.experimental.pallas import tpu as pltpu

F32 = jnp.float32
BF16 = jnp.bfloat16

D_MODEL = 1024
N_HEADS = 8
HEAD_DIM = 64
V_DIM = 2 * HEAD_DIM
ROT_DIM = HEAD_DIM // 4
ROPE_THETA = 500000.0
N_EXPERTS = 8
RMS_EPS = 1e-6
LOG2E = 1.4426950408889634
LANES = 128
ROW_SUBLANES = 8

TOKEN_TILE = 512
ATTN_KV_TILE = 256
ATTN_Q_TILES = 2
MOE_TILE = 512
FF_CHUNK = 512
GATHER_TILE = 512
DISPATCH_RING = 3
VMEM_LIMIT = 56 * 1024 * 1024
EXPERT_VMEM_LIMIT = 62 * 1024 * 1024

_NEG_BIG = -1e30
_EXP2_SAFE_LOG2 = 60.0
_BF16_ROUNDING_MARGIN = 1.02


def _cparams(*sem, vmem_limit=VMEM_LIMIT):
    return pltpu.CompilerParams(dimension_semantics=sem, vmem_limit_bytes=vmem_limit)


def _rms_rows(x, g):
    ms = jnp.mean(x * x, axis=-1, keepdims=True)
    return x * lax.rsqrt(ms + RMS_EPS) * g


def _dot(a, b):
    return jnp.dot(a, b, preferred_element_type=F32)


def _silu(a):
    return a * (1.0 / (1.0 + jnp.exp(-a)))


def _ff_chunks(width):
    out, f0 = [], 0
    while f0 < width:
        fl = min(FF_CHUNK, width - f0)
        out.append((f0, fl))
        f0 += fl
    return out


def _conv_in_kernel(x_ref, g_ref, w_ref, b_ref, u_ref):
    d = x_ref.shape[1]
    h = _rms_rows(x_ref[...], g_ref[...])
    b_ref[...] = _dot(h, w_ref[:, 0:d]).astype(BF16)
    c = _dot(h, w_ref[:, d:2 * d])
    v = _dot(h, w_ref[:, 2 * d:3 * d])
    u_ref[...] = (c * v).astype(BF16)


def _conv_out_ffn_kernel(tiles_per_seq, u_ref, up_ref, un_ref, b_ref, x_ref, cw_ref, w_ref,
                         g_ref, w1_ref, w3_ref, w2_ref, o_ref):
    i = pl.program_id(0)
    tm = u_ref.shape[0]
    halo = up_ref.shape[0]
    u = u_ref[...].astype(F32)
    first = (i % tiles_per_seq) == 0
    last = (i % tiles_per_seq) == tiles_per_seq - 1
    prev_row = jnp.where(first, 0.0, up_ref[halo - 1:halo, :].astype(F32))
    next_row = jnp.where(last, 0.0, un_ref[0:1, :].astype(F32))
    rows = lax.broadcasted_iota(jnp.int32, (tm, 1), 0)
    u_m1 = jnp.where(rows == 0, prev_row, pltpu.roll(u, 1, 0))
    u_p1 = jnp.where(rows == tm - 1, next_row, pltpu.roll(u, tm - 1, 0))
    cw = cw_ref[...]
    conv = u_m1 * cw[0:1, :] + u * cw[1:2, :] + u_p1 * cw[2:3, :]
    y = (b_ref[...].astype(F32) * conv).astype(BF16)
    x1 = x_ref[...] + _dot(y, w_ref[...])
    h = _rms_rows(x1, g_ref[...])
    acc = x1
    for f0, fl in _ff_chunks(w1_ref.shape[1]):
        a = _dot(h, w1_ref[:, f0:f0 + fl])
        b = _dot(h, w3_ref[:, f0:f0 + fl])
        acc = acc + _dot(_silu(a) * b, w2_ref[f0:f0 + fl, :])
    o_ref[...] = acc


def _resident(shape):
    return pl.BlockSpec(shape, lambda i: (0,) * len(shape), pipeline_mode=pl.Buffered(1))


def _layer0(x, g_mix, w_in, conv_w, w_out, g_ffn, w1, w3, w2, seq):
    t, d = x.shape
    f = w1.shape[1]
    tm = TOKEN_TILE
    nt = t // tm
    halo = 16
    b, u = pl.pallas_call(
        _conv_in_kernel,
        grid=(nt,),
        in_specs=[
            pl.BlockSpec((tm, d), lambda i: (i, 0)),
            _resident((1, d)),
            _resident((d, 3 * d)),
        ],
        out_specs=[pl.BlockSpec((tm, d), lambda i: (i, 0))] * 2,
        out_shape=[jax.ShapeDtypeStruct((t, d), BF16)] * 2,
        compiler_params=_cparams("parallel"),
        name="conv_in",
    )(x, g_mix, w_in)
    hb = tm // halo
    nhb = t // halo
    return pl.pallas_call(
        functools.partial(_conv_out_ffn_kernel, seq // tm),
        grid=(nt,),
        in_specs=[
            pl.BlockSpec((tm, d), lambda i: (i, 0)),
            pl.BlockSpec((halo, d), lambda i: (jnp.maximum(i * hb - 1, 0), 0)),
            pl.BlockSpec((halo, d), lambda i: (jnp.minimum((i + 1) * hb, nhb - 1), 0)),
            pl.BlockSpec((tm, d), lambda i: (i, 0)),
            pl.BlockSpec((tm, d), lambda i: (i, 0)),
            _resident((3, d)),
            _resident((d, d)),
            _resident((1, d)),
            _resident((d, f)),
            _resident((d, f)),
            _resident((f, d)),
        ],
        out_specs=pl.BlockSpec((tm, d), lambda i: (i, 0)),
        out_shape=jax.ShapeDtypeStruct((t, d), F32),
        compiler_params=_cparams("parallel"),
        name="conv_out_ffn",
    )(u, u, u, b, x, conv_w, w_out, g_ffn, w1, w3, w2)


def _qkv_kernel(x_ref, g_ref, w_ref, pos_ref, freq_ref, qn_ref, kn_ref, q_ref, k_ref, v_ref, wt_ref):
    tm, d = x_ref.shape

    @pl.when(pl.program_id(0) == 0)
    def _():
        for j in range(3):
            wt_ref[j * d:(j + 1) * d, :] = w_ref[:, j * d:(j + 1) * d].T.astype(BF16)

    h = _rms_rows(x_ref[...], g_ref[...]).astype(BF16)
    nt = (((1,), (1,)), ((), ()))
    ang = pos_ref[...].astype(F32) * freq_ref[...]
    cos = jnp.cos(ang)[None]
    sin = jnp.sin(ang)[None]
    half = ROT_DIM // 2
    groups = d // HEAD_DIM

    def norm_rope(w_rows, gain, scale):
        t = lax.dot_general(w_rows, h, nt, preferred_element_type=F32)
        t = t.reshape(groups, HEAD_DIM, tm)
        ms = jnp.mean(t * t, axis=1, keepdims=True)
        t = t * lax.rsqrt(ms + RMS_EPS) * gain[None]
        t1 = t[:, 0:half, :]
        t2 = t[:, half:ROT_DIM, :]
        t = jnp.concatenate([t1 * cos - t2 * sin, t2 * cos + t1 * sin, t[:, ROT_DIM:, :]], axis=1)
        return (t * scale).reshape(d, tm)

    q = norm_rope(wt_ref[0:d, :], qn_ref[...], (HEAD_DIM ** -0.5) * LOG2E)
    q_ref[0] = q.astype(BF16)
    k = norm_rope(wt_ref[d:2 * d, :], kn_ref[...], 1.0)
    k_ref[...] = k.T.astype(BF16)
    v = lax.dot_general(wt_ref[2 * d:3 * d, :], h, nt, preferred_element_type=F32)
    v_ref[0] = v.astype(BF16)


def _attn_kernel(lambda_init, unshifted_ref, q_ref, k_ref, v_ref, lam_ref, sub_ref, o_ref,
                 acc_ref, p_ref):
    nqt, _, tq = q_ref.shape
    nkb = v_ref.shape[0]
    tk = v_ref.shape[2]
    rows = lax.broadcasted_iota(jnp.int32, (V_DIM, 1), 0)
    acc_ref[...] = jnp.zeros_like(acc_ref)
    zer = jnp.zeros((1, tq), F32)

    def split_q(t):
        q = q_ref[t]
        zero = jnp.zeros_like(q)
        return jnp.where(rows < HEAD_DIM, q, zero), jnp.where(rows >= HEAD_DIM, q, zero)

    def load_kv(kb):
        k = k_ref[pl.ds(pl.multiple_of(kb * tk, tk), tk), :]
        return k, v_ref[kb]

    def finish(t, l1, l2):
        lp = lam_ref[...]
        lam = (jnp.exp(jnp.sum(lp[0:1, :] * lp[1:2, :], axis=-1, keepdims=True))
               - jnp.exp(jnp.sum(lp[2:3, :] * lp[3:4, :], axis=-1, keepdims=True))
               + lambda_init)
        o = acc_ref[t, 0] / l1 - lam * (acc_ref[t, 1] / l2)
        ms = jnp.mean(o * o, axis=0, keepdims=True)
        o = o * lax.rsqrt(ms + RMS_EPS) * sub_ref[...] * (1.0 - lambda_init)
        o_ref[t] = o.astype(BF16)

    @pl.when(unshifted_ref[0] == 1)
    def _():
        ts = p_ref.shape[2]
        per = tk // ts
        nks = nkb * per

        def scores(qs, j, l, slot):
            k = k_ref[j * ts:(j + 1) * ts, :]
            ps = [jnp.exp2(_dot(k, qc)) for qc in qs]
            for c, p in enumerate(ps):
                p_ref[slot, c] = p.astype(BF16)
            return [lc + jnp.sum(p, axis=0, keepdims=True) for lc, p in zip(l, ps)]

        def values(t, j, slot):
            v = v_ref[j // per][:, (j % per) * ts:(j % per + 1) * ts]
            for c in (0, 1):
                acc_ref[t, c] += _dot(v, p_ref[slot, c])

        stage = 0
        pending = None
        for t in range(nqt):
            qs = split_q(t)
            l = [zer, zer]
            for j in range(nks):
                l = scores(qs, j, l, stage % 2)
                if pending is not None:
                    values(*pending)
                    if pending[1] == nks - 1:
                        finish(pending[0], *l_done)
                pending = (t, j, stage % 2)
                stage += 1
            l_done = l
        values(*pending)
        finish(pending[0], *l_done)

    @pl.when(unshifted_ref[0] == 0)
    def _():
        def one_map(k, v, qc, m, l, t, c):
            s = _dot(k, qc)
            m_new = jnp.maximum(m, jnp.max(s, axis=0, keepdims=True))
            alpha = jnp.exp2(m - m_new)
            p = jnp.exp2(s - m_new)
            acc_ref[t, c] = alpha * acc_ref[t, c] + _dot(v, p.astype(BF16))
            return m_new, alpha * l + jnp.sum(p, axis=0, keepdims=True)

        neg = jnp.full((1, tq), _NEG_BIG, F32)
        for t in range(nqt):
            q1, q2 = split_q(t)

            def body(kb, carry):
                k, v = load_kv(kb)
                m1, l1 = one_map(k, v, q1, carry[0], carry[1], t, 0)
                m2, l2 = one_map(k, v, q2, carry[2], carry[3], t, 1)
                return m1, l1, m2, l2

            _, l1, _, l2 = lax.fori_loop(0, nkb, body, (neg, zer, neg, zer))
            finish(t, l1, l2)


def _store_row_tiles(ref, x):
    rows = x.shape[0]
    for j in range(ROW_SUBLANES):
        ref[pl.ds(j, rows, stride=ROW_SUBLANES), :] = x[:, j * LANES:(j + 1) * LANES]


def _load_row_tiles(ref, base, rows):
    return jnp.concatenate(
        [ref[pl.ds(base + j, rows, stride=ROW_SUBLANES), :] for j in range(ROW_SUBLANES)], axis=1)


def _attn_out_router_kernel(o_ref, x_ref, w_ref, g_ref, rt_ref, x2_ref, h2_ref, idx_ref, gate_ref):
    tn = (((0,), (0,)), ((), ()))
    y = lax.dot_general(o_ref[0], w_ref[...], tn, preferred_element_type=F32)
    x2 = x_ref[...] + y
    x2_ref[...] = x2
    h2 = _rms_rows(x2, g_ref[...])
    _store_row_tiles(h2_ref, h2)
    ne = rt_ref.shape[0]
    nt = (((1,), (1,)), ((), ()))
    r = rt_ref[...]
    r_hi = r.astype(BF16)
    r_lo = (r - r_hi.astype(F32)).astype(BF16)
    h_hi = h2.astype(BF16)
    h_lo = (h2 - h_hi.astype(F32)).astype(BF16)
    both = lax.dot_general(jnp.concatenate([r_hi, r_lo], axis=0), h_hi, nt, preferred_element_type=F32)
    logits = (both[0:ne] + both[ne:2 * ne]
              + lax.dot_general(r_hi, h_lo, nt, preferred_element_type=F32))
    eid = lax.broadcasted_iota(jnp.int32, logits.shape, 0)
    m1 = jnp.max(logits, axis=0, keepdims=True)
    i1 = jnp.min(jnp.where(logits == m1, eid, ne), axis=0, keepdims=True)
    rest = jnp.where(eid == i1, -jnp.inf, logits)
    m2 = jnp.max(rest, axis=0, keepdims=True)
    i2 = jnp.min(jnp.where(rest == m2, eid, ne), axis=0, keepdims=True)
    e2 = jnp.exp(m2 - m1)
    den = 1.0 + e2
    idx_ref[0:1, :] = i1
    idx_ref[1:2, :] = i2
    gate_ref[0:1, :] = 1.0 / den
    gate_ref[1:2, :] = e2 / den


def _diff_attention_and_router(x, positions, g_mix, w_in, q_norm, k_norm, lam_params, subln, w_out,
                               g_ffn, router, lambda_init, batch, seq):
    t, d = x.shape
    tm = TOKEN_TILE
    nt = t // tm
    inv_freq = ROPE_THETA ** (-jnp.arange(0, ROT_DIM, 2, dtype=F32) / ROT_DIM)
    q, k, v = pl.pallas_call(
        _qkv_kernel,
        grid=(nt,),
        in_specs=[
            pl.BlockSpec((tm, d), lambda i: (i, 0)),
            _resident((1, d)),
            _resident((d, 3 * d)),
            pl.BlockSpec((1, tm), lambda i: (0, i)),
            _resident((ROT_DIM // 2, 1)),
            _resident((HEAD_DIM, 1)),
            _resident((HEAD_DIM, 1)),
        ],
        out_specs=[
            pl.BlockSpec((1, d, tm), lambda i: (i, 0, 0)),
            pl.BlockSpec((tm, d), lambda i: (i, 0)),
            pl.BlockSpec((1, d, tm), lambda i: (i, 0, 0)),
        ],
        out_shape=[
            jax.ShapeDtypeStruct((nt, d, tm), BF16),
            jax.ShapeDtypeStruct((t, d), BF16),
            jax.ShapeDtypeStruct((nt, d, tm), BF16),
        ],
        scratch_shapes=[pltpu.VMEM((3 * d, d), BF16)],
        compiler_params=_cparams("arbitrary"),
        name="attn_qkv",
    )(x, g_mix, w_in, positions.reshape(1, t), inv_freq.reshape(-1, 1),
      q_norm.reshape(-1, 1), k_norm.reshape(-1, 1))

    nq = seq // tm
    qg = ATTN_Q_TILES
    ngrp = nq // qg
    score_bound = (HEAD_DIM ** 0.5 * LOG2E * _BF16_ROUNDING_MARGIN
                   * jnp.max(jnp.abs(q_norm)) * jnp.max(jnp.abs(k_norm)))
    unshifted = (score_bound <= _EXP2_SAFE_LOG2).astype(jnp.int32).reshape(1)
    o = pl.pallas_call(
        functools.partial(_attn_kernel, lambda_init),
        grid_spec=pltpu.PrefetchScalarGridSpec(
            num_scalar_prefetch=1,
            grid=(batch, N_HEADS, ngrp),
            in_specs=[
                pl.BlockSpec((qg, V_DIM, tm), lambda b, h, i, f: (b * ngrp + i, h, 0)),
                pl.BlockSpec((seq, V_DIM), lambda b, h, i, f: (b, h)),
                pl.BlockSpec((nq, V_DIM, tm), lambda b, h, i, f: (b, h, 0)),
                pl.BlockSpec((4, HEAD_DIM), lambda b, h, i, f: (0, 0)),
                pl.BlockSpec((V_DIM, 1), lambda b, h, i, f: (0, 0)),
            ],
            out_specs=pl.BlockSpec((qg, V_DIM, tm), lambda b, h, i, f: (b * ngrp + i, h, 0)),
            scratch_shapes=[pltpu.VMEM((qg, 2, V_DIM, tm), F32),
                            pltpu.VMEM((2, 2, ATTN_KV_TILE, tm), BF16)],
        ),
        out_shape=jax.ShapeDtypeStruct((nt, d, tm), BF16),
        compiler_params=_cparams("parallel", "parallel", "arbitrary"),
        name="diff_attn",
    )(unshifted, q, k, v, lam_params, subln.reshape(-1, 1))

    x2, h2, idx, gates = pl.pallas_call(
        _attn_out_router_kernel,
        grid=(nt,),
        in_specs=[
            pl.BlockSpec((1, d, tm), lambda i: (i, 0, 0)),
            pl.BlockSpec((tm, d), lambda i: (i, 0)),
            _resident((d, d)),
            _resident((1, d)),
            _resident((N_EXPERTS, d)),
        ],
        out_specs=[
            pl.BlockSpec((tm, d), lambda i: (i, 0)),
            pl.BlockSpec((tm * ROW_SUBLANES, LANES), lambda i: (i, 0)),
            pl.BlockSpec((2, tm), lambda i: (0, i)),
            pl.BlockSpec((2, tm), lambda i: (0, i)),
        ],
        out_shape=[
            jax.ShapeDtypeStruct((t, d), F32),
            jax.ShapeDtypeStruct((t * ROW_SUBLANES, LANES), F32),
            jax.ShapeDtypeStruct((2, t), jnp.int32),
            jax.ShapeDtypeStruct((2, t), F32),
        ],
        compiler_params=_cparams("parallel"),
        name="attn_out_router",
    )(o, x, w_out, g_ffn, router.T)
    return x2, h2, idx, gates.T


def _routing_plan(idx, n_tiles):
    experts = jnp.arange(N_EXPERTS, dtype=jnp.int32)[:, None]
    chosen = [(idx[c][None, :] == experts).astype(jnp.int32) for c in (0, 1)]
    hot = chosen[0] + chosen[1]
    incl = jnp.cumsum(hot, axis=1)
    rank = incl - hot
    tiles = (incl[:, -1] + MOE_TILE - 1) // MOE_TILE
    tile_end = jnp.cumsum(tiles)
    row_start = (tile_end - tiles) * MOE_TILE
    slot = row_start[:, None] + rank
    pos = jnp.stack([jnp.sum(slot * one_hot, axis=0) for one_hot in chosen])
    tile_id = jnp.arange(n_tiles, dtype=jnp.int32)
    tile_expert = jnp.minimum(jnp.sum(tile_id[:, None] >= tile_end[None, :], axis=1), N_EXPERTS - 1)
    new_expert = jnp.concatenate([jnp.ones((1,), bool), tile_expert[1:] != tile_expert[:-1]])
    rows_left = incl[:, -1][tile_expert] - (tile_id - (tile_end - tiles)[tile_expert]) * MOE_TILE
    half_full = (~new_expert) & (rows_left <= MOE_TILE // 2)
    tile_state = jnp.where(tile_id < tile_end[-1],
                           jnp.where(half_full, 3, 1 + new_expert.astype(jnp.int32)), 0)
    last_tile = jnp.where(tiles > 0, tile_end - 1, -1)
    tail = tile_end[-1] + jnp.arange(N_EXPERTS, dtype=jnp.int32)
    pad_tiles = jnp.concatenate([last_tile, jnp.where(tail < n_tiles, tail, -1)])
    return (pos.astype(jnp.int32), tile_expert.astype(jnp.int32), tile_state.astype(jnp.int32),
            pad_tiles.astype(jnp.int32))


def _row_tile(ref, row):
    return ref.at[pl.ds(pl.multiple_of(row * ROW_SUBLANES, ROW_SUBLANES), ROW_SUBLANES)]


def _dispatch_kernel(pad_ref, pos_ref, h_hbm, xs_hbm, zero_ref, ring, load_sem, row_sem):
    i = pl.program_id(0)
    last = pl.num_programs(0) - 1
    n = pos_ref.shape[2] // 2
    mt = zero_ref.shape[0]
    nslot = ring.shape[0]
    block_rows = n * ROW_SUBLANES

    def load(step):
        src = h_hbm.at[pl.ds(pl.multiple_of(step * block_rows, block_rows), block_rows)]
        return pltpu.make_async_copy(src, ring.at[step % nslot], load_sem.at[step % nslot])

    def wait_rows(step):
        for _ in (0, 1):
            pltpu.make_async_copy(ring.at[0], xs_hbm.at[pl.ds(0, block_rows)],
                                  row_sem.at[step % nslot]).wait()

    @pl.when(i == 0)
    def _():
        load(0).start()
        zero_ref[...] = jnp.zeros_like(zero_ref)
        fills = [pltpu.make_async_copy(
            zero_ref, xs_hbm.at[pl.ds(pl.multiple_of(jnp.maximum(pad_ref[j], 0) * mt, mt), mt)],
            row_sem.at[0]) for j in range(pad_ref.shape[0])]
        for j, fill in enumerate(fills):
            pl.when(pad_ref[j] >= 0)(fill.start)
        for j, fill in enumerate(fills):
            pl.when(pad_ref[j] >= 0)(fill.wait)

    pl.when(i < last)(lambda: load(i + 1).start())
    load(i).wait()
    block = ring.at[i % nslot]

    def issue(r, carry):
        for c in (0, 1):
            pltpu.make_async_copy(_row_tile(block, r), _row_tile(xs_hbm, pos_ref[0, 0, c * n + r]),
                                  row_sem.at[i % nslot]).start(priority=c)
        return carry

    lax.fori_loop(0, n, issue, 0, unroll=8)
    pl.when(i > 0)(lambda: wait_rows(i - 1))
    pl.when(i == last)(lambda: wait_rows(i))


def _expert_kernel(te_ref, ts_ref, xs_ref, w1_hbm, w3_hbm, w2_hbm, y_ref, w1_ref, w3_ref, w2_ref, sem):
    i = pl.program_id(0)
    mt = xs_ref.shape[0] // ROW_SUBLANES
    chunks = _ff_chunks(w1_ref.shape[1])

    def swiglu_tile(before_chunk, rows=mt):
        x = _load_row_tiles(xs_ref, 0, rows)
        acc = None
        for c, (f0, fl) in enumerate(chunks):
            before_chunk(c)
            a = _dot(x, w1_ref[:, f0:f0 + fl])
            b = _dot(x, w3_ref[:, f0:f0 + fl])
            act = _silu(a) * b
            part = _dot(act, w2_ref[f0:f0 + fl, :])
            acc = part if acc is None else acc + part
        _store_row_tiles(y_ref.at[0:rows * ROW_SUBLANES], acc)
        if rows < mt:
            y_ref[rows * ROW_SUBLANES:, :] = jnp.zeros(((mt - rows) * ROW_SUBLANES, LANES), F32)

    @pl.when(ts_ref[i] == 2)
    def _():
        e = te_ref[i]
        fetch = []
        for c, (f0, fl) in enumerate(chunks):
            cols = pl.ds(f0, fl)
            fetch.append([
                pltpu.make_async_copy(w1_hbm.at[e, :, cols], w1_ref.at[:, cols], sem.at[c]),
                pltpu.make_async_copy(w3_hbm.at[e, :, cols], w3_ref.at[:, cols], sem.at[c]),
                pltpu.make_async_copy(w2_hbm.at[e, cols, :], w2_ref.at[cols, :], sem.at[c]),
            ])
        for copies in fetch:
            for copy in copies:
                copy.start()

        def wait_chunk(c):
            for copy in fetch[c]:
                copy.wait()

        swiglu_tile(wait_chunk)

    @pl.when(ts_ref[i] == 1)
    def _():
        swiglu_tile(lambda c: None)

    @pl.when(ts_ref[i] == 3)
    def _():
        swiglu_tile(lambda c: None, rows=mt // 2)

    @pl.when(ts_ref[i] == 0)
    def _():
        y_ref[...] = jnp.zeros_like(y_ref)


def _combine_kernel(pos_ref, pos_next_ref, x_ref, g_ref, y_hbm, o_ref, buf, sem):
    i = pl.program_id(0)
    n = x_ref.shape[0]

    def gather(p_ref, slot):
        def issue(r, carry):
            for c in (0, 1):
                pltpu.make_async_copy(_row_tile(y_hbm, p_ref[0, 0, c * n + r]),
                                      _row_tile(buf, (slot * 2 + c) * n + r),
                                      sem.at[slot]).start(priority=c)
            return carry
        lax.fori_loop(0, n, issue, 0, unroll=8)

    @pl.when(i == 0)
    def _():
        gather(pos_ref, 0)

    @pl.when(i + 1 < pl.num_programs(0))
    def _():
        gather(pos_next_ref, (i + 1) % 2)

    slot = i % 2
    chunk = n * ROW_SUBLANES
    base = pl.multiple_of(slot * 2 * chunk, chunk)
    pltpu.make_async_copy(y_hbm.at[pl.ds(0, 2 * chunk)], buf.at[pl.ds(base, 2 * chunk)],
                          sem.at[slot]).wait()
    g = g_ref[...]
    o_ref[...] = (x_ref[...] + g[:, 0:1] * _load_row_tiles(buf, base, n)
                  + g[:, 1:2] * _load_row_tiles(buf, base + chunk, n))


def _moe(x2, h2, idx, gates, w1, w3, w2):
    t, d = x2.shape
    fe = w1.shape[2]
    assert d == ROW_SUBLANES * LANES
    rs = ROW_SUBLANES
    n_tiles = (2 * t) // MOE_TILE + N_EXPERTS
    rows = n_tiles * MOE_TILE
    pos, tile_expert, tile_state, pad_tiles = _routing_plan(idx, n_tiles)
    gt = GATHER_TILE
    ng = t // gt
    pos_blocks = pos.reshape(2, ng, gt).transpose(1, 0, 2).reshape(ng, 1, 2 * gt)

    xs = pl.pallas_call(
        _dispatch_kernel,
        grid_spec=pltpu.PrefetchScalarGridSpec(
            num_scalar_prefetch=1,
            grid=(ng,),
            in_specs=[
                pl.BlockSpec((1, 1, 2 * gt), lambda i, pad: (i, 0, 0), memory_space=pltpu.SMEM),
                pl.BlockSpec(memory_space=pl.ANY),
            ],
            out_specs=pl.BlockSpec(memory_space=pl.ANY),
            scratch_shapes=[pltpu.VMEM((MOE_TILE * rs, LANES), F32),
                            pltpu.VMEM((DISPATCH_RING, gt * rs, LANES), F32),
                            pltpu.SemaphoreType.DMA((DISPATCH_RING,)),
                            pltpu.SemaphoreType.DMA((DISPATCH_RING,))],
        ),
        out_shape=jax.ShapeDtypeStruct((rows * rs, LANES), F32),
        compiler_params=_cparams("arbitrary"),
        name="moe_dispatch",
    )(pad_tiles, pos_blocks, h2)

    y = pl.pallas_call(
        _expert_kernel,
        grid_spec=pltpu.PrefetchScalarGridSpec(
            num_scalar_prefetch=2,
            grid=(n_tiles,),
            in_specs=[
                pl.BlockSpec((MOE_TILE * rs, LANES), lambda i, te, ts: (i, 0)),
                pl.BlockSpec(memory_space=pl.ANY),
                pl.BlockSpec(memory_space=pl.ANY),
                pl.BlockSpec(memory_space=pl.ANY),
            ],
            out_specs=pl.BlockSpec((MOE_TILE * rs, LANES), lambda i, te, ts: (i, 0)),
            scratch_shapes=[pltpu.VMEM((d, fe), F32), pltpu.VMEM((d, fe), F32),
                            pltpu.VMEM((fe, d), F32),
                            pltpu.SemaphoreType.DMA((len(_ff_chunks(fe)),))],
        ),
        out_shape=jax.ShapeDtypeStruct((rows * rs, LANES), F32),
        compiler_params=_cparams("arbitrary", vmem_limit=EXPERT_VMEM_LIMIT),
        name="moe_experts",
    )(tile_expert, tile_state, xs, w1, w3, w2)

    return pl.pallas_call(
        _combine_kernel,
        grid=(ng,),
        in_specs=[
            pl.BlockSpec((1, 1, 2 * gt), lambda i: (i, 0, 0), memory_space=pltpu.SMEM),
            pl.BlockSpec((1, 1, 2 * gt), lambda i: (jnp.minimum(i + 1, ng - 1), 0, 0),
                         memory_space=pltpu.SMEM),
            pl.BlockSpec((gt, d), lambda i: (i, 0)),
            pl.BlockSpec((gt, 2), lambda i: (i, 0)),
            pl.BlockSpec(memory_space=pl.ANY),
        ],
        out_specs=pl.BlockSpec((gt, d), lambda i: (i, 0)),
        out_shape=jax.ShapeDtypeStruct((t, d), F32),
        scratch_shapes=[pltpu.VMEM((2 * 2 * gt * rs, LANES), F32), pltpu.SemaphoreType.DMA((2,))],
        compiler_params=_cparams("arbitrary"),
        name="moe_combine",
    )(pos_blocks, pos_blocks, x2, gates, y)


def kernel(x, positions, norm_mix, norm_ffn, conv_in, conv_w, conv_out, attn_in, q_norm, k_norm,
           lam_q1, lam_k1, lam_q2, lam_k2, subln, attn_out, ffn_w1, ffn_w3, ffn_w2,
           router, moe_w1, moe_w3, moe_w2):
    batch, seq, d = x.shape
    xt = x.reshape(batch * seq, d)
    bf = lambda w: w.astype(BF16)

    xt = _layer0(xt, norm_mix[0:1], conv_in[0], conv_w[0], bf(conv_out[0]),
                 norm_ffn[0:1], ffn_w1[0], ffn_w3[0], ffn_w2[0], seq)

    lambda_init = 0.8 - 0.6 * math.exp(-0.3 * 1)
    lam_params = jnp.concatenate([lam_q1, lam_k1, lam_q2, lam_k2], axis=0)
    x2, h2, idx, gates = _diff_attention_and_router(
        xt, positions, norm_mix[1:2], attn_in[0], q_norm[0], k_norm[0], lam_params, subln[0],
        bf(attn_out[0]), norm_ffn[1:2], router[0], lambda_init, batch, seq)
    out = _moe(x2, h2, idx, gates, moe_w1[0], moe_w3[0], moe_w2[0])
    return out.reshape(batch, seq, d)
```

```python
import functools
import math

import jax
import jax.numpy as jnp
from jax import lax
from jax.experimental import pallas as pl
from jax.experimental.pallas import tpu as pltpu

F32 = jnp.float32
BF16 = jnp.bfloat16

D_MODEL = 1024
N_HEADS = 8
HEAD_DIM = 64
V_DIM = 2 * HEAD_DIM
ROT_DIM = HEAD_DIM // 4
ROPE_THETA = 500000.0
N_EXPERTS = 8
RMS_EPS = 1e-6
LOG2E = 1.4426950408889634
LANES = 128
ROW_SUBLANES = 8

TOKEN_TILE = 512
ATTN_KV_TILE = 256
ATTN_Q_TILES = 2
MOE_TILE = 512
FF_CHUNK = 512
GATHER_TILE = 512
DISPATCH_RING = 3
VMEM_LIMIT = 56 * 1024 * 1024
EXPERT_VMEM_LIMIT = 62 * 1024 * 1024

_NEG_BIG = -1e30
_EXP2_SAFE_LOG2 = 60.0
_BF16_ROUNDING_MARGIN = 1.02


def _cparams(*sem, vmem_limit=VMEM_LIMIT):
    return pltpu.CompilerParams(dimension_semantics=sem, vmem_limit_bytes=vmem_limit)


def _rms_rows(x, g):
    ms = jnp.mean(x * x, axis=-1, keepdims=True)
    return x * lax.rsqrt(ms + RMS_EPS) * g


def _dot(a, b):
    return jnp.dot(a, b, preferred_element_type=F32)


def _silu(a):
    return a * (1.0 / (1.0 + jnp.exp(-a)))


def _ff_chunks(width):
    out, f0 = [], 0
    while f0 < width:
        fl = min(FF_CHUNK, width - f0)
        out.append((f0, fl))
        f0 += fl
    return out


def _conv_in_kernel(x_ref, g_ref, w_ref, b_ref, u_ref):
    d = x_ref.shape[1]
    h = _rms_rows(x_ref[...], g_ref[...])
    b_ref[...] = _dot(h, w_ref[:, 0:d]).astype(BF16)
    c = _dot(h, w_ref[:, d:2 * d])
    v = _dot(h, w_ref[:, 2 * d:3 * d])
    u_ref[...] = (c * v).astype(BF16)


def _conv_out_ffn_kernel(tiles_per_seq, u_ref, up_ref, un_ref, b_ref, x_ref, cw_ref, w_ref,
                         g_ref, w1_ref, w3_ref, w2_ref, o_ref):
    i = pl.program_id(0)
    tm = u_ref.shape[0]
    halo = up_ref.shape[0]
    u = u_ref[...].astype(F32)
    first = (i % tiles_per_seq) == 0
    last = (i % tiles_per_seq) == tiles_per_seq - 1
    prev_row = jnp.where(first, 0.0, up_ref[halo - 1:halo, :].astype(F32))
    next_row = jnp.where(last, 0.0, un_ref[0:1, :].astype(F32))
    rows = lax.broadcasted_iota(jnp.int32, (tm, 1), 0)
    u_m1 = jnp.where(rows == 0, prev_row, pltpu.roll(u, 1, 0))
    u_p1 = jnp.where(rows == tm - 1, next_row, pltpu.roll(u, tm - 1, 0))
    cw = cw_ref[...]
    conv = u_m1 * cw[0:1, :] + u * cw[1:2, :] + u_p1 * cw[2:3, :]
    y = (b_ref[...].astype(F32) * conv).astype(BF16)
    x1 = x_ref[...] + _dot(y, w_ref[...])
    h = _rms_rows(x1, g_ref[...])
    acc = x1
    for f0, fl in _ff_chunks(w1_ref.shape[1]):
        a = _dot(h, w1_ref[:, f0:f0 + fl])
        b = _dot(h, w3_ref[:, f0:f0 + fl])
        acc = acc + _dot(_silu(a) * b, w2_ref[f0:f0 + fl, :])
    o_ref[...] = acc


def _resident(shape):
    return pl.BlockSpec(shape, lambda i: (0,) * len(shape), pipeline_mode=pl.Buffered(1))


def _layer0(x, g_mix, w_in, conv_w, w_out, g_ffn, w1, w3, w2, seq):
    t, d = x.shape
    f = w1.shape[1]
    tm = TOKEN_TILE
    nt = t // tm
    halo = 16
    b, u = pl.pallas_call(
        _conv_in_kernel,
        grid=(nt,),
        in_specs=[
            pl.BlockSpec((tm, d), lambda i: (i, 0)),
            _resident((1, d)),
            _resident((d, 3 * d)),
        ],
        out_specs=[pl.BlockSpec((tm, d), lambda i: (i, 0))] * 2,
        out_shape=[jax.ShapeDtypeStruct((t, d), BF16)] * 2,
        compiler_params=_cparams("parallel"),
        name="conv_in",
    )(x, g_mix, w_in)
    hb = tm // halo
    nhb = t // halo
    return pl.pallas_call(
        functools.partial(_conv_out_ffn_kernel, seq // tm),
        grid=(nt,),
        in_specs=[
            pl.BlockSpec((tm, d), lambda i: (i, 0)),
            pl.BlockSpec((halo, d), lambda i: (jnp.maximum(i * hb - 1, 0), 0)),
            pl.BlockSpec((halo, d), lambda i: (jnp.minimum((i + 1) * hb, nhb - 1), 0)),
            pl.BlockSpec((tm, d), lambda i: (i, 0)),
            pl.BlockSpec((tm, d), lambda i: (i, 0)),
            _resident((3, d)),
            _resident((d, d)),
            _resident((1, d)),
            _resident((d, f)),
            _resident((d, f)),
            _resident((f, d)),
        ],
        out_specs=pl.BlockSpec((tm, d), lambda i: (i, 0)),
        out_shape=jax.ShapeDtypeStruct((t, d), F32),
        compiler_params=_cparams("parallel"),
        name="conv_out_ffn",
    )(u, u, u, b, x, conv_w, w_out, g_ffn, w1, w3, w2)


def _qkv_kernel(x_ref, g_ref, w_ref, pos_ref, freq_ref, qn_ref, kn_ref, q_ref, k_ref, v_ref, wt_ref):
    tm, d = x_ref.shape

    @pl.when(pl.program_id(0) == 0)
    def _():
        for j in range(3):
            wt_ref[j * d:(j + 1) * d, :] = w_ref[:, j * d:(j + 1) * d].T.astype(BF16)

    h = _rms_rows(x_ref[...], g_ref[...]).astype(BF16)
    nt = (((1,), (1,)), ((), ()))
    ang = pos_ref[...].astype(F32) * freq_ref[...]
    cos = jnp.cos(ang)[None]
    sin = jnp.sin(ang)[None]
    half = ROT_DIM // 2
    groups = d // HEAD_DIM

    def norm_rope(w_rows, gain, scale):
        t = lax.dot_general(w_rows, h, nt, preferred_element_type=F32)
        t = t.reshape(groups, HEAD_DIM, tm)
        ms = jnp.mean(t * t, axis=1, keepdims=True)
        t = t * lax.rsqrt(ms + RMS_EPS) * gain[None]
        t1 = t[:, 0:half, :]
        t2 = t[:, half:ROT_DIM, :]
        t = jnp.concatenate([t1 * cos - t2 * sin, t2 * cos + t1 * sin, t[:, ROT_DIM:, :]], axis=1)
        return (t * scale).reshape(d, tm)

    q = norm_rope(wt_ref[0:d, :], qn_ref[...], (HEAD_DIM ** -0.5) * LOG2E)
    q_ref[0] = q.astype(BF16)
    k = norm_rope(wt_ref[d:2 * d, :], kn_ref[...], 1.0)
    k_ref[...] = k.T.astype(BF16)
    v = lax.dot_general(wt_ref[2 * d:3 * d, :], h, nt, preferred_element_type=F32)
    v_ref[0] = v.astype(BF16)


def _attn_kernel(lambda_init, unshifted_ref, q_ref, k_ref, v_ref, lam_ref, sub_ref, o_ref,
                 acc_ref, p_ref):
    nqt, _, tq = q_ref.shape
    nkb = v_ref.shape[0]
    tk = v_ref.shape[2]
    rows = lax.broadcasted_iota(jnp.int32, (V_DIM, 1), 0)
    acc_ref[...] = jnp.zeros_like(acc_ref)
    zer = jnp.zeros((1, tq), F32)

    def split_q(t):
        q = q_ref[t]
        zero = jnp.zeros_like(q)
        return jnp.where(rows < HEAD_DIM, q, zero), jnp.where(rows >= HEAD_DIM, q, zero)

    def load_kv(kb):
        k = k_ref[pl.ds(pl.multiple_of(kb * tk, tk), tk), :]
        return k, v_ref[kb]

    def finish(t, l1, l2):
        lp = lam_ref[...]
        lam = (jnp.exp(jnp.sum(lp[0:1, :] * lp[1:2, :], axis=-1, keepdims=True))
               - jnp.exp(jnp.sum(lp[2:3, :] * lp[3:4, :], axis=-1, keepdims=True))
               + lambda_init)
        o = acc_ref[t, 0] / l1 - lam * (acc_ref[t, 1] / l2)
        ms = jnp.mean(o * o, axis=0, keepdims=True)
        o = o * lax.rsqrt(ms + RMS_EPS) * sub_ref[...] * (1.0 - lambda_init)
        o_ref[t] = o.astype(BF16)

    @pl.when(unshifted_ref[0] == 1)
    def _():
        ts = p_ref.shape[2]
        per = tk // ts
        nks = nkb * per

        def scores(qs, j, l, slot):
            k = k_ref[j * ts:(j + 1) * ts, :]
            ps = [jnp.exp2(_dot(k, qc)) for qc in qs]
            for c, p in enumerate(ps):
                p_ref[slot, c] = p.astype(BF16)
            return [lc + jnp.sum(p, axis=0, keepdims=True) for lc, p in zip(l, ps)]

        def values(t, j, slot):
            v = v_ref[j // per][:, (j % per) * ts:(j % per + 1) * ts]
            for c in (0, 1):
                acc_ref[t, c] += _dot(v, p_ref[slot, c])

        stage = 0
        pending = None
        for t in range(nqt):
            qs = split_q(t)
            l = [zer, zer]
            for j in range(nks):
                l = scores(qs, j, l, stage % 2)
                if pending is not None:
                    values(*pending)
                    if pending[1] == nks - 1:
                        finish(pending[0], *l_done)
                pending = (t, j, stage % 2)
                stage += 1
            l_done = l
        values(*pending)
        finish(pending[0], *l_done)

    @pl.when(unshifted_ref[0] == 0)
    def _():
        def one_map(k, v, qc, m, l, t, c):
            s = _dot(k, qc)
            m_new = jnp.maximum(m, jnp.max(s, axis=0, keepdims=True))
            alpha = jnp.exp2(m - m_new)
            p = jnp.exp2(s - m_new)
            acc_ref[t, c] = alpha * acc_ref[t, c] + _dot(v, p.astype(BF16))
            return m_new, alpha * l + jnp.sum(p, axis=0, keepdims=True)

        neg = jnp.full((1, tq), _NEG_BIG, F32)
        for t in range(nqt):
            q1, q2 = split_q(t)

            def body(kb, carry):
                k, v = load_kv(kb)
                m1, l1 = one_map(k, v, q1, carry[0], carry[1], t, 0)
                m2, l2 = one_map(k, v, q2, carry[2], carry[3], t, 1)
                return m1, l1, m2, l2

            _, l1, _, l2 = lax.fori_loop(0, nkb, body, (neg, zer, neg, zer))
            finish(t, l1, l2)


def _store_row_tiles(ref, x):
    rows = x.shape[0]
    for j in range(ROW_SUBLANES):
        ref[pl.ds(j, rows, stride=ROW_SUBLANES), :] = x[:, j * LANES:(j + 1) * LANES]


def _load_row_tiles(ref, base, rows):
    return jnp.concatenate(
        [ref[pl.ds(base + j, rows, stride=ROW_SUBLANES), :] for j in range(ROW_SUBLANES)], axis=1)


def _attn_out_router_kernel(o_ref, x_ref, w_ref, g_ref, rt_ref, x2_ref, h2_ref, idx_ref, gate_ref):
    tn = (((0,), (0,)), ((), ()))
    y = lax.dot_general(o_ref[0], w_ref[...], tn, preferred_element_type=F32)
    x2 = x_ref[...] + y
    x2_ref[...] = x2
    h2 = _rms_rows(x2, g_ref[...])
    _store_row_tiles(h2_ref, h2)
    ne = rt_ref.shape[0]
    nt = (((1,), (1,)), ((), ()))
    r = rt_ref[...]
    r_hi = r.astype(BF16)
    r_lo = (r - r_hi.astype(F32)).astype(BF16)
    h_hi = h2.astype(BF16)
    h_lo = (h2 - h_hi.astype(F32)).astype(BF16)
    both = lax.dot_general(jnp.concatenate([r_hi, r_lo], axis=0), h_hi, nt, preferred_element_type=F32)
    logits = (both[0:ne] + both[ne:2 * ne]
              + lax.dot_general(r_hi, h_lo, nt, preferred_element_type=F32))
    eid = lax.broadcasted_iota(jnp.int32, logits.shape, 0)
    m1 = jnp.max(logits, axis=0, keepdims=True)
    i1 = jnp.min(jnp.where(logits == m1, eid, ne), axis=0, keepdims=True)
    rest = jnp.where(eid == i1, -jnp.inf, logits)
    m2 = jnp.max(rest, axis=0, keepdims=True)
    i2 = jnp.min(jnp.where(rest == m2, eid, ne), axis=0, keepdims=True)
    e2 = jnp.exp(m2 - m1)
    den = 1.0 + e2
    idx_ref[0:1, :] = i1
    idx_ref[1:2, :] = i2
    gate_ref[0:1, :] = 1.0 / den
    gate_ref[1:2, :] = e2 / den


def _diff_attention_and_router(x, positions, g_mix, w_in, q_norm, k_norm, lam_params, subln, w_out,
                               g_ffn, router, lambda_init, batch, seq):
    t, d = x.shape
    tm = TOKEN_TILE
    nt = t // tm
    inv_freq = ROPE_THETA ** (-jnp.arange(0, ROT_DIM, 2, dtype=F32) / ROT_DIM)
    q, k, v = pl.pallas_call(
        _qkv_kernel,
        grid=(nt,),
        in_specs=[
            pl.BlockSpec((tm, d), lambda i: (i, 0)),
            _resident((1, d)),
            _resident((d, 3 * d)),
            pl.BlockSpec((1, tm), lambda i: (0, i)),
            _resident((ROT_DIM // 2, 1)),
            _resident((HEAD_DIM, 1)),
            _resident((HEAD_DIM, 1)),
        ],
        out_specs=[
            pl.BlockSpec((1, d, tm), lambda i: (i, 0, 0)),
            pl.BlockSpec((tm, d), lambda i: (i, 0)),
            pl.BlockSpec((1, d, tm), lambda i: (i, 0, 0)),
        ],
        out_shape=[
            jax.ShapeDtypeStruct((nt, d, tm), BF16),
            jax.ShapeDtypeStruct((t, d), BF16),
            jax.ShapeDtypeStruct((nt, d, tm), BF16),
        ],
        scratch_shapes=[pltpu.VMEM((3 * d, d), BF16)],
        compiler_params=_cparams("arbitrary"),
        name="attn_qkv",
    )(x, g_mix, w_in, positions.reshape(1, t), inv_freq.reshape(-1, 1),
      q_norm.reshape(-1, 1), k_norm.reshape(-1, 1))

    nq = seq // tm
    qg = ATTN_Q_TILES
    ngrp = nq // qg
    score_bound = (HEAD_DIM ** 0.5 * LOG2E * _BF16_ROUNDING_MARGIN
                   * jnp.max(jnp.abs(q_norm)) * jnp.max(jnp.abs(k_norm)))
    unshifted = (score_bound <= _EXP2_SAFE_LOG2).astype(jnp.int32).reshape(1)
    o = pl.pallas_call(
        functools.partial(_attn_kernel, lambda_init),
        grid_spec=pltpu.PrefetchScalarGridSpec(
            num_scalar_prefetch=1,
            grid=(batch, N_HEADS, ngrp),
            in_specs=[
                pl.BlockSpec((qg, V_DIM, tm), lambda b, h, i, f: (b * ngrp + i, h, 0)),
                pl.BlockSpec((seq, V_DIM), lambda b, h, i, f: (b, h)),
                pl.BlockSpec((nq, V_DIM, tm), lambda b, h, i, f: (b, h, 0)),
                pl.BlockSpec((4, HEAD_DIM), lambda b, h, i, f: (0, 0)),
                pl.BlockSpec((V_DIM, 1), lambda b, h, i, f: (0, 0)),
            ],
            out_specs=pl.BlockSpec((qg, V_DIM, tm), lambda b, h, i, f: (b * ngrp + i, h, 0)),
            scratch_shapes=[pltpu.VMEM((qg, 2, V_DIM, tm), F32),
                            pltpu.VMEM((2, 2, ATTN_KV_TILE, tm), BF16)],
        ),
        out_shape=jax.ShapeDtypeStruct((nt, d, tm), BF16),
        compiler_params=_cparams("parallel", "parallel", "arbitrary"),
        name="diff_attn",
    )(unshifted, q, k, v, lam_params, subln.reshape(-1, 1))

    x2, h2, idx, gates = pl.pallas_call(
        _attn_out_router_kernel,
        grid=(nt,),
        in_specs=[
            pl.BlockSpec((1, d, tm), lambda i: (i, 0, 0)),
            pl.BlockSpec((tm, d), lambda i: (i, 0)),
            _resident((d, d)),
            _resident((1, d)),
            _resident((N_EXPERTS, d)),
        ],
        out_specs=[
            pl.BlockSpec((tm, d), lambda i: (i, 0)),
            pl.BlockSpec((tm * ROW_SUBLANES, LANES), lambda i: (i, 0)),
            pl.BlockSpec((2, tm), lambda i: (0, i)),
            pl.BlockSpec((2, tm), lambda i: (0, i)),
        ],
        out_shape=[
            jax.ShapeDtypeStruct((t, d), F32),
            jax.ShapeDtypeStruct((t * ROW_SUBLANES, LANES), F32),
            jax.ShapeDtypeStruct((2, t), jnp.int32),
            jax.ShapeDtypeStruct((2, t), F32),
        ],
        compiler_params=_cparams("parallel"),
        name="attn_out_router",
    )(o, x, w_out, g_ffn, router.T)
    return x2, h2, idx, gates.T


def _routing_plan(idx, n_tiles):
    experts = jnp.arange(N_EXPERTS, dtype=jnp.int32)[:, None]
    chosen = [(idx[c][None, :] == experts).astype(jnp.int32) for c in (0, 1)]
    hot = chosen[0] + chosen[1]
    incl = jnp.cumsum(hot, axis=1)
    rank = incl - hot
    tiles = (incl[:, -1] + MOE_TILE - 1) // MOE_TILE
    tile_end = jnp.cumsum(tiles)
    row_start = (tile_end - tiles) * MOE_TILE
    slot = row_start[:, None] + rank
    pos = jnp.stack([jnp.sum(slot * one_hot, axis=0) for one_hot in chosen])
    tile_id = jnp.arange(n_tiles, dtype=jnp.int32)
    tile_expert = jnp.minimum(jnp.sum(tile_id[:, None] >= tile_end[None, :], axis=1), N_EXPERTS - 1)
    new_expert = jnp.concatenate([jnp.ones((1,), bool), tile_expert[1:] != tile_expert[:-1]])
    of_expert = (tile_expert[:, None] == experts.T).astype(jnp.int32)
    rows_left = (jnp.sum(of_expert * incl[:, -1][None, :], axis=1)
                 - (tile_id - jnp.sum(of_expert * (tile_end - tiles)[None, :], axis=1)) * MOE_TILE)
    half_full = (~new_expert) & (rows_left <= MOE_TILE // 2)
    tile_state = jnp.where(tile_id < tile_end[-1],
                           jnp.where(half_full, 3, 1 + new_expert.astype(jnp.int32)), 0)
    last_tile = jnp.where(tiles > 0, tile_end - 1, -1)
    tail = tile_end[-1] + jnp.arange(N_EXPERTS, dtype=jnp.int32)
    pad_tiles = jnp.concatenate([last_tile, jnp.where(tail < n_tiles, tail, -1)])
    return (pos.astype(jnp.int32), tile_expert.astype(jnp.int32), tile_state.astype(jnp.int32),
            pad_tiles.astype(jnp.int32))


def _row_tile(ref, row):
    return ref.at[pl.ds(pl.multiple_of(row * ROW_SUBLANES, ROW_SUBLANES), ROW_SUBLANES)]


def _dispatch_kernel(pad_ref, pos_ref, h_hbm, xs_hbm, zero_ref, ring, load_sem, row_sem):
    i = pl.program_id(0)
    last = pl.num_programs(0) - 1
    n = pos_ref.shape[2] // 2
    mt = zero_ref.shape[0]
    nslot = ring.shape[0]
    block_rows = n * ROW_SUBLANES

    def load(step):
        src = h_hbm.at[pl.ds(pl.multiple_of(step * block_rows, block_rows), block_rows)]
        return pltpu.make_async_copy(src, ring.at[step % nslot], load_sem.at[step % nslot])

    def wait_rows(step):
        for _ in (0, 1):
            pltpu.make_async_copy(ring.at[0], xs_hbm.at[pl.ds(0, block_rows)],
                                  row_sem.at[step % nslot]).wait()

    @pl.when(i == 0)
    def _():
        load(0).start()
        zero_ref[...] = jnp.zeros_like(zero_ref)
        fills = [pltpu.make_async_copy(
            zero_ref, xs_hbm.at[pl.ds(pl.multiple_of(jnp.maximum(pad_ref[j], 0) * mt, mt), mt)],
            row_sem.at[0]) for j in range(pad_ref.shape[0])]
        for j, fill in enumerate(fills):
            pl.when(pad_ref[j] >= 0)(fill.start)
        for j, fill in enumerate(fills):
            pl.when(pad_ref[j] >= 0)(fill.wait)

    pl.when(i < last)(lambda: load(i + 1).start())
    load(i).wait()
    block = ring.at[i % nslot]

    def issue(r, carry):
        for c in (0, 1):
            pltpu.make_async_copy(_row_tile(block, r), _row_tile(xs_hbm, pos_ref[0, 0, c * n + r]),
                                  row_sem.at[i % nslot]).start(priority=c)
        return carry

    lax.fori_loop(0, n, issue, 0, unroll=8)
    pl.when(i > 0)(lambda: wait_rows(i - 1))
    pl.when(i == last)(lambda: wait_rows(i))


def _expert_kernel(te_ref, ts_ref, xs_ref, w1_hbm, w3_hbm, w2_hbm, y_ref, w1_ref, w3_ref, w2_ref, sem):
    i = pl.program_id(0)
    mt = xs_ref.shape[0] // ROW_SUBLANES
    chunks = _ff_chunks(w1_ref.shape[1])

    def swiglu_tile(before_chunk, rows=mt):
        x = _load_row_tiles(xs_ref, 0, rows)
        acc = None
        for c, (f0, fl) in enumerate(chunks):
            before_chunk(c)
            a = _dot(x, w1_ref[:, f0:f0 + fl])
            b = _dot(x, w3_ref[:, f0:f0 + fl])
            act = _silu(a) * b
            part = _dot(act, w2_ref[f0:f0 + fl, :])
            acc = part if acc is None else acc + part
        _store_row_tiles(y_ref.at[0:rows * ROW_SUBLANES], acc)
        if rows < mt:
            y_ref[rows * ROW_SUBLANES:, :] = jnp.zeros(((mt - rows) * ROW_SUBLANES, LANES), F32)

    @pl.when(ts_ref[i] == 2)
    def _():
        e = te_ref[i]
        fetch = []
        for c, (f0, fl) in enumerate(chunks):
            cols = pl.ds(f0, fl)
            fetch.append([
                pltpu.make_async_copy(w1_hbm.at[e, :, cols], w1_ref.at[:, cols], sem.at[c]),
                pltpu.make_async_copy(w3_hbm.at[e, :, cols], w3_ref.at[:, cols], sem.at[c]),
                pltpu.make_async_copy(w2_hbm.at[e, cols, :], w2_ref.at[cols, :], sem.at[c]),
            ])
        for copies in fetch:
            for copy in copies:
                copy.start()

        def wait_chunk(c):
            for copy in fetch[c]:
                copy.wait()

        swiglu_tile(wait_chunk)

    @pl.when(ts_ref[i] == 1)
    def _():
        swiglu_tile(lambda c: None)

    @pl.when(ts_ref[i] == 3)
    def _():
        swiglu_tile(lambda c: None, rows=mt // 2)

    @pl.when(ts_ref[i] == 0)
    def _():
        y_ref[...] = jnp.zeros_like(y_ref)


def _combine_kernel(pos_ref, pos_next_ref, x_ref, g_ref, y_hbm, o_ref, buf, sem):
    i = pl.program_id(0)
    n = x_ref.shape[0]

    def gather(p_ref, slot):
        def issue(r, carry):
            for c in (0, 1):
                pltpu.make_async_copy(_row_tile(y_hbm, p_ref[0, 0, c * n + r]),
                                      _row_tile(buf, (slot * 2 + c) * n + r),
                                      sem.at[slot]).start(priority=c)
            return carry
        lax.fori_loop(0, n, issue, 0, unroll=8)

    @pl.when(i == 0)
    def _():
        gather(pos_ref, 0)

    @pl.when(i + 1 < pl.num_programs(0))
    def _():
        gather(pos_next_ref, (i + 1) % 2)

    slot = i % 2
    chunk = n * ROW_SUBLANES
    base = pl.multiple_of(slot * 2 * chunk, chunk)
    pltpu.make_async_copy(y_hbm.at[pl.ds(0, 2 * chunk)], buf.at[pl.ds(base, 2 * chunk)],
                          sem.at[slot]).wait()
    g = g_ref[...]
    o_ref[...] = (x_ref[...] + g[:, 0:1] * _load_row_tiles(buf, base, n)
                  + g[:, 1:2] * _load_row_tiles(buf, base + chunk, n))


def _moe(x2, h2, idx, gates, w1, w3, w2):
    t, d = x2.shape
    fe = w1.shape[2]
    assert d == ROW_SUBLANES * LANES
    rs = ROW_SUBLANES
    n_tiles = (2 * t) // MOE_TILE + N_EXPERTS
    rows = n_tiles * MOE_TILE
    pos, tile_expert, tile_state, pad_tiles = _routing_plan(idx, n_tiles)
    gt = GATHER_TILE
    ng = t // gt
    pos_blocks = pos.reshape(2, ng, gt).transpose(1, 0, 2).reshape(ng, 1, 2 * gt)

    xs = pl.pallas_call(
        _dispatch_kernel,
        grid_spec=pltpu.PrefetchScalarGridSpec(
            num_scalar_prefetch=1,
            grid=(ng,),
            in_specs=[
                pl.BlockSpec((1, 1, 2 * gt), lambda i, pad: (i, 0, 0), memory_space=pltpu.SMEM),
                pl.BlockSpec(memory_space=pl.ANY),
            ],
            out_specs=pl.BlockSpec(memory_space=pl.ANY),
            scratch_shapes=[pltpu.VMEM((MOE_TILE * rs, LANES), F32),
                            pltpu.VMEM((DISPATCH_RING, gt * rs, LANES), F32),
                            pltpu.SemaphoreType.DMA((DISPATCH_RING,)),
                            pltpu.SemaphoreType.DMA((DISPATCH_RING,))],
        ),
        out_shape=jax.ShapeDtypeStruct((rows * rs, LANES), F32),
        compiler_params=_cparams("arbitrary"),
        name="moe_dispatch",
    )(pad_tiles, pos_blocks, h2)

    y = pl.pallas_call(
        _expert_kernel,
        grid_spec=pltpu.PrefetchScalarGridSpec(
            num_scalar_prefetch=2,
            grid=(n_tiles,),
            in_specs=[
                pl.BlockSpec((MOE_TILE * rs, LANES), lambda i, te, ts: (i, 0)),
                pl.BlockSpec(memory_space=pl.ANY),
                pl.BlockSpec(memory_space=pl.ANY),
                pl.BlockSpec(memory_space=pl.ANY),
            ],
            out_specs=pl.BlockSpec((MOE_TILE * rs, LANES), lambda i, te, ts: (i, 0)),
            scratch_shapes=[pltpu.VMEM((d, fe), F32), pltpu.VMEM((d, fe), F32),
                            pltpu.VMEM((fe, d), F32),
                            pltpu.SemaphoreType.DMA((len(_ff_chunks(fe)),))],
        ),
        out_shape=jax.ShapeDtypeStruct((rows * rs, LANES), F32),
        compiler_params=_cparams("arbitrary", vmem_limit=EXPERT_VMEM_LIMIT),
        name="moe_experts",
    )(tile_expert, tile_state, xs, w1, w3, w2)

    return pl.pallas_call(
        _combine_kernel,
        grid=(ng,),
        in_specs=[
            pl.BlockSpec((1, 1, 2 * gt), lambda i: (i, 0, 0), memory_space=pltpu.SMEM),
            pl.BlockSpec((1, 1, 2 * gt), lambda i: (jnp.minimum(i + 1, ng - 1), 0, 0),
                         memory_space=pltpu.SMEM),
            pl.BlockSpec((gt, d), lambda i: (i, 0)),
            pl.BlockSpec((gt, 2), lambda i: (i, 0)),
            pl.BlockSpec(memory_space=pl.ANY),
        ],
        out_specs=pl.BlockSpec((gt, d), lambda i: (i, 0)),
        out_shape=jax.ShapeDtypeStruct((t, d), F32),
        scratch_shapes=[pltpu.VMEM((2 * 2 * gt * rs, LANES), F32), pltpu.SemaphoreType.DMA((2,))],
        compiler_params=_cparams("arbitrary"),
        name="moe_combine",
    )(pos_blocks, pos_blocks, x2, gates, y)


def kernel(x, positions, norm_mix, norm_ffn, conv_in, conv_w, conv_out, attn_in, q_norm, k_norm,
           lam_q1, lam_k1, lam_q2, lam_k2, subln, attn_out, ffn_w1, ffn_w3, ffn_w2,
           router, moe_w1, moe_w3, moe_w2):
    batch, seq, d = x.shape
    xt = x.reshape(batch * seq, d)
    bf = lambda w: w.astype(BF16)

    xt = _layer0(xt, norm_mix[0:1], conv_in[0], conv_w[0], bf(conv_out[0]),
                 norm_ffn[0:1], ffn_w1[0], ffn_w3[0], ffn_w2[0], seq)

    lambda_init = 0.8 - 0.6 * math.exp(-0.3 * 1)
    lam_params = jnp.concatenate([lam_q1, lam_k1, lam_q2, lam_k2], axis=0)
    x2, h2, idx, gates = _diff_attention_and_router(
        xt, positions, norm_mix[1:2], attn_in[0], q_norm[0], k_norm[0], lam_params, subln[0],
        bf(attn_out[0]), norm_ffn[1:2], router[0], lambda_init, batch, seq)
    out = _moe(x2, h2, idx, gates, moe_w1[0], moe_w3[0], moe_w2[0])
    return out.reshape(batch, seq, d)
```

```python
import functools
import math

import jax
import jax.numpy as jnp
from jax import lax
from jax.experimental import pallas as pl
from jax.experimental.pallas import tpu as pltpu

F32 = jnp.float32
BF16 = jnp.bfloat16

D_MODEL = 1024
N_HEADS = 8
HEAD_DIM = 64
V_DIM = 2 * HEAD_DIM
ROT_DIM = HEAD_DIM // 4
ROPE_THETA = 500000.0
N_EXPERTS = 8
RMS_EPS = 1e-6
LOG2E = 1.4426950408889634
LANES = 128
ROW_SUBLANES = 8

TOKEN_TILE = 512
ATTN_KV_TILE = 256
ATTN_Q_TILES = 2
MOE_TILE = 512
FF_CHUNK = 512
GATHER_TILE = 512
DISPATCH_RING = 3
VMEM_LIMIT = 56 * 1024 * 1024
EXPERT_VMEM_LIMIT = 62 * 1024 * 1024

_NEG_BIG = -1e30
_EXP2_SAFE_LOG2 = 60.0
_BF16_ROUNDING_MARGIN = 1.02


def _cparams(*sem, vmem_limit=VMEM_LIMIT):
    return pltpu.CompilerParams(dimension_semantics=sem, vmem_limit_bytes=vmem_limit)


def _rms_rows(x, g):
    ms = jnp.mean(x * x, axis=-1, keepdims=True)
    return x * lax.rsqrt(ms + RMS_EPS) * g


def _dot(a, b):
    return jnp.dot(a, b, preferred_element_type=F32)


def _silu(a):
    return a * (1.0 / (1.0 + jnp.exp(-a)))


def _ff_chunks(width):
    out, f0 = [], 0
    while f0 < width:
        fl = min(FF_CHUNK, width - f0)
        out.append((f0, fl))
        f0 += fl
    return out


def _conv_in_kernel(x_ref, g_ref, w_ref, b_ref, u_ref):
    d = x_ref.shape[1]
    h = _rms_rows(x_ref[...], g_ref[...])
    b_ref[...] = _dot(h, w_ref[:, 0:d]).astype(BF16)
    c = _dot(h, w_ref[:, d:2 * d])
    v = _dot(h, w_ref[:, 2 * d:3 * d])
    u_ref[...] = (c * v).astype(BF16)


def _conv_out_ffn_kernel(tiles_per_seq, u_ref, up_ref, un_ref, b_ref, x_ref, cw_ref, w_ref,
                         g_ref, w1_ref, w3_ref, w2_ref, o_ref):
    i = pl.program_id(0)
    tm = u_ref.shape[0]
    halo = up_ref.shape[0]
    u = u_ref[...].astype(F32)
    first = (i % tiles_per_seq) == 0
    last = (i % tiles_per_seq) == tiles_per_seq - 1
    prev_row = jnp.where(first, 0.0, up_ref[halo - 1:halo, :].astype(F32))
    next_row = jnp.where(last, 0.0, un_ref[0:1, :].astype(F32))
    rows = lax.broadcasted_iota(jnp.int32, (tm, 1), 0)
    u_m1 = jnp.where(rows == 0, prev_row, pltpu.roll(u, 1, 0))
    u_p1 = jnp.where(rows == tm - 1, next_row, pltpu.roll(u, tm - 1, 0))
    cw = cw_ref[...]
    conv = u_m1 * cw[0:1, :] + u * cw[1:2, :] + u_p1 * cw[2:3, :]
    y = (b_ref[...].astype(F32) * conv).astype(BF16)
    x1 = x_ref[...] + _dot(y, w_ref[...])
    h = _rms_rows(x1, g_ref[...])
    acc = x1
    for f0, fl in _ff_chunks(w1_ref.shape[1]):
        a = _dot(h, w1_ref[:, f0:f0 + fl])
        b = _dot(h, w3_ref[:, f0:f0 + fl])
        acc = acc + _dot(_silu(a) * b, w2_ref[f0:f0 + fl, :])
    o_ref[...] = acc


def _resident(shape):
    return pl.BlockSpec(shape, lambda i: (0,) * len(shape), pipeline_mode=pl.Buffered(1))


def _layer0(x, g_mix, w_in, conv_w, w_out, g_ffn, w1, w3, w2, seq):
    t, d = x.shape
    f = w1.shape[1]
    tm = TOKEN_TILE
    nt = t // tm
    halo = 16
    b, u = pl.pallas_call(
        _conv_in_kernel,
        grid=(nt,),
        in_specs=[
            pl.BlockSpec((tm, d), lambda i: (i, 0)),
            _resident((1, d)),
            _resident((d, 3 * d)),
        ],
        out_specs=[pl.BlockSpec((tm, d), lambda i: (i, 0))] * 2,
        out_shape=[jax.ShapeDtypeStruct((t, d), BF16)] * 2,
        compiler_params=_cparams("parallel"),
        name="conv_in",
    )(x, g_mix, w_in)
    hb = tm // halo
    nhb = t // halo
    return pl.pallas_call(
        functools.partial(_conv_out_ffn_kernel, seq // tm),
        grid=(nt,),
        in_specs=[
            pl.BlockSpec((tm, d), lambda i: (i, 0)),
            pl.BlockSpec((halo, d), lambda i: (jnp.maximum(i * hb - 1, 0), 0)),
            pl.BlockSpec((halo, d), lambda i: (jnp.minimum((i + 1) * hb, nhb - 1), 0)),
            pl.BlockSpec((tm, d), lambda i: (i, 0)),
            pl.BlockSpec((tm, d), lambda i: (i, 0)),
            _resident((3, d)),
            _resident((d, d)),
            _resident((1, d)),
            _resident((d, f)),
            _resident((d, f)),
            _resident((f, d)),
        ],
        out_specs=pl.BlockSpec((tm, d), lambda i: (i, 0)),
        out_shape=jax.ShapeDtypeStruct((t, d), F32),
        compiler_params=_cparams("parallel"),
        name="conv_out_ffn",
    )(u, u, u, b, x, conv_w, w_out, g_ffn, w1, w3, w2)


def _qkv_kernel(x_ref, g_ref, w_ref, pos_ref, freq_ref, qn_ref, kn_ref, q_ref, k_ref, v_ref, wt_ref):
    tm, d = x_ref.shape

    @pl.when(pl.program_id(0) == 0)
    def _():
        for j in range(3):
            wt_ref[j * d:(j + 1) * d, :] = w_ref[:, j * d:(j + 1) * d].T.astype(BF16)

    h = _rms_rows(x_ref[...], g_ref[...]).astype(BF16)
    nt = (((1,), (1,)), ((), ()))
    ang = pos_ref[...].astype(F32) * freq_ref[...]
    cos = jnp.cos(ang)[None]
    sin = jnp.sin(ang)[None]
    half = ROT_DIM // 2
    groups = d // HEAD_DIM

    def norm_rope(w_rows, gain, scale):
        t = lax.dot_general(w_rows, h, nt, preferred_element_type=F32)
        t = t.reshape(groups, HEAD_DIM, tm)
        ms = jnp.mean(t * t, axis=1, keepdims=True)
        t = t * lax.rsqrt(ms + RMS_EPS) * gain[None]
        t1 = t[:, 0:half, :]
        t2 = t[:, half:ROT_DIM, :]
        t = jnp.concatenate([t1 * cos - t2 * sin, t2 * cos + t1 * sin, t[:, ROT_DIM:, :]], axis=1)
        return (t * scale).reshape(d, tm)

    q = norm_rope(wt_ref[0:d, :], qn_ref[...], (HEAD_DIM ** -0.5) * LOG2E)
    q_ref[0] = q.astype(BF16)
    k = norm_rope(wt_ref[d:2 * d, :], kn_ref[...], 1.0)
    k_ref[...] = k.T.astype(BF16)
    v = lax.dot_general(wt_ref[2 * d:3 * d, :], h, nt, preferred_element_type=F32)
    v_ref[0] = v.astype(BF16)


def _attn_kernel(lambda_init, unshifted_ref, q_ref, k_ref, v_ref, lam_ref, sub_ref, o_ref,
                 acc_ref, p_ref):
    nqt = acc_ref.shape[0]
    ngrp = q_ref.shape[0] // nqt
    tq = q_ref.shape[2]
    nkb = v_ref.shape[0]
    tk = v_ref.shape[2]
    rows = lax.broadcasted_iota(jnp.int32, (V_DIM, 1), 0)
    zer = jnp.zeros((1, tq), F32)

    def split_q(t):
        q = q_ref[t]
        zero = jnp.zeros_like(q)
        return jnp.where(rows < HEAD_DIM, q, zero), jnp.where(rows >= HEAD_DIM, q, zero)

    def load_kv(kb):
        k = k_ref[pl.ds(pl.multiple_of(kb * tk, tk), tk), :]
        return k, v_ref[kb]

    def finish(g, t, l1, l2):
        lp = lam_ref[...]
        lam = (jnp.exp(jnp.sum(lp[0:1, :] * lp[1:2, :], axis=-1, keepdims=True))
               - jnp.exp(jnp.sum(lp[2:3, :] * lp[3:4, :], axis=-1, keepdims=True))
               + lambda_init)
        o = acc_ref[t, 0] / l1 - lam * (acc_ref[t, 1] / l2)
        ms = jnp.mean(o * o, axis=0, keepdims=True)
        o = o * lax.rsqrt(ms + RMS_EPS) * sub_ref[...] * (1.0 - lambda_init)
        o_ref[g * nqt + t] = o.astype(BF16)

    @pl.when(unshifted_ref[0] == 1)
    def _():
        ts = p_ref.shape[2]
        per = tk // ts
        nks = nkb * per

        def scores(qs, j, l, slot):
            k = k_ref[j * ts:(j + 1) * ts, :]
            ps = [jnp.exp2(_dot(k, qc)) for qc in qs]
            for c, p in enumerate(ps):
                p_ref[slot, c] = p.astype(BF16)
            return [lc + jnp.sum(p, axis=0, keepdims=True) for lc, p in zip(l, ps)]

        def values(t, j, slot):
            v = v_ref[j // per][:, (j % per) * ts:(j % per + 1) * ts]
            for c in (0, 1):
                acc_ref[t, c] += _dot(v, p_ref[slot, c])

        def group(g, carry):
            acc_ref[...] = jnp.zeros_like(acc_ref)
            stage = 0
            pending = None
            for t in range(nqt):
                qs = split_q(g * nqt + t)
                l = [zer, zer]
                for j in range(nks):
                    l = scores(qs, j, l, stage % 2)
                    if pending is not None:
                        values(*pending)
                        if pending[1] == nks - 1:
                            finish(g, pending[0], *l_done)
                    pending = (t, j, stage % 2)
                    stage += 1
                l_done = l
            values(*pending)
            finish(g, pending[0], *l_done)
            return carry

        lax.fori_loop(0, ngrp, group, 0)

    @pl.when(unshifted_ref[0] == 0)
    def _():
        def one_map(k, v, qc, m, l, t, c):
            s = _dot(k, qc)
            m_new = jnp.maximum(m, jnp.max(s, axis=0, keepdims=True))
            alpha = jnp.exp2(m - m_new)
            p = jnp.exp2(s - m_new)
            acc_ref[t, c] = alpha * acc_ref[t, c] + _dot(v, p.astype(BF16))
            return m_new, alpha * l + jnp.sum(p, axis=0, keepdims=True)

        neg = jnp.full((1, tq), _NEG_BIG, F32)

        def group(g, carry):
            acc_ref[...] = jnp.zeros_like(acc_ref)
            for t in range(nqt):
                q1, q2 = split_q(g * nqt + t)

                def body(kb, c):
                    k, v = load_kv(kb)
                    m1, l1 = one_map(k, v, q1, c[0], c[1], t, 0)
                    m2, l2 = one_map(k, v, q2, c[2], c[3], t, 1)
                    return m1, l1, m2, l2

                _, l1, _, l2 = lax.fori_loop(0, nkb, body, (neg, zer, neg, zer))
                finish(g, t, l1, l2)
            return carry

        lax.fori_loop(0, ngrp, group, 0)


def _store_row_tiles(ref, x):
    rows = x.shape[0]
    for j in range(ROW_SUBLANES):
        ref[pl.ds(j, rows, stride=ROW_SUBLANES), :] = x[:, j * LANES:(j + 1) * LANES]


def _load_row_tiles(ref, base, rows):
    return jnp.concatenate(
        [ref[pl.ds(base + j, rows, stride=ROW_SUBLANES), :] for j in range(ROW_SUBLANES)], axis=1)


def _attn_out_router_kernel(o_ref, x_ref, w_ref, g_ref, rt_ref, x2_ref, h2_ref, idx_ref, gate_ref):
    tn = (((0,), (0,)), ((), ()))
    y = lax.dot_general(o_ref[0], w_ref[...], tn, preferred_element_type=F32)
    x2 = x_ref[...] + y
    x2_ref[...] = x2
    h2 = _rms_rows(x2, g_ref[...])
    _store_row_tiles(h2_ref, h2)
    ne = rt_ref.shape[0]
    nt = (((1,), (1,)), ((), ()))
    r = rt_ref[...]
    r_hi = r.astype(BF16)
    r_lo = (r - r_hi.astype(F32)).astype(BF16)
    h_hi = h2.astype(BF16)
    h_lo = (h2 - h_hi.astype(F32)).astype(BF16)
    both = lax.dot_general(jnp.concatenate([r_hi, r_lo], axis=0), h_hi, nt, preferred_element_type=F32)
    logits = (both[0:ne] + both[ne:2 * ne]
              + lax.dot_general(r_hi, h_lo, nt, preferred_element_type=F32))
    eid = lax.broadcasted_iota(jnp.int32, logits.shape, 0)
    m1 = jnp.max(logits, axis=0, keepdims=True)
    i1 = jnp.min(jnp.where(logits == m1, eid, ne), axis=0, keepdims=True)
    rest = jnp.where(eid == i1, -jnp.inf, logits)
    m2 = jnp.max(rest, axis=0, keepdims=True)
    i2 = jnp.min(jnp.where(rest == m2, eid, ne), axis=0, keepdims=True)
    e2 = jnp.exp(m2 - m1)
    den = 1.0 + e2
    idx_ref[0:1, :] = i1
    idx_ref[1:2, :] = i2
    gate_ref[0:1, :] = 1.0 / den
    gate_ref[1:2, :] = e2 / den


def _diff_attention_and_router(x, positions, g_mix, w_in, q_norm, k_norm, lam_params, subln, w_out,
                               g_ffn, router, lambda_init, batch, seq):
    t, d = x.shape
    tm = TOKEN_TILE
    nt = t // tm
    inv_freq = ROPE_THETA ** (-jnp.arange(0, ROT_DIM, 2, dtype=F32) / ROT_DIM)
    q, k, v = pl.pallas_call(
        _qkv_kernel,
        grid=(nt,),
        in_specs=[
            pl.BlockSpec((tm, d), lambda i: (i, 0)),
            _resident((1, d)),
            _resident((d, 3 * d)),
            pl.BlockSpec((1, tm), lambda i: (0, i)),
            _resident((ROT_DIM // 2, 1)),
            _resident((HEAD_DIM, 1)),
            _resident((HEAD_DIM, 1)),
        ],
        out_specs=[
            pl.BlockSpec((1, d, tm), lambda i: (i, 0, 0)),
            pl.BlockSpec((tm, d), lambda i: (i, 0)),
            pl.BlockSpec((1, d, tm), lambda i: (i, 0, 0)),
        ],
        out_shape=[
            jax.ShapeDtypeStruct((nt, d, tm), BF16),
            jax.ShapeDtypeStruct((t, d), BF16),
            jax.ShapeDtypeStruct((nt, d, tm), BF16),
        ],
        scratch_shapes=[pltpu.VMEM((3 * d, d), BF16)],
        compiler_params=_cparams("arbitrary"),
        name="attn_qkv",
    )(x, g_mix, w_in, positions.reshape(1, t), inv_freq.reshape(-1, 1),
      q_norm.reshape(-1, 1), k_norm.reshape(-1, 1))

    nq = seq // tm
    qg = ATTN_Q_TILES
    score_bound = (HEAD_DIM ** 0.5 * LOG2E * _BF16_ROUNDING_MARGIN
                   * jnp.max(jnp.abs(q_norm)) * jnp.max(jnp.abs(k_norm)))
    unshifted = (score_bound <= _EXP2_SAFE_LOG2).astype(jnp.int32).reshape(1)
    o = pl.pallas_call(
        functools.partial(_attn_kernel, lambda_init),
        grid_spec=pltpu.PrefetchScalarGridSpec(
            num_scalar_prefetch=1,
            grid=(batch, N_HEADS),
            in_specs=[
                pl.BlockSpec((nq, V_DIM, tm), lambda b, h, f: (b, h, 0)),
                pl.BlockSpec((seq, V_DIM), lambda b, h, f: (b, h)),
                pl.BlockSpec((nq, V_DIM, tm), lambda b, h, f: (b, h, 0)),
                pl.BlockSpec((4, HEAD_DIM), lambda b, h, f: (0, 0)),
                pl.BlockSpec((V_DIM, 1), lambda b, h, f: (0, 0)),
            ],
            out_specs=pl.BlockSpec((nq, V_DIM, tm), lambda b, h, f: (b, h, 0)),
            scratch_shapes=[pltpu.VMEM((qg, 2, V_DIM, tm), F32),
                            pltpu.VMEM((2, 2, ATTN_KV_TILE, tm), BF16)],
        ),
        out_shape=jax.ShapeDtypeStruct((nt, d, tm), BF16),
        compiler_params=_cparams("parallel", "parallel"),
        name="diff_attn",
    )(unshifted, q, k, v, lam_params, subln.reshape(-1, 1))

    x2, h2, idx, gates = pl.pallas_call(
        _attn_out_router_kernel,
        grid=(nt,),
        in_specs=[
            pl.BlockSpec((1, d, tm), lambda i: (i, 0, 0)),
            pl.BlockSpec((tm, d), lambda i: (i, 0)),
            _resident((d, d)),
            _resident((1, d)),
            _resident((N_EXPERTS, d)),
        ],
        out_specs=[
            pl.BlockSpec((tm, d), lambda i: (i, 0)),
            pl.BlockSpec((tm * ROW_SUBLANES, LANES), lambda i: (i, 0)),
            pl.BlockSpec((2, tm), lambda i: (0, i)),
            pl.BlockSpec((2, tm), lambda i: (0, i)),
        ],
        out_shape=[
            jax.ShapeDtypeStruct((t, d), F32),
            jax.ShapeDtypeStruct((t * ROW_SUBLANES, LANES), F32),
            jax.ShapeDtypeStruct((2, t), jnp.int32),
            jax.ShapeDtypeStruct((2, t), F32),
        ],
        compiler_params=_cparams("parallel"),
        name="attn_out_router",
    )(o, x, w_out, g_ffn, router.T)
    return x2, h2, idx, gates.T


def _routing_plan(idx, n_tiles):
    experts = jnp.arange(N_EXPERTS, dtype=jnp.int32)[:, None]
    chosen = [(idx[c][None, :] == experts).astype(jnp.int32) for c in (0, 1)]
    hot = chosen[0] + chosen[1]
    incl = jnp.cumsum(hot, axis=1)
    rank = incl - hot
    tiles = (incl[:, -1] + MOE_TILE - 1) // MOE_TILE
    tile_end = jnp.cumsum(tiles)
    row_start = (tile_end - tiles) * MOE_TILE
    slot = row_start[:, None] + rank
    pos = jnp.stack([jnp.sum(slot * one_hot, axis=0) for one_hot in chosen])
    tile_id = jnp.arange(n_tiles, dtype=jnp.int32)
    tile_expert = jnp.minimum(jnp.sum(tile_id[:, None] >= tile_end[None, :], axis=1), N_EXPERTS - 1)
    new_expert = jnp.concatenate([jnp.ones((1,), bool), tile_expert[1:] != tile_expert[:-1]])
    of_expert = (tile_expert[:, None] == experts.T).astype(jnp.int32)
    rows_left = (jnp.sum(of_expert * incl[:, -1][None, :], axis=1)
                 - (tile_id - jnp.sum(of_expert * (tile_end - tiles)[None, :], axis=1)) * MOE_TILE)
    half_full = (~new_expert) & (rows_left <= MOE_TILE // 2)
    tile_state = jnp.where(tile_id < tile_end[-1],
                           jnp.where(half_full, 3, 1 + new_expert.astype(jnp.int32)), 0)
    last_tile = jnp.where(tiles > 0, tile_end - 1, -1)
    tail = tile_end[-1] + jnp.arange(N_EXPERTS, dtype=jnp.int32)
    pad_tiles = jnp.concatenate([last_tile, jnp.where(tail < n_tiles, tail, -1)])
    return (pos.astype(jnp.int32), tile_expert.astype(jnp.int32), tile_state.astype(jnp.int32),
            pad_tiles.astype(jnp.int32))


def _row_tile(ref, row):
    return ref.at[pl.ds(pl.multiple_of(row * ROW_SUBLANES, ROW_SUBLANES), ROW_SUBLANES)]


def _dispatch_kernel(pad_ref, pos_ref, h_hbm, xs_hbm, zero_ref, ring, load_sem, row_sem):
    i = pl.program_id(0)
    last = pl.num_programs(0) - 1
    n = pos_ref.shape[2] // 2
    mt = zero_ref.shape[0]
    nslot = ring.shape[0]
    block_rows = n * ROW_SUBLANES

    def load(step):
        src = h_hbm.at[pl.ds(pl.multiple_of(step * block_rows, block_rows), block_rows)]
        return pltpu.make_async_copy(src, ring.at[step % nslot], load_sem.at[step % nslot])

    def wait_rows(step):
        for _ in (0, 1):
            pltpu.make_async_copy(ring.at[0], xs_hbm.at[pl.ds(0, block_rows)],
                                  row_sem.at[step % nslot]).wait()

    @pl.when(i == 0)
    def _():
        load(0).start()
        zero_ref[...] = jnp.zeros_like(zero_ref)
        fills = [pltpu.make_async_copy(
            zero_ref, xs_hbm.at[pl.ds(pl.multiple_of(jnp.maximum(pad_ref[j], 0) * mt, mt), mt)],
            row_sem.at[0]) for j in range(pad_ref.shape[0])]
        for j, fill in enumerate(fills):
            pl.when(pad_ref[j] >= 0)(fill.start)
        for j, fill in enumerate(fills):
            pl.when(pad_ref[j] >= 0)(fill.wait)

    pl.when(i < last)(lambda: load(i + 1).start())
    load(i).wait()
    block = ring.at[i % nslot]

    def issue(r, carry):
        for c in (0, 1):
            pltpu.make_async_copy(_row_tile(block, r), _row_tile(xs_hbm, pos_ref[0, 0, c * n + r]),
                                  row_sem.at[i % nslot]).start(priority=c)
        return carry

    lax.fori_loop(0, n, issue, 0, unroll=8)
    pl.when(i > 0)(lambda: wait_rows(i - 1))
    pl.when(i == last)(lambda: wait_rows(i))


def _expert_kernel(te_ref, ts_ref, xs_ref, w1_hbm, w3_hbm, w2_hbm, y_ref, w1_ref, w3_ref, w2_ref, sem):
    i = pl.program_id(0)
    mt = xs_ref.shape[0] // ROW_SUBLANES
    chunks = _ff_chunks(w1_ref.shape[1])

    def swiglu_tile(before_chunk, rows=mt):
        x = _load_row_tiles(xs_ref, 0, rows)
        acc = None
        for c, (f0, fl) in enumerate(chunks):
            before_chunk(c)
            a = _dot(x, w1_ref[:, f0:f0 + fl])
            b = _dot(x, w3_ref[:, f0:f0 + fl])
            act = _silu(a) * b
            part = _dot(act, w2_ref[f0:f0 + fl, :])
            acc = part if acc is None else acc + part
        _store_row_tiles(y_ref.at[0:rows * ROW_SUBLANES], acc)
        if rows < mt:
            y_ref[rows * ROW_SUBLANES:, :] = jnp.zeros(((mt - rows) * ROW_SUBLANES, LANES), F32)

    @pl.when(ts_ref[i] == 2)
    def _():
        e = te_ref[i]
        fetch = []
        for c, (f0, fl) in enumerate(chunks):
            cols = pl.ds(f0, fl)
            fetch.append([
                pltpu.make_async_copy(w1_hbm.at[e, :, cols], w1_ref.at[:, cols], sem.at[c]),
                pltpu.make_async_copy(w3_hbm.at[e, :, cols], w3_ref.at[:, cols], sem.at[c]),
                pltpu.make_async_copy(w2_hbm.at[e, cols, :], w2_ref.at[cols, :], sem.at[c]),
            ])
        for copies in fetch:
            for copy in copies:
                copy.start()

        def wait_chunk(c):
            for copy in fetch[c]:
                copy.wait()

        swiglu_tile(wait_chunk)

    @pl.when(ts_ref[i] == 1)
    def _():
        swiglu_tile(lambda c: None)

    @pl.when(ts_ref[i] == 3)
    def _():
        swiglu_tile(lambda c: None, rows=mt // 2)

    @pl.when(ts_ref[i] == 0)
    def _():
        y_ref[...] = jnp.zeros_like(y_ref)


def _combine_kernel(pos_ref, pos_next_ref, x_ref, g_ref, y_hbm, o_ref, buf, sem):
    i = pl.program_id(0)
    n = x_ref.shape[0]

    def gather(p_ref, slot):
        def issue(r, carry):
            for c in (0, 1):
                pltpu.make_async_copy(_row_tile(y_hbm, p_ref[0, 0, c * n + r]),
                                      _row_tile(buf, (slot * 2 + c) * n + r),
                                      sem.at[slot]).start(priority=c)
            return carry
        lax.fori_loop(0, n, issue, 0, unroll=8)

    @pl.when(i == 0)
    def _():
        gather(pos_ref, 0)

    @pl.when(i + 1 < pl.num_programs(0))
    def _():
        gather(pos_next_ref, (i + 1) % 2)

    slot = i % 2
    chunk = n * ROW_SUBLANES
    base = pl.multiple_of(slot * 2 * chunk, chunk)
    pltpu.make_async_copy(y_hbm.at[pl.ds(0, 2 * chunk)], buf.at[pl.ds(base, 2 * chunk)],
                          sem.at[slot]).wait()
    g = g_ref[...]
    o_ref[...] = (x_ref[...] + g[:, 0:1] * _load_row_tiles(buf, base, n)
                  + g[:, 1:2] * _load_row_tiles(buf, base + chunk, n))


def _moe(x2, h2, idx, gates, w1, w3, w2):
    t, d = x2.shape
    fe = w1.shape[2]
    assert d == ROW_SUBLANES * LANES
    rs = ROW_SUBLANES
    n_tiles = (2 * t) // MOE_TILE + N_EXPERTS
    rows = n_tiles * MOE_TILE
    pos, tile_expert, tile_state, pad_tiles = _routing_plan(idx, n_tiles)
    gt = GATHER_TILE
    ng = t // gt
    pos_blocks = pos.reshape(2, ng, gt).transpose(1, 0, 2).reshape(ng, 1, 2 * gt)

    xs = pl.pallas_call(
        _dispatch_kernel,
        grid_spec=pltpu.PrefetchScalarGridSpec(
            num_scalar_prefetch=1,
            grid=(ng,),
            in_specs=[
                pl.BlockSpec((1, 1, 2 * gt), lambda i, pad: (i, 0, 0), memory_space=pltpu.SMEM),
                pl.BlockSpec(memory_space=pl.ANY),
            ],
            out_specs=pl.BlockSpec(memory_space=pl.ANY),
            scratch_shapes=[pltpu.VMEM((MOE_TILE * rs, LANES), F32),
                            pltpu.VMEM((DISPATCH_RING, gt * rs, LANES), F32),
                            pltpu.SemaphoreType.DMA((DISPATCH_RING,)),
                            pltpu.SemaphoreType.DMA((DISPATCH_RING,))],
        ),
        out_shape=jax.ShapeDtypeStruct((rows * rs, LANES), F32),
        compiler_params=_cparams("arbitrary"),
        name="moe_dispatch",
    )(pad_tiles, pos_blocks, h2)

    y = pl.pallas_call(
        _expert_kernel,
        grid_spec=pltpu.PrefetchScalarGridSpec(
            num_scalar_prefetch=2,
            grid=(n_tiles,),
            in_specs=[
                pl.BlockSpec((MOE_TILE * rs, LANES), lambda i, te, ts: (i, 0)),
                pl.BlockSpec(memory_space=pl.ANY),
                pl.BlockSpec(memory_space=pl.ANY),
                pl.BlockSpec(memory_space=pl.ANY),
            ],
            out_specs=pl.BlockSpec((MOE_TILE * rs, LANES), lambda i, te, ts: (i, 0)),
            scratch_shapes=[pltpu.VMEM((d, fe), F32), pltpu.VMEM((d, fe), F32),
                            pltpu.VMEM((fe, d), F32),
                            pltpu.SemaphoreType.DMA((len(_ff_chunks(fe)),))],
        ),
        out_shape=jax.ShapeDtypeStruct((rows * rs, LANES), F32),
        compiler_params=_cparams("arbitrary", vmem_limit=EXPERT_VMEM_LIMIT),
        name="moe_experts",
    )(tile_expert, tile_state, xs, w1, w3, w2)

    return pl.pallas_call(
        _combine_kernel,
        grid=(ng,),
        in_specs=[
            pl.BlockSpec((1, 1, 2 * gt), lambda i: (i, 0, 0), memory_space=pltpu.SMEM),
            pl.BlockSpec((1, 1, 2 * gt), lambda i: (jnp.minimum(i + 1, ng - 1), 0, 0),
                         memory_space=pltpu.SMEM),
            pl.BlockSpec((gt, d), lambda i: (i, 0)),
            pl.BlockSpec((gt, 2), lambda i: (i, 0)),
            pl.BlockSpec(memory_space=pl.ANY),
        ],
        out_specs=pl.BlockSpec((gt, d), lambda i: (i, 0)),
        out_shape=jax.ShapeDtypeStruct((t, d), F32),
        scratch_shapes=[pltpu.VMEM((2 * 2 * gt * rs, LANES), F32), pltpu.SemaphoreType.DMA((2,))],
        compiler_params=_cparams("arbitrary"),
        name="moe_combine",
    )(pos_blocks, pos_blocks, x2, gates, y)


def kernel(x, positions, norm_mix, norm_ffn, conv_in, conv_w, conv_out, attn_in, q_norm, k_norm,
           lam_q1, lam_k1, lam_q2, lam_k2, subln, attn_out, ffn_w1, ffn_w3, ffn_w2,
           router, moe_w1, moe_w3, moe_w2):
    batch, seq, d = x.shape
    xt = x.reshape(batch * seq, d)
    bf = lambda w: w.astype(BF16)

    xt = _layer0(xt, norm_mix[0:1], conv_in[0], conv_w[0], bf(conv_out[0]),
                 norm_ffn[0:1], ffn_w1[0], ffn_w3[0], ffn_w2[0], seq)

    lambda_init = 0.8 - 0.6 * math.exp(-0.3 * 1)
    lam_params = jnp.concatenate([lam_q1, lam_k1, lam_q2, lam_k2], axis=0)
    x2, h2, idx, gates = _diff_attention_and_router(
        xt, positions, norm_mix[1:2], attn_in[0], q_norm[0], k_norm[0], lam_params, subln[0],
        bf(attn_out[0]), norm_ffn[1:2], router[0], lambda_init, batch, seq)
    out = _moe(x2, h2, idx, gates, moe_w1[0], moe_w3[0], moe_w2[0])
    return out.reshape(batch, seq, d)
```
